```python
import math
import jax
import jax.numpy as jnp
from jax import lax
import numpy as np


D_MODEL = 4096
BATCH = 4
SEQ = 4096
DEPTH = 1

MIX_WIDTH = D_MODEL
ATTN_WIDTH = MIX_WIDTH // 2
SSM_WIDTH = MIX_WIDTH - ATTN_WIDTH
ATTN_HEAD_DIM = 128
ATTN_HEADS = ATTN_WIDTH // (2 * ATTN_HEAD_DIM)
Q_BLOCK = 128
NUM_BUCKETS = 32
MAX_DISTANCE = 128
SSM_HEAD_DIM = 64
SSM_HEADS = SSM_WIDTH // SSM_HEAD_DIM
SSM_GROUPS = 8
SSM_STATE = 128
CONV_WIDTH = 5
CONV_CHANNELS = SSM_WIDTH + 2 * SSM_GROUPS * SSM_STATE
CHUNK = 128
IN_PROJ_WIDTH = 3 * ATTN_WIDTH + SSM_WIDTH + CONV_CHANNELS + 2 * SSM_HEADS
PROJ_SPLITS = (ATTN_WIDTH, 2 * ATTN_WIDTH, 3 * ATTN_WIDTH, 3 * ATTN_WIDTH + SSM_WIDTH,
               3 * ATTN_WIDTH + SSM_WIDTH + CONV_CHANNELS,
               3 * ATTN_WIDTH + SSM_WIDTH + CONV_CHANNELS + SSM_HEADS)
N_EXPERT_GROUPS = 4
EXPERTS_PER_GROUP = 8
N_EXPERTS = N_EXPERT_GROUPS * EXPERTS_PER_GROUP
TOP_K = 2
D_FF = D_MODEL // 4
EXPERT_BLOCK = 128
EPS = 1e-6

kernel_name = 'hymba_diffattn_ssd_hmoe_encoder'


def lambda_init_at(layer):
    return 0.8 - 0.6 * math.exp(-0.3 * layer)


def rmsnorm(x, w):
    xf = x.astype(jnp.float32)
    y = xf * lax.rsqrt(jnp.mean(xf * xf, axis=-1, keepdims=True) + EPS)
    return (y * w.astype(jnp.float32)).astype(x.dtype)


def t5_bucket(rel):
    half = NUM_BUCKETS // 2
    max_exact = half // 2
    n = jnp.abs(rel)
    large = max_exact + (jnp.log(jnp.maximum(n, 1).astype(jnp.float32) / max_exact)
                         / math.log(MAX_DISTANCE / max_exact) * (half - max_exact)).astype(jnp.int32)
    large = jnp.minimum(large, half - 1)
    return jnp.where(rel > 0, half, 0) + jnp.where(n < max_exact, n, large)


def diff_attention(q, k, v, rel_bias, lam, subln_w, lambda_init):
    b, s, h, _, dh = q.shape
    nqb = s // Q_BLOCK
    q_blocks = jnp.moveaxis(q.reshape(b, nqb, Q_BLOCK, h, 2, dh), 1, 0)
    k_pos = jnp.arange(s, dtype=jnp.int32)
    scale = dh ** -0.5

    def one_block(args):
        q_blk, start = args
        logits = jnp.einsum('bqhmd,bkhmd->bhmqk', q_blk, k,
                            preferred_element_type=jnp.float32) * scale
        q_pos = start + jnp.arange(Q_BLOCK, dtype=jnp.int32)
        bias = rel_bias[t5_bucket(k_pos[None, :] - q_pos[:, None])].astype(jnp.float32)
        logits = logits + jnp.transpose(bias, (2, 0, 1))[None, :, None]
        p = jax.nn.softmax(logits, axis=-1)
        a = p[:, :, 0] - lam * p[:, :, 1]
        return jnp.einsum('bhqk,bkhe->bqhe', a.astype(v.dtype), v)

    starts = jnp.arange(nqb, dtype=jnp.int32) * Q_BLOCK
    out = lax.map(one_block, (q_blocks, starts))
    out = jnp.moveaxis(out, 0, 1).reshape(b, s, h, 2 * dh)
    out = rmsnorm(out, subln_w) * (1.0 - lambda_init)
    return out.reshape(b, s, h * 2 * dh)


def centred_dwconv(u, w, bias):
    y = lax.conv_general_dilated(u, w[:, None, :].astype(u.dtype), window_strides=(1,),
                                 padding=[(CONV_WIDTH // 2, CONV_WIDTH // 2)],
                                 dimension_numbers=('NWC', 'WIO', 'NWC'),
                                 feature_group_count=u.shape[-1])
    return y + bias.astype(u.dtype)


def ssd_chunked(xs, a, bm, cm):
    b, s, h, p = xs.shape
    g, n = bm.shape[2], bm.shape[3]
    r = h // g
    nc = s // CHUNK
    xc = xs.reshape(b, nc, CHUNK, g, r, p)
    ac = jnp.transpose(a.reshape(b, nc, CHUNK, g, r), (0, 3, 4, 1, 2))
    bc = bm.reshape(b, nc, CHUNK, g, n)
    cc = cm.reshape(b, nc, CHUNK, g, n)
    a_cs = jnp.cumsum(ac, axis=-1)
    lower = jnp.tril(jnp.ones((CHUNK, CHUNK), dtype=bool))
    decay_in = jnp.exp(jnp.where(lower, a_cs[..., :, None] - a_cs[..., None, :], -jnp.inf))
    cb = jnp.einsum('bclgn,bcsgn->bcgls', cc, bc)
    y_diag = jnp.einsum('bcgls,bgrcls,bcsgrp->bclgrp', cb, decay_in, xc)
    decay_to_end = jnp.exp(a_cs[..., -1:] - a_cs)
    states = jnp.einsum('bclgn,bgrcl,bclgrp->bcgrpn', bc, decay_to_end, xc)
    chunk_decay = jnp.exp(a_cs[..., -1])

    def carry_state(state, inp):
        s_c, d_c = inp
        return state * d_c[..., None, None] + s_c, state

    init = jnp.zeros((b, g, r, p, n), xs.dtype)
    _, prev = lax.scan(carry_state, init,
                       (jnp.moveaxis(states, 1, 0), jnp.moveaxis(chunk_decay, -1, 0)))
    y_off = jnp.einsum('bclgn,cbgrpn,bgrcl->bclgrp', cc, prev, jnp.exp(a_cs))
    return (y_diag + y_off).reshape(b, s, h, p)


def ssd_direction(xs, bm, cm, dt_raw, dt_bias, a_log):
    dt = jax.nn.softplus(dt_raw.astype(jnp.float32) + dt_bias.astype(jnp.float32))
    a = -jnp.exp(a_log.astype(jnp.float32))
    return ssd_chunked(xs * dt[..., None], a * dt, bm, cm)


def flip_seq(t):
    return jnp.flip(t, axis=1)


def ssd_mixer(z, xbc, dt_f, dt_b, conv_w, conv_b, dt_bias_f, dt_bias_b, a_log_f, a_log_b, d_skip, norm_w):
    b, s, _ = xbc.shape
    xbc = jax.nn.silu(centred_dwconv(xbc, conv_w, conv_b)).astype(jnp.float32)
    gn = SSM_GROUPS * SSM_STATE
    xs = xbc[..., :SSM_WIDTH].reshape(b, s, SSM_HEADS, SSM_HEAD_DIM)
    bm = xbc[..., SSM_WIDTH:SSM_WIDTH + gn].reshape(b, s, SSM_GROUPS, SSM_STATE)
    cm = xbc[..., SSM_WIDTH + gn:].reshape(b, s, SSM_GROUPS, SSM_STATE)
    y_fwd = ssd_direction(xs, bm, cm, dt_f, dt_bias_f, a_log_f)
    y_bwd = flip_seq(ssd_direction(flip_seq(xs), flip_seq(bm), flip_seq(cm), flip_seq(dt_b), dt_bias_b, a_log_b))
    y = y_fwd + y_bwd + d_skip.astype(jnp.float32)[:, None] * xs
    gw = SSM_WIDTH // SSM_GROUPS
    y = y.reshape(b, s, SSM_GROUPS, gw) * jax.nn.silu(z.astype(jnp.float32)).reshape(b, s, SSM_GROUPS, gw)
    y = y * lax.rsqrt(jnp.mean(y * y, axis=-1, keepdims=True) + EPS)
    return (y.reshape(b, s, SSM_WIDTH) * norm_w.astype(jnp.float32)).astype(z.dtype)


def hierarchical_moe(h, w_group_router, b_group_router, w_expert_router, b_expert_router, w_gate, w_up, w_down):
    b, s, d = h.shape
    t = b * s
    hf = h.reshape(t, d)
    tok = jnp.arange(t)
    g_prob = jax.nn.softmax((hf @ w_group_router).astype(jnp.float32) + b_group_router.astype(jnp.float32), axis=-1)
    g_sel = jnp.argmax(g_prob, axis=-1)
    p_group = jnp.max(g_prob, axis=-1)
    e_logits = ((hf @ w_expert_router).astype(jnp.float32) + b_expert_router.astype(jnp.float32)).reshape(
        t, N_EXPERT_GROUPS, EXPERTS_PER_GROUP)
    e_prob = jax.nn.softmax(e_logits[tok, g_sel], axis=-1)
    top_p, top_i = lax.top_k(e_prob, TOP_K)
    gate = p_group[:, None] * top_p / jnp.sum(top_p, axis=-1, keepdims=True)
    expert = g_sel[:, None] * EXPERTS_PER_GROUP + top_i
    n = t * TOP_K
    flat_e = expert.reshape(n).astype(jnp.int32)
    order = jnp.argsort(flat_e)
    se = flat_e[order]
    stok = (order // TOP_K).astype(jnp.int32)
    sgate = gate.reshape(n)[order]
    counts = jnp.bincount(flat_e, length=N_EXPERTS)
    starts = jnp.cumsum(counts) - counts
    padded = (counts + EXPERT_BLOCK - 1) // EXPERT_BLOCK * EXPERT_BLOCK
    padded_end = jnp.cumsum(padded)
    dest = padded_end[se] - padded[se] + jnp.arange(n) - starts[se]
    n_blocks = -(-n // EXPERT_BLOCK) + N_EXPERTS
    rows = n_blocks * EXPERT_BLOCK
    row_tok = jnp.zeros((rows,), jnp.int32).at[dest].set(stok)
    row_gate = jnp.zeros((rows,), jnp.float32).at[dest].set(sgate)
    block_expert = jnp.minimum(
        jnp.searchsorted(padded_end, jnp.arange(n_blocks) * EXPERT_BLOCK, side='right'), N_EXPERTS - 1)

    def expert_block(args):
        e, toks, gts = args
        xb = hf[toks]
        act = jax.nn.silu(xb @ w_gate[e]) * (xb @ w_up[e])
        return (act @ w_down[e]) * gts[:, None].astype(hf.dtype)

    y = lax.map(expert_block, (block_expert, row_tok.reshape(n_blocks, EXPERT_BLOCK),
                               row_gate.reshape(n_blocks, EXPERT_BLOCK)))
    out = jnp.zeros((t, d), hf.dtype).at[row_tok].add(y.reshape(rows, d))
    return out.reshape(b, s, d)


def setup_inputs(seed: int = 0) -> dict:
    key = jax.random.key(seed)
    ks = jax.random.split(key, 32)
    f32 = jnp.float32

    def nrm(k, shape, scale):
        return jax.random.normal(k, shape, f32) * scale

    L = DEPTH
    u = jax.random.uniform(ks[10], (L, SSM_HEADS), f32)
    dt0 = jnp.exp(u * (math.log(0.1) - math.log(0.001)) + math.log(0.001))
    u2 = jax.random.uniform(ks[11], (L, SSM_HEADS), f32)
    dt1 = jnp.exp(u2 * (math.log(0.1) - math.log(0.001)) + math.log(0.001))
    return {
        'x': nrm(ks[0], (BATCH, SEQ, D_MODEL), 1.0),
        'rel_bias': nrm(ks[1], (NUM_BUCKETS, ATTN_HEADS), 0.5),
        'norm1_w': 1.0 + nrm(ks[2], (L, D_MODEL), 0.02),
        'w_in': nrm(ks[3], (L, D_MODEL, IN_PROJ_WIDTH), D_MODEL ** -0.5),
        'lambda_q1': nrm(ks[4], (L, ATTN_HEAD_DIM), 0.1),
        'lambda_k1': nrm(ks[5], (L, ATTN_HEAD_DIM), 0.1),
        'lambda_q2': nrm(ks[6], (L, ATTN_HEAD_DIM), 0.1),
        'lambda_k2': nrm(ks[7], (L, ATTN_HEAD_DIM), 0.1),
        'subln_w': 1.0 + nrm(ks[8], (L, 2 * ATTN_HEAD_DIM), 0.02),
        'conv_w': nrm(ks[9], (L, CONV_WIDTH, CONV_CHANNELS), CONV_WIDTH ** -0.5),
        'conv_b': nrm(ks[12], (L, CONV_CHANNELS), 0.01),
        'dt_bias_f': dt0 + jnp.log(-jnp.expm1(-dt0)),
        'dt_bias_b': dt1 + jnp.log(-jnp.expm1(-dt1)),
        'a_log_f': jnp.log(jax.random.uniform(ks[13], (L, SSM_HEADS), f32, 1.0, 16.0)),
        'a_log_b': jnp.log(jax.random.uniform(ks[14], (L, SSM_HEADS), f32, 1.0, 16.0)),
        'd_skip': 1.0 + nrm(ks[15], (L, SSM_HEADS), 0.02),
        'ssm_norm_w': 1.0 + nrm(ks[16], (L, SSM_WIDTH), 0.02),
        'w_out': nrm(ks[17], (L, MIX_WIDTH, D_MODEL), MIX_WIDTH ** -0.5),
        'norm2_w': 1.0 + nrm(ks[18], (L, D_MODEL), 0.02),
        'w_group_router': nrm(ks[19], (L, D_MODEL, N_EXPERT_GROUPS), D_MODEL ** -0.5),
        'b_group_router': nrm(ks[20], (L, N_EXPERT_GROUPS), 0.01),
        'w_expert_router': nrm(ks[21], (L, D_MODEL, N_EXPERTS), D_MODEL ** -0.5),
        'b_expert_router': nrm(ks[22], (L, N_EXPERTS), 0.01),
        'w_gate': nrm(ks[23], (L, N_EXPERTS, D_MODEL, D_FF), D_MODEL ** -0.5),
        'w_up': nrm(ks[24], (L, N_EXPERTS, D_MODEL, D_FF), D_MODEL ** -0.5),
        'w_down': nrm(ks[25], (L, N_EXPERTS, D_FF, D_MODEL), D_FF ** -0.5),
        'final_norm_w': 1.0 + nrm(ks[26], (D_MODEL,), 0.02),
    }


def reference(x, rel_bias, norm1_w, w_in, lambda_q1, lambda_k1, lambda_q2, lambda_k2, subln_w, conv_w, conv_b,
              dt_bias_f, dt_bias_b, a_log_f, a_log_b, d_skip, ssm_norm_w, w_out, norm2_w, w_group_router,
              b_group_router, w_expert_router, b_expert_router, w_gate, w_up, w_down, final_norm_w):
    b, s, _ = x.shape
    for layer in range(DEPTH):
        h = rmsnorm(x, norm1_w[layer])
        proj = h @ w_in[layer]
        q, k, v, z, xbc, dt_f, dt_b = jnp.split(proj, PROJ_SPLITS, axis=-1)
        lam_init = lambda_init_at(layer)
        lam = (jnp.exp(jnp.sum(lambda_q1[layer].astype(jnp.float32) * lambda_k1[layer].astype(jnp.float32)))
               - jnp.exp(jnp.sum(lambda_q2[layer].astype(jnp.float32) * lambda_k2[layer].astype(jnp.float32)))
               + lam_init)
        attn_out = diff_attention(q.reshape(b, s, ATTN_HEADS, 2, ATTN_HEAD_DIM),
                                  k.reshape(b, s, ATTN_HEADS, 2, ATTN_HEAD_DIM),
                                  v.reshape(b, s, ATTN_HEADS, 2 * ATTN_HEAD_DIM),
                                  rel_bias, lam, subln_w[layer], lam_init)
        ssm_out = ssd_mixer(z, xbc, dt_f, dt_b, conv_w[layer], conv_b[layer], dt_bias_f[layer], dt_bias_b[layer],
                            a_log_f[layer], a_log_b[layer], d_skip[layer], ssm_norm_w[layer])
        x = x + jnp.concatenate([attn_out, ssm_out], axis=-1) @ w_out[layer]
        x = x + hierarchical_moe(rmsnorm(x, norm2_w[layer]), w_group_router[layer], b_group_router[layer],
                                 w_expert_router[layer], b_expert_router[layer], w_gate[layer], w_up[layer],
                                 w_down[layer])
    return rmsnorm(x, final_norm_w)
```

```python
import functools
import math

import jax
import jax.numpy as jnp
from jax import lax
from jax.experimental import pallas as pl
from jax.experimental.pallas import tpu as pltpu

F32 = jnp.float32
BF16 = jnp.bfloat16
EPS = 1e-6

ATTN_HEAD_DIM = 128
NUM_BUCKETS = 32
MAX_DISTANCE = 128
SSM_HEAD_DIM = 64
SSM_GROUPS = 8
SSM_STATE = 128
CONV_WIDTH = 5
CHUNK = 128
N_EXPERT_GROUPS = 4
EXPERTS_PER_GROUP = 8
N_EXPERTS = N_EXPERT_GROUPS * EXPERTS_PER_GROUP
TOP_K = 2

LANES = 128
SUBLANES = 8
BF16_ROWS = 16
VMEM_LIMIT = 56 * 1024 * 1024


def _params(*sem):
    return pltpu.CompilerParams(dimension_semantics=sem, vmem_limit_bytes=VMEM_LIMIT)


def _sigmoid(x):
    return 1.0 / (1.0 + jnp.exp(-x))


def _rmsnorm_body(x_ref, w_ref, o_ref):
    x = x_ref[...]
    ms = jnp.mean(x * x, axis=-1, keepdims=True)
    o_ref[...] = (x * lax.rsqrt(ms + EPS) * w_ref[...]).astype(o_ref.dtype)


def rmsnorm_rows(x, w, out_dtype, tm):
    t, d = x.shape
    return pl.pallas_call(
        _rmsnorm_body,
        grid=(t // tm,),
        in_specs=[pl.BlockSpec((tm, d), lambda i: (i, 0)), pl.BlockSpec((1, d), lambda i: (0, 0))],
        out_specs=pl.BlockSpec((tm, d), lambda i: (i, 0)),
        out_shape=jax.ShapeDtypeStruct((t, d), out_dtype),
        compiler_params=_params("parallel"),
        name="rmsnorm",
    )(x, w.reshape(1, d).astype(F32))


def _matmul_body(x_ref, w_ref, o_ref):
    o_ref[...] = jnp.dot(x_ref[...], w_ref[...], preferred_element_type=F32).astype(o_ref.dtype)


def matmul(x, w, out_dtype, tm, tn, name):
    m, k = x.shape
    n = w.shape[1]
    return pl.pallas_call(
        _matmul_body,
        grid=(m // tm, n // tn),
        in_specs=[pl.BlockSpec((tm, k), lambda i, j: (i, 0)), pl.BlockSpec((k, tn), lambda i, j: (0, j))],
        out_specs=pl.BlockSpec((tm, tn), lambda i, j: (i, j)),
        out_shape=jax.ShapeDtypeStruct((m, n), out_dtype),
        compiler_params=_params("parallel", "parallel"),
        name=name,
    )(x, w)


def _outproj_body(a_ref, s_ref, wa_ref, ws_ref, r_ref, o_ref):
    acc = jnp.dot(a_ref[...], wa_ref[...], preferred_element_type=F32)
    acc = acc + jnp.dot(s_ref[...], ws_ref[...], preferred_element_type=F32)
    o_ref[...] = r_ref[...] + acc


def out_projection(attn, ssm, w, resid, tm, tn):
    m, ka = attn.shape
    ks = ssm.shape[1]
    assert ka == ks
    n = w.shape[1]
    return pl.pallas_call(
        _outproj_body,
        grid=(m // tm, n // tn),
        in_specs=[
            pl.BlockSpec((tm, ka), lambda i, j: (i, 0)),
            pl.BlockSpec((tm, ks), lambda i, j: (i, 0)),
            pl.BlockSpec((ka, tn), lambda i, j: (0, j)),
            pl.BlockSpec((ks, tn), lambda i, j: (1, j)),
            pl.BlockSpec((tm, tn), lambda i, j: (i, j)),
        ],
        out_specs=pl.BlockSpec((tm, tn), lambda i, j: (i, j)),
        out_shape=jax.ShapeDtypeStruct((m, n), F32),
        compiler_params=_params("parallel", "parallel"),
        name="out_proj",
    )(attn, ssm, w, w, resid)


def _t5_bucket(rel):
    half = NUM_BUCKETS // 2
    max_exact = half // 2
    n = jnp.abs(rel)
    large = max_exact + (
        jnp.log(jnp.maximum(n, 1).astype(F32) / max_exact) / math.log(MAX_DISTANCE / max_exact) * (half - max_exact)
    ).astype(jnp.int32)
    large = jnp.minimum(large, half - 1)
    return jnp.where(rel > 0, half, 0) + jnp.where(n < max_exact, n, large)


def _band_table(rel_bias, tq, tk):
    assert tk + 1 >= MAX_DISTANCE
    i = jnp.arange(tq, dtype=jnp.int32)[:, None]
    c = jnp.arange(tq + 4 * tk, dtype=jnp.int32)[None, :]
    return jnp.transpose(rel_bias[_t5_bucket(c - 2 * tk - i)].astype(F32), (2, 0, 1))


def _attn_body(q_ref, k_ref, v_ref, band_ref, lq1_ref, lk1_ref, lq2_ref, lk2_ref, subw_ref, o_ref,
               m_ref, l_ref, acc_ref, *, tq, tk, nk, lam_init):
    dh = ATTN_HEAD_DIM
    qi = pl.program_id(2)
    scale = dh ** -0.5
    m_ref[...] = jnp.full(m_ref.shape, -jnp.inf, F32)
    l_ref[...] = jnp.zeros(l_ref.shape, F32)
    acc_ref[...] = jnp.zeros(acc_ref.shape, F32)

    def chunk(kc, carry):
        k0 = pl.multiple_of(kc * tk, tk)
        start = jnp.clip(kc * tk - qi * tq + 2 * tk, 0, tq + 3 * tk)
        bias = band_ref[0, :, pl.ds(pl.multiple_of(start, LANES), tk)]
        v = v_ref[pl.ds(k0, tk), :]
        for mi in range(2):
            q = q_ref[:, mi * dh:(mi + 1) * dh]
            k = k_ref[pl.ds(k0, tk), mi * dh:(mi + 1) * dh]
            s = lax.dot_general(q, k, (((1,), (1,)), ((), ())), preferred_element_type=F32) * scale + bias
            m_prev = m_ref[mi]
            m_new = jnp.maximum(m_prev, jnp.max(s, axis=-1, keepdims=True))
            alpha = jnp.exp(m_prev - m_new)
            p = jnp.exp(s - jnp.concatenate([m_new] * (tk // LANES), axis=1))
            l_ref[mi] = alpha * l_ref[mi] + jnp.sum(p, axis=-1, keepdims=True)
            m_ref[mi] = m_new
            pv = jnp.dot(p.astype(BF16), v, preferred_element_type=F32)
            acc_ref[mi] = acc_ref[mi] * jnp.concatenate([alpha] * (2 * dh // LANES), axis=1) + pv
        return carry

    lax.fori_loop(0, nk, chunk, 0)

    lam = (jnp.exp(jnp.sum(lq1_ref[...] * lk1_ref[...], axis=-1, keepdims=True))
           - jnp.exp(jnp.sum(lq2_ref[...] * lk2_ref[...], axis=-1, keepdims=True)) + lam_init)
    o1 = acc_ref[0] / jnp.concatenate([l_ref[0]] * (2 * dh // LANES), axis=1)
    o2 = acc_ref[1] / jnp.concatenate([l_ref[1]] * (2 * dh // LANES), axis=1)
    o = o1 - lam * o2
    ms = jnp.mean(o * o, axis=-1, keepdims=True)
    o = o * lax.rsqrt(ms + EPS) * subw_ref[...]
    o_ref[...] = (o * (1.0 - lam_init)).astype(o_ref.dtype)


def diff_attention(proj, band, lq1, lk1, lq2, lk2, subw, *, batch, seq, heads, lam_init, tq, tk):
    dv = 2 * ATTN_HEAD_DIM
    nq, nk = seq // tq, seq // tk
    vec = pl.BlockSpec((1, ATTN_HEAD_DIM), lambda b, h, i: (0, 0))
    body = functools.partial(_attn_body, tq=tq, tk=tk, nk=nk, lam_init=lam_init)
    return pl.pallas_call(
        body,
        grid=(batch, heads, nq),
        in_specs=[
            pl.BlockSpec((tq, dv), lambda b, h, i: (b * nq + i, h)),
            pl.BlockSpec((seq, dv), lambda b, h, i: (b, heads + h)),
            pl.BlockSpec((seq, dv), lambda b, h, i: (b, 2 * heads + h)),
            pl.BlockSpec((1, tq, tq + 4 * tk), lambda b, h, i: (h, 0, 0)),
            vec, vec, vec, vec,
            pl.BlockSpec((1, dv), lambda b, h, i: (0, 0)),
        ],
        out_specs=pl.BlockSpec((tq, dv), lambda b, h, i: (b * nq + i, h)),
        out_shape=jax.ShapeDtypeStruct((batch * seq, heads * dv), BF16),
        scratch_shapes=[
            pltpu.VMEM((2, tq, LANES), F32),
            pltpu.VMEM((2, tq, LANES), F32),
            pltpu.VMEM((2, tq, dv), F32),
        ],
        compiler_params=_params("parallel", "parallel", "parallel"),
        name="diff_attention",
    )(proj, proj, proj, band, lq1, lk1, lq2, lk2, subw)


def _ssd_body(*refs, reverse, final, heads, col_off):
    if final:
        (z_ref, xbc_ref, hp_ref, hn_ref, dt_ref, cw_ref, cb_ref, dtb_ref, alog_ref, dskip_ref, nw_ref, yb_ref,
         o_ref, u_ref, act_ref, state_ref) = refs
    else:
        (xbc_ref, hp_ref, hn_ref, dt_ref, cw_ref, cb_ref, dtb_ref, alog_ref, o_ref, u_ref, act_ref, state_ref) = refs
    L = CHUNK
    G, N, P = SSM_GROUPS, SSM_STATE, SSM_HEAD_DIM
    R = heads // G
    GW = R * P
    W = heads * P
    CC = W + 2 * G * N
    HALO = SUBLANES
    c = pl.program_id(1)
    nc = pl.num_programs(1)
    pos = (nc - 1 - c) if reverse else c

    @pl.when(c == 0)
    def _():
        state_ref[...] = jnp.zeros(state_ref.shape, F32)

    prev = hp_ref[...].astype(F32)[BF16_ROWS - HALO:, :]
    nxt = hn_ref[...].astype(F32)[:HALO, :]
    u_ref[0:HALO, :] = jnp.where(pos > 0, prev, 0.0)
    u_ref[HALO:HALO + L, :] = xbc_ref[...].astype(F32)
    u_ref[HALO + L:, :] = jnp.where(pos < nc - 1, nxt, 0.0)
    SLAB = 2 * LANES

    def conv_slab(j, carry):
        lo = pl.multiple_of(j * SLAB, SLAB)
        acc = jnp.broadcast_to(cb_ref[:, pl.ds(lo, SLAB)], (L, SLAB))
        for t in range(CONV_WIDTH):
            r0 = HALO - CONV_WIDTH // 2 + t
            acc = acc + cw_ref[t:t + 1, pl.ds(lo, SLAB)] * u_ref[r0:r0 + L, pl.ds(lo, SLAB)]
        act_ref[:, pl.ds(lo, SLAB)] = acc * _sigmoid(acc)
        return carry

    lax.fori_loop(0, CC // SLAB, conv_slab, 0)

    xdt = dt_ref[...] + dtb_ref[...]
    dtv = jnp.maximum(xdt, 0.0) + jnp.log1p(jnp.exp(-jnp.abs(xdt)))
    a = -jnp.exp(alog_ref[...]) * dtv
    row = lax.broadcasted_iota(jnp.int32, (L, L), 0)
    col = lax.broadcasted_iota(jnp.int32, (L, L), 1)
    keep = (row <= col) if reverse else (row >= col)
    cum = jnp.dot(keep.astype(F32), a, precision=lax.Precision.HIGHEST, preferred_element_type=F32)
    cum_t = cum.T
    total = cum[0:1, :] if reverse else cum[L - 1:L, :]
    seg = lax.broadcasted_iota(jnp.int32, (L, GW), 1) // P
    seg1 = lax.broadcasted_iota(jnp.int32, (1, GW), 1) // P

    for g in range(G):
        xs = act_ref[:, g * GW:(g + 1) * GW]
        bmat = act_ref[:, W + g * N:W + (g + 1) * N]
        cmat = act_ref[:, W + (G + g) * N:W + (G + g + 1) * N]
        b_bf = bmat.astype(BF16)
        c_bf = cmat.astype(BF16)
        cb = lax.dot_general(c_bf, b_bf, (((1,), (1,)), ((), ())), preferred_element_type=F32)
        bt_bf = bmat.T.astype(BF16)
        dt_e = jnp.zeros((L, GW), F32)
        cum_e = jnp.zeros((L, GW), F32)
        tot_e = jnp.zeros((1, GW), F32)
        for r in range(R):
            hc = col_off + g * R + r
            dt_e = jnp.where(seg == r, dtv[:, hc:hc + 1], dt_e)
            cum_e = jnp.where(seg == r, cum[:, hc:hc + 1], cum_e)
            tot_e = jnp.where(seg1 == r, total[:, hc:hc + 1], tot_e)
        x_dt = xs * dt_e
        y = jnp.zeros((L, GW), F32)
        for r in range(R):
            hc = col_off + g * R + r
            decay = jnp.exp(jnp.where(keep, cum[:, hc:hc + 1] - cum_t[hc:hc + 1, :], -jnp.inf))
            m_h = (cb * decay).astype(BF16)
            x_h = jnp.where(seg == r, x_dt, 0.0).astype(BF16)
            y = y + jnp.dot(m_h, x_h, preferred_element_type=F32)
        s_prev = state_ref[g]
        y = y + jnp.dot(c_bf, s_prev.astype(BF16), preferred_element_type=F32) * jnp.exp(cum_e)
        x_end = (x_dt * jnp.exp(tot_e - cum_e)).astype(BF16)
        state_ref[g] = s_prev * jnp.exp(tot_e) + jnp.dot(bt_bf, x_end, preferred_element_type=F32)
        if final:
            y = y + yb_ref[:, g * GW:(g + 1) * GW] + dskip_ref[:, g * GW:(g + 1) * GW] * xs
            zg = z_ref[:, g * GW:(g + 1) * GW].astype(F32)
            y = y * (zg * _sigmoid(zg))
            ms = jnp.mean(y * y, axis=-1, keepdims=True)
            y = y * lax.rsqrt(ms + EPS) * nw_ref[:, g * GW:(g + 1) * GW]
        o_ref[:, g * GW:(g + 1) * GW] = y.astype(o_ref.dtype)


def ssd_pass(proj, dt, conv_w, conv_b, dt_bias, a_log, extra, *, batch, seq, heads, z_blk, xbc_blk, reverse, final,
             col_off):
    L = CHUNK
    W = heads * SSM_HEAD_DIM
    CC = W + 2 * SSM_GROUPS * SSM_STATE
    GW = W // SSM_GROUPS
    nc = seq // L
    nhalo = batch * seq // BF16_ROWS
    per = L // BF16_ROWS

    def rb(b, c):
        return b * nc + ((nc - 1 - c) if reverse else c)

    full = lambda shape: pl.BlockSpec(shape, lambda b, c: (0,) * len(shape))
    in_specs = [
        pl.BlockSpec((L, CC), lambda b, c: (rb(b, c), xbc_blk)),
        pl.BlockSpec((BF16_ROWS, CC), lambda b, c: (jnp.maximum(rb(b, c) * per - 1, 0), xbc_blk)),
        pl.BlockSpec((BF16_ROWS, CC), lambda b, c: (jnp.minimum((rb(b, c) + 1) * per, nhalo - 1), xbc_blk)),
        pl.BlockSpec((L, LANES), lambda b, c: (rb(b, c), 0)),
        full((CONV_WIDTH, CC)), full((1, CC)), full((1, LANES)), full((1, LANES)),
    ]
    args = [proj, proj, proj, dt, conv_w, conv_b, dt_bias, a_log]
    if final:
        in_specs = [pl.BlockSpec((L, W), lambda b, c: (rb(b, c), z_blk))] + in_specs + [
            full((1, W)), full((1, W)), pl.BlockSpec((L, W), lambda b, c: (rb(b, c), 0))]
        args = [proj] + args + [extra["d_skip"], extra["norm_w"], extra["y_other"]]
    body = functools.partial(_ssd_body, reverse=reverse, final=final, heads=heads, col_off=col_off)
    return pl.pallas_call(
        body,
        grid=(batch, nc),
        in_specs=in_specs,
        out_specs=pl.BlockSpec((L, W), lambda b, c: (rb(b, c), 0)),
        out_shape=jax.ShapeDtypeStruct((batch * seq, W), BF16 if final else F32),
        scratch_shapes=[
            pltpu.VMEM((L + 2 * SUBLANES, CC), F32),
            pltpu.VMEM((L, CC), F32),
            pltpu.VMEM((SSM_GROUPS, SSM_STATE, GW), F32),
        ],
        compiler_params=_params("parallel", "arbitrary"),
        name="ssd_final" if final else "ssd_scan",
    )(*args)


def _router_body(x_ref, nw_ref, wr_ref, br_ref, h_ref, idx_ref, gate_ref):
    x = x_ref[...]
    ms = jnp.mean(x * x, axis=-1, keepdims=True)
    h = x * lax.rsqrt(ms + EPS) * nw_ref[...]
    h_ref[...] = h.astype(h_ref.dtype)
    logits = jnp.dot(h, wr_ref[...], precision=lax.Precision.HIGHEST, preferred_element_type=F32) + br_ref[...]
    lane = lax.broadcasted_iota(jnp.int32, logits.shape, 1).astype(F32)
    ninf = -jnp.inf
    is_g = lane < N_EXPERT_GROUPS
    gl = jnp.where(is_g, logits, ninf)
    gmax = jnp.max(gl, axis=-1, keepdims=True)
    gsum = jnp.sum(jnp.where(is_g, jnp.exp(gl - gmax), 0.0), axis=-1, keepdims=True)
    p_group = 1.0 / gsum
    g_sel = jnp.min(jnp.where(gl == gmax, lane, float(LANES)), axis=-1, keepdims=True)
    lo = N_EXPERT_GROUPS + EXPERTS_PER_GROUP * g_sel
    el = jnp.where((lane >= lo) & (lane < lo + EXPERTS_PER_GROUP), logits, ninf)
    m1 = jnp.max(el, axis=-1, keepdims=True)
    i1 = jnp.min(jnp.where(el == m1, lane, float(LANES)), axis=-1, keepdims=True)
    el2 = jnp.where(lane == i1, ninf, el)
    m2 = jnp.max(el2, axis=-1, keepdims=True)
    i2 = jnp.min(jnp.where(el2 == m2, lane, float(LANES)), axis=-1, keepdims=True)
    r = jnp.exp(m2 - m1)
    g1 = p_group / (1.0 + r)
    g2 = p_group * r / (1.0 + r)
    e1 = i1 - N_EXPERT_GROUPS
    e2 = i2 - N_EXPERT_GROUPS
    idx_ref[...] = jnp.where(lane == 0, e1, jnp.where(lane == 1, e2, 0.0)).astype(jnp.int32)
    gate_ref[...] = jnp.where(lane == 0, g1, jnp.where(lane == 1, g2, 0.0))


def norm_and_route(x, norm_w, w_router, b_router, tm):
    t, d = x.shape
    row = lambda w: pl.BlockSpec((tm, w), lambda i: (i, 0))
    return pl.pallas_call(
        _router_body,
        grid=(t // tm,),
        in_specs=[row(d), pl.BlockSpec((1, d), lambda i: (0, 0)), pl.BlockSpec((d, LANES), lambda i: (0, 0)),
                  pl.BlockSpec((1, LANES), lambda i: (0, 0))],
        out_specs=[row(d), row(LANES), row(LANES)],
        out_shape=[jax.ShapeDtypeStruct((t, d), BF16), jax.ShapeDtypeStruct((t, LANES), jnp.int32),
                   jax.ShapeDtypeStruct((t, LANES), F32)],
        compiler_params=_params("parallel"),
        name="router",
    )(x, norm_w.reshape(1, d).astype(F32), w_router, b_router)


def _expert_changed(be_ref, i):
    return (i == 0) | (be_ref[i] != be_ref[jnp.maximum(i - 1, 0)])


def _gate_up_body(be_ref, nu_ref, x_ref, wg_ref, wu_ref, o_ref, wgb_ref, wub_ref):
    i = pl.program_id(1)

    @pl.when(_expert_changed(be_ref, i))
    def _():
        wgb_ref[...] = wg_ref[0].astype(BF16)
        wub_ref[...] = wu_ref[0].astype(BF16)

    @pl.when(i < nu_ref[0])
    def _():
        x = x_ref[...]
        a = jnp.dot(x, wgb_ref[...], preferred_element_type=F32)
        b = jnp.dot(x, wub_ref[...], preferred_element_type=F32)
        o_ref[...] = (a * _sigmoid(a) * b).astype(o_ref.dtype)


def expert_gate_up(blk_e, n_used, xs, w_gate, w_up, tm, tf):
    rows, d = xs.shape
    f = w_gate.shape[2]
    nblk = rows // tm
    last = lambda i, nu: jnp.minimum(i, nu[0] - 1)
    wspec = pl.BlockSpec((1, d, tf), lambda j, i, be, nu: (be[i], 0, j))
    return pl.pallas_call(
        _gate_up_body,
        grid_spec=pltpu.PrefetchScalarGridSpec(
            num_scalar_prefetch=2,
            grid=(f // tf, nblk),
            in_specs=[pl.BlockSpec((tm, d), lambda j, i, be, nu: (last(i, nu), 0)), wspec, wspec],
            out_specs=pl.BlockSpec((tm, tf), lambda j, i, be, nu: (last(i, nu), j)),
            scratch_shapes=[pltpu.VMEM((d, tf), BF16), pltpu.VMEM((d, tf), BF16)],
        ),
        out_shape=jax.ShapeDtypeStruct((rows, f), BF16),
        compiler_params=_params("arbitrary", "arbitrary"),
        name="expert_gate_up",
    )(blk_e, n_used, xs, w_gate, w_up)


def _down_body(be_ref, nu_ref, a_ref, wd_ref, g_ref, o_ref, wdb_ref):
    i = pl.program_id(1)

    @pl.when(_expert_changed(be_ref, i))
    def _():
        wdb_ref[...] = wd_ref[0].astype(BF16)

    @pl.when(i < nu_ref[0])
    def _():
        y = jnp.dot(a_ref[...], wdb_ref[...], preferred_element_type=F32)
        o_ref[...] = (y * g_ref[...]).astype(o_ref.dtype)


def expert_down(blk_e, n_used, act, w_down, row_gate, tm, tn):
    rows, f = act.shape
    d = w_down.shape[2]
    nblk = rows // tm
    last = lambda i, nu: jnp.minimum(i, nu[0] - 1)
    return pl.pallas_call(
        _down_body,
        grid_spec=pltpu.PrefetchScalarGridSpec(
            num_scalar_prefetch=2,
            grid=(d // tn, nblk),
            in_specs=[
                pl.BlockSpec((tm, f), lambda j, i, be, nu: (last(i, nu), 0)),
                pl.BlockSpec((1, f, tn), lambda j, i, be, nu: (be[i], 0, j)),
                pl.BlockSpec((tm, 1), lambda j, i, be, nu: (last(i, nu), 0)),
            ],
            out_specs=pl.BlockSpec((tm, tn), lambda j, i, be, nu: (last(i, nu), j)),
            scratch_shapes=[pltpu.VMEM((f, tn), BF16)],
        ),
        out_shape=jax.ShapeDtypeStruct((rows, d), BF16),
        compiler_params=_params("arbitrary", "arbitrary"),
        name="expert_down",
    )(blk_e, n_used, act, w_down, row_gate)


def _moe_plan(idx, gate, tm):
    t = idx.shape[0]
    n = t * TOP_K
    flat_e = idx.reshape(n)
    onehot = (flat_e[:, None] == jnp.arange(N_EXPERTS, dtype=jnp.int32)[None, :]).astype(jnp.int32)
    csum = jnp.cumsum(onehot, axis=0)
    rank = jnp.sum(csum * onehot, axis=1) - 1
    counts = csum[-1]
    padded = (counts + tm - 1) // tm * tm
    pend = jnp.cumsum(padded)
    dest = (pend - padded)[flat_e] + rank
    nblk = n // tm + N_EXPERTS
    rows = nblk * tm
    tok = jnp.arange(n, dtype=jnp.int32) // TOP_K
    row_tok = jnp.zeros((rows,), jnp.int32).at[dest].set(tok)
    row_gate = jnp.zeros((rows,), F32).at[dest].set(gate.reshape(n))
    n_used = (pend[-1] // tm).astype(jnp.int32)
    blk = jnp.arange(nblk, dtype=jnp.int32)
    blk_e = jnp.minimum(jnp.searchsorted(pend, blk * tm, side="right"), N_EXPERTS - 1).astype(jnp.int32)
    blk_e = jnp.where(blk < n_used, blk_e, blk_e[jnp.maximum(n_used - 1, 0)])
    return row_tok, row_gate, blk_e, n_used.reshape(1), dest.reshape(t, TOP_K)


def _combine_body(x_ref, y0_ref, y1_ref, w_ref, o_ref):
    x = x_ref[...] + (y0_ref[...].astype(F32) + y1_ref[...].astype(F32))
    ms = jnp.mean(x * x, axis=-1, keepdims=True)
    o_ref[...] = x * lax.rsqrt(ms + EPS) * w_ref[...]


def combine_and_norm(x, y0, y1, w, tm):
    t, d = x.shape
    row = pl.BlockSpec((tm, d), lambda i: (i, 0))
    return pl.pallas_call(
        _combine_body,
        grid=(t // tm,),
        in_specs=[row, row, row, pl.BlockSpec((1, d), lambda i: (0, 0))],
        out_specs=row,
        out_shape=jax.ShapeDtypeStruct((t, d), F32),
        compiler_params=_params("parallel"),
        name="combine_norm",
    )(x, y0, y1, w.reshape(1, d).astype(F32))


def _tiles(batch, seq):
    t = batch * seq
    return dict(
        norm_tm=min(256, t),
        mm_tm=min(1024, t),
        mm_tn=1024,
        out_tn=512,
        attn_tq=min(512, seq),
        attn_tk=min(512, seq),
        moe_tm=min(256, t),
        moe_tf=512,
        moe_tn=1024,
    )


def _lambda_init_at(layer):
    return 0.8 - 0.6 * math.exp(-0.3 * layer)


def kernel(x, rel_bias, norm1_w, w_in, lambda_q1, lambda_k1, lambda_q2, lambda_k2, subln_w, conv_w, conv_b,
           dt_bias_f, dt_bias_b, a_log_f, a_log_b, d_skip, ssm_norm_w, w_out, norm2_w, w_group_router,
           b_group_router, w_expert_router, b_expert_router, w_gate, w_up, w_down, final_norm_w):
    batch, seq, d = x.shape
    t = batch * seq
    depth = norm1_w.shape[0]
    attn_w = d // 2
    ssm_w = d - attn_w
    dv = 2 * ATTN_HEAD_DIM
    a_heads = attn_w // dv
    s_heads = ssm_w // SSM_HEAD_DIM
    cc = ssm_w + 2 * SSM_GROUPS * SSM_STATE
    main_w = 3 * attn_w + ssm_w + cc
    assert ssm_w == attn_w and cc == 2 * ssm_w and 2 * s_heads <= LANES
    tl = _tiles(batch, seq)
    band = _band_table(rel_bias, tl["attn_tq"], tl["attn_tk"])

    def pad_lanes(v):
        return jnp.pad(v.astype(F32), (0, LANES - v.shape[0])).reshape(1, LANES)

    xf = x.reshape(t, d)
    for layer in range(depth):
        lam_init = _lambda_init_at(layer)
        wl = w_in[layer]
        w_main = wl[:, :main_w].astype(BF16)
        w_dt = jnp.pad(wl[:, main_w:], ((0, 0), (0, LANES - 2 * s_heads))).astype(BF16)
        h = rmsnorm_rows(xf, norm1_w[layer], BF16, tl["norm_tm"])
        proj = matmul(h, w_main, BF16, tl["mm_tm"], tl["mm_tn"], "in_proj")
        dt = matmul(h, w_dt, F32, tl["mm_tm"], LANES, "dt_proj")

        vec = lambda v: v[layer].reshape(1, -1).astype(F32)
        attn = diff_attention(proj, band, vec(lambda_q1), vec(lambda_k1), vec(lambda_q2), vec(lambda_k2),
                              vec(subln_w), batch=batch, seq=seq, heads=a_heads, lam_init=lam_init,
                              tq=tl["attn_tq"], tk=tl["attn_tk"])

        dt_bias = pad_lanes(jnp.concatenate([dt_bias_f[layer], dt_bias_b[layer]]))
        a_log = pad_lanes(jnp.concatenate([a_log_f[layer], a_log_b[layer]]))
        cw = conv_w[layer].astype(F32)
        cb = conv_b[layer].reshape(1, cc).astype(F32)
        common = dict(batch=batch, seq=seq, heads=s_heads, z_blk=3 * attn_w // ssm_w, xbc_blk=(3 * attn_w + ssm_w) // cc)
        y_bwd = ssd_pass(proj, dt, cw, cb, dt_bias, a_log, None, reverse=True, final=False, col_off=s_heads, **common)
        extra = dict(d_skip=jnp.repeat(d_skip[layer].astype(F32), SSM_HEAD_DIM).reshape(1, ssm_w),
                     norm_w=vec(ssm_norm_w), y_other=y_bwd)
        ssm = ssd_pass(proj, dt, cw, cb, dt_bias, a_log, extra, reverse=False, final=True, col_off=0, **common)

        x1 = out_projection(attn, ssm, w_out[layer].astype(BF16), xf, tl["mm_tm"], tl["out_tn"])

        w_router = jnp.pad(jnp.concatenate([w_group_router[layer], w_expert_router[layer]], axis=1).astype(F32),
                           ((0, 0), (0, LANES - N_EXPERT_GROUPS - N_EXPERTS)))
        b_router = pad_lanes(jnp.concatenate([b_group_router[layer], b_expert_router[layer]]))
        h2, idx, gate = norm_and_route(x1, norm2_w[layer], w_router, b_router, tl["norm_tm"])
        tm = tl["moe_tm"]
        row_tok, row_gate, blk_e, n_used, dest = _moe_plan(idx[:, :TOP_K], gate[:, :TOP_K], tm)
        xs = jnp.take(h2, row_tok, axis=0)
        act = expert_gate_up(blk_e, n_used, xs, w_gate[layer], w_up[layer], tm, tl["moe_tf"])
        y = expert_down(blk_e, n_used, act, w_down[layer], row_gate.reshape(-1, 1), tm, tl["moe_tn"])
        y0 = jnp.take(y, dest[:, 0], axis=0)
        y1 = jnp.take(y, dest[:, 1], axis=0)
        if layer + 1 < depth:
            raise NotImplementedError("multi-layer stacking needs an un-normalised combine")
        xf = combine_and_norm(x1, y0, y1, final_norm_w, tl["norm_tm"])
    return xf.reshape(batch, seq, d)
```

```python
import functools
import math

import jax
import jax.numpy as jnp
from jax import lax
from jax.experimental import pallas as pl
from jax.experimental.pallas import tpu as pltpu

F32 = jnp.float32
BF16 = jnp.bfloat16
U32 = jnp.uint32
I32 = jnp.int32
EPS = 1e-6

ATTN_HEAD_DIM = 128
NUM_BUCKETS = 32
MAX_DISTANCE = 128
SSM_HEAD_DIM = 64
SSM_GROUPS = 8
SSM_STATE = 128
CONV_WIDTH = 5
CHUNK = 128
N_EXPERT_GROUPS = 4
EXPERTS_PER_GROUP = 8
N_EXPERTS = N_EXPERT_GROUPS * EXPERTS_PER_GROUP
TOP_K = 2

LANES = 128
SUBLANES = 8
BF16_ROWS = 16
VMEM_LIMIT = 56 * 1024 * 1024
HI16 = 0xFFFF0000


def _params(*sem):
    return pltpu.CompilerParams(dimension_semantics=sem, vmem_limit_bytes=VMEM_LIMIT)


def _sigmoid(x):
    return 1.0 / (1.0 + jnp.exp(-x))


def _pack_pairs(lo, hi):
    lo_b = lax.bitcast_convert_type(lo.astype(BF16).astype(F32), U32)
    hi_b = lax.bitcast_convert_type(hi.astype(BF16).astype(F32), U32)
    return (lo_b >> 16) | (hi_b & U32(HI16))


def _unpack_pairs(w):
    return lax.bitcast_convert_type(w << 16, F32), lax.bitcast_convert_type(w & U32(HI16), F32)


def _rmsnorm_body(x_ref, w_ref, o_ref):
    x = x_ref[...]
    ms = jnp.mean(x * x, axis=-1, keepdims=True)
    o_ref[...] = (x * lax.rsqrt(ms + EPS) * w_ref[...]).astype(o_ref.dtype)


def rmsnorm_rows(x, w, out_dtype, tm):
    t, d = x.shape
    return pl.pallas_call(
        _rmsnorm_body,
        grid=(t // tm,),
        in_specs=[pl.BlockSpec((tm, d), lambda i: (i, 0)), pl.BlockSpec((1, d), lambda i: (0, 0))],
        out_specs=pl.BlockSpec((tm, d), lambda i: (i, 0)),
        out_shape=jax.ShapeDtypeStruct((t, d), out_dtype),
        compiler_params=_params("parallel"),
        name="rmsnorm",
    )(x, w.reshape(1, d).astype(F32))


def _matmul_body(x_ref, w_ref, o_ref):
    o_ref[...] = jnp.dot(x_ref[...], w_ref[...], preferred_element_type=F32).astype(o_ref.dtype)


def matmul(x, w, out_dtype, tm, tn, name):
    m, k = x.shape
    n = w.shape[1]
    return pl.pallas_call(
        _matmul_body,
        grid=(m // tm, n // tn),
        in_specs=[pl.BlockSpec((tm, k), lambda i, j: (i, 0)), pl.BlockSpec((k, tn), lambda i, j: (0, j))],
        out_specs=pl.BlockSpec((tm, tn), lambda i, j: (i, j)),
        out_shape=jax.ShapeDtypeStruct((m, n), out_dtype),
        compiler_params=_params("parallel", "parallel"),
        name=name,
    )(x, w)


def _outproj_body(a_ref, s_ref, wa_ref, ws_ref, r_ref, o_ref):
    acc = jnp.dot(a_ref[...], wa_ref[...], preferred_element_type=F32)
    acc = acc + jnp.dot(s_ref[...], ws_ref[...], preferred_element_type=F32)
    o_ref[...] = r_ref[...] + acc


def out_projection(attn, ssm, w, resid, tm, tn):
    m, ka = attn.shape
    ks = ssm.shape[1]
    assert ka == ks
    n = w.shape[1]
    return pl.pallas_call(
        _outproj_body,
        grid=(m // tm, n // tn),
        in_specs=[
            pl.BlockSpec((tm, ka), lambda i, j: (i, 0)),
            pl.BlockSpec((tm, ks), lambda i, j: (i, 0)),
            pl.BlockSpec((ka, tn), lambda i, j: (0, j)),
            pl.BlockSpec((ks, tn), lambda i, j: (1, j)),
            pl.BlockSpec((tm, tn), lambda i, j: (i, j)),
        ],
        out_specs=pl.BlockSpec((tm, tn), lambda i, j: (i, j)),
        out_shape=jax.ShapeDtypeStruct((m, n), F32),
        compiler_params=_params("parallel", "parallel"),
        name="out_proj",
    )(attn, ssm, w, w, resid)


def _t5_bucket(rel):
    half = NUM_BUCKETS // 2
    max_exact = half // 2
    n = jnp.abs(rel)
    large = max_exact + (
        jnp.log(jnp.maximum(n, 1).astype(F32) / max_exact) / math.log(MAX_DISTANCE / max_exact) * (half - max_exact)
    ).astype(jnp.int32)
    large = jnp.minimum(large, half - 1)
    return jnp.where(rel > 0, half, 0) + jnp.where(n < max_exact, n, large)


def _band_table(rel_bias, tq, tk):
    assert tk + 1 >= MAX_DISTANCE
    heads = rel_bias.shape[1]
    w = tq + 4 * tk
    n = w + tq - 1
    rel = jnp.arange(n, dtype=jnp.int32) - (2 * tk + tq - 1)
    v = rel_bias[_t5_bucket(rel)].astype(F32).T
    m = jnp.pad(jnp.broadcast_to(v[:, None, :], (heads, tq, n)), ((0, 0), (0, 0), (0, 1)))
    flat = jnp.pad(m.reshape(heads, tq * (n + 1)), ((0, 0), (0, tq)))
    return flat.reshape(heads, tq, n + 2)[:, ::-1, :w]


def _attn_body(q_ref, k_ref, v_ref, band_ref, lq1_ref, lk1_ref, lq2_ref, lk2_ref, subw_ref, o_ref,
               m_ref, l_ref, acc_ref, *, tq, tk, nk, lam_init):
    dh = ATTN_HEAD_DIM
    qi = pl.program_id(2)
    scale = dh ** -0.5
    m_ref[...] = jnp.full(m_ref.shape, -jnp.inf, F32)
    l_ref[...] = jnp.zeros(l_ref.shape, F32)
    acc_ref[...] = jnp.zeros(acc_ref.shape, F32)

    def chunk(kc, carry):
        k0 = pl.multiple_of(kc * tk, tk)
        start = jnp.clip(kc * tk - qi * tq + 2 * tk, 0, tq + 3 * tk)
        bias = band_ref[0, :, pl.ds(pl.multiple_of(start, LANES), tk)]
        v = v_ref[pl.ds(k0, tk), :]
        for mi in range(2):
            q = q_ref[:, mi * dh:(mi + 1) * dh]
            k = k_ref[pl.ds(k0, tk), mi * dh:(mi + 1) * dh]
            s = lax.dot_general(q, k, (((1,), (1,)), ((), ())), preferred_element_type=F32) * scale + bias
            m_prev = m_ref[mi]
            m_new = jnp.maximum(m_prev, jnp.max(s, axis=-1, keepdims=True))
            alpha = jnp.exp(m_prev - m_new)
            p = jnp.exp(s - jnp.concatenate([m_new] * (tk // LANES), axis=1))
            l_ref[mi] = alpha * l_ref[mi] + jnp.sum(p, axis=-1, keepdims=True)
            m_ref[mi] = m_new
            pv = jnp.dot(p.astype(BF16), v, preferred_element_type=F32)
            acc_ref[mi] = acc_ref[mi] * jnp.concatenate([alpha] * (2 * dh // LANES), axis=1) + pv
        return carry

    lax.fori_loop(0, nk, chunk, 0)

    lam = (jnp.exp(jnp.sum(lq1_ref[...] * lk1_ref[...], axis=-1, keepdims=True))
           - jnp.exp(jnp.sum(lq2_ref[...] * lk2_ref[...], axis=-1, keepdims=True)) + lam_init)
    o1 = acc_ref[0] / jnp.concatenate([l_ref[0]] * (2 * dh // LANES), axis=1)
    o2 = acc_ref[1] / jnp.concatenate([l_ref[1]] * (2 * dh // LANES), axis=1)
    o = o1 - lam * o2
    ms = jnp.mean(o * o, axis=-1, keepdims=True)
    o = o * lax.rsqrt(ms + EPS) * subw_ref[...]
    o_ref[...] = (o * (1.0 - lam_init)).astype(o_ref.dtype)


def diff_attention(proj, band, lq1, lk1, lq2, lk2, subw, *, batch, seq, heads, lam_init, tq, tk):
    dv = 2 * ATTN_HEAD_DIM
    nq, nk = seq // tq, seq // tk
    vec = pl.BlockSpec((1, ATTN_HEAD_DIM), lambda b, h, i: (0, 0))
    body = functools.partial(_attn_body, tq=tq, tk=tk, nk=nk, lam_init=lam_init)
    return pl.pallas_call(
        body,
        grid=(batch, heads, nq),
        in_specs=[
            pl.BlockSpec((tq, dv), lambda b, h, i: (b * nq + i, h)),
            pl.BlockSpec((seq, dv), lambda b, h, i: (b, heads + h)),
            pl.BlockSpec((seq, dv), lambda b, h, i: (b, 2 * heads + h)),
            pl.BlockSpec((1, tq, tq + 4 * tk), lambda b, h, i: (h, 0, 0)),
            vec, vec, vec, vec,
            pl.BlockSpec((1, dv), lambda b, h, i: (0, 0)),
        ],
        out_specs=pl.BlockSpec((tq, dv), lambda b, h, i: (b * nq + i, h)),
        out_shape=jax.ShapeDtypeStruct((batch * seq, heads * dv), BF16),
        scratch_shapes=[
            pltpu.VMEM((2, tq, LANES), F32),
            pltpu.VMEM((2, tq, LANES), F32),
            pltpu.VMEM((2, tq, dv), F32),
        ],
        compiler_params=_params("parallel", "parallel", "parallel"),
        name="diff_attention",
    )(proj, proj, proj, band, lq1, lk1, lq2, lk2, subw)


def _ssd_body(*refs, reverse, final, heads, col_off):
    if final:
        (z_ref, xbc_ref, hp_ref, hn_ref, dt_ref, cw_ref, cb_ref, dtb_ref, alog_ref, dskip_ref, nw_ref, yb_ref,
         o_ref, u_ref, act_ref, state_ref) = refs
    else:
        (xbc_ref, hp_ref, hn_ref, dt_ref, cw_ref, cb_ref, dtb_ref, alog_ref, o_ref, u_ref, act_ref, state_ref) = refs
    L = CHUNK
    G, N, P = SSM_GROUPS, SSM_STATE, SSM_HEAD_DIM
    R = heads // G
    GW = R * P
    W = heads * P
    CC = W + 2 * G * N
    HALO = SUBLANES
    c = pl.program_id(1)
    nc = pl.num_programs(1)
    pos = (nc - 1 - c) if reverse else c

    @pl.when(c == 0)
    def _():
        state_ref[...] = jnp.zeros(state_ref.shape, F32)

    prev = hp_ref[...].astype(F32)[BF16_ROWS - HALO:, :]
    nxt = hn_ref[...].astype(F32)[:HALO, :]
    u_ref[0:HALO, :] = jnp.where(pos > 0, prev, 0.0)
    u_ref[HALO:HALO + L, :] = xbc_ref[...].astype(F32)
    u_ref[HALO + L:, :] = jnp.where(pos < nc - 1, nxt, 0.0)
    SLAB = 2 * LANES

    def conv_slab(j, carry):
        lo = pl.multiple_of(j * SLAB, SLAB)
        acc = jnp.broadcast_to(cb_ref[:, pl.ds(lo, SLAB)], (L, SLAB))
        for t in range(CONV_WIDTH):
            r0 = HALO - CONV_WIDTH // 2 + t
            acc = acc + cw_ref[t:t + 1, pl.ds(lo, SLAB)] * u_ref[r0:r0 + L, pl.ds(lo, SLAB)]
        act_ref[:, pl.ds(lo, SLAB)] = acc * _sigmoid(acc)
        return carry

    lax.fori_loop(0, CC // SLAB, conv_slab, 0)

    xdt = dt_ref[...] + dtb_ref[...]
    dtv = jnp.maximum(xdt, 0.0) + jnp.log1p(jnp.exp(-jnp.abs(xdt)))
    a = -jnp.exp(alog_ref[...]) * dtv
    row = lax.broadcasted_iota(jnp.int32, (L, L), 0)
    col = lax.broadcasted_iota(jnp.int32, (L, L), 1)
    keep = (row <= col) if reverse else (row >= col)
    cum = jnp.dot(keep.astype(F32), a, precision=lax.Precision.HIGHEST, preferred_element_type=F32)
    cum_t = cum.T
    total = cum[0:1, :] if reverse else cum[L - 1:L, :]
    seg = lax.broadcasted_iota(jnp.int32, (L, GW), 1) // P
    seg1 = lax.broadcasted_iota(jnp.int32, (1, GW), 1) // P

    for g in range(G):
        xs = act_ref[:, g * GW:(g + 1) * GW]
        bmat = act_ref[:, W + g * N:W + (g + 1) * N]
        cmat = act_ref[:, W + (G + g) * N:W + (G + g + 1) * N]
        b_bf = bmat.astype(BF16)
        c_bf = cmat.astype(BF16)
        cb = lax.dot_general(c_bf, b_bf, (((1,), (1,)), ((), ())), preferred_element_type=F32)
        bt_bf = bmat.T.astype(BF16)
        dt_e = jnp.zeros((L, GW), F32)
        cum_e = jnp.zeros((L, GW), F32)
        tot_e = jnp.zeros((1, GW), F32)
        for r in range(R):
            hc = col_off + g * R + r
            dt_e = jnp.where(seg == r, dtv[:, hc:hc + 1], dt_e)
            cum_e = jnp.where(seg == r, cum[:, hc:hc + 1], cum_e)
            tot_e = jnp.where(seg1 == r, total[:, hc:hc + 1], tot_e)
        x_dt = xs * dt_e
        y = jnp.zeros((L, GW), F32)
        for r in range(R):
            hc = col_off + g * R + r
            decay = jnp.exp(jnp.where(keep, cum[:, hc:hc + 1] - cum_t[hc:hc + 1, :], -jnp.inf))
            m_h = (cb * decay).astype(BF16)
            x_h = jnp.where(seg == r, x_dt, 0.0).astype(BF16)
            y = y + jnp.dot(m_h, x_h, preferred_element_type=F32)
        s_prev = state_ref[g]
        y = y + jnp.dot(c_bf, s_prev.astype(BF16), preferred_element_type=F32) * jnp.exp(cum_e)
        x_end = (x_dt * jnp.exp(tot_e - cum_e)).astype(BF16)
        state_ref[g] = s_prev * jnp.exp(tot_e) + jnp.dot(bt_bf, x_end, preferred_element_type=F32)
        if final:
            y = y + yb_ref[:, g * GW:(g + 1) * GW] + dskip_ref[:, g * GW:(g + 1) * GW] * xs
            zg = z_ref[:, g * GW:(g + 1) * GW].astype(F32)
            y = y * (zg * _sigmoid(zg))
            ms = jnp.mean(y * y, axis=-1, keepdims=True)
            y = y * lax.rsqrt(ms + EPS) * nw_ref[:, g * GW:(g + 1) * GW]
        o_ref[:, g * GW:(g + 1) * GW] = y.astype(o_ref.dtype)


def ssd_pass(proj, dt, conv_w, conv_b, dt_bias, a_log, extra, *, batch, seq, heads, z_blk, xbc_blk, reverse, final,
             col_off):
    L = CHUNK
    W = heads * SSM_HEAD_DIM
    CC = W + 2 * SSM_GROUPS * SSM_STATE
    GW = W // SSM_GROUPS
    nc = seq // L
    nhalo = batch * seq // BF16_ROWS
    per = L // BF16_ROWS

    def rb(b, c):
        return b * nc + ((nc - 1 - c) if reverse else c)

    full = lambda shape: pl.BlockSpec(shape, lambda b, c: (0,) * len(shape))
    in_specs = [
        pl.BlockSpec((L, CC), lambda b, c: (rb(b, c), xbc_blk)),
        pl.BlockSpec((BF16_ROWS, CC), lambda b, c: (jnp.maximum(rb(b, c) * per - 1, 0), xbc_blk)),
        pl.BlockSpec((BF16_ROWS, CC), lambda b, c: (jnp.minimum((rb(b, c) + 1) * per, nhalo - 1), xbc_blk)),
        pl.BlockSpec((L, LANES), lambda b, c: (rb(b, c), 0)),
        full((CONV_WIDTH, CC)), full((1, CC)), full((1, LANES)), full((1, LANES)),
    ]
    args = [proj, proj, proj, dt, conv_w, conv_b, dt_bias, a_log]
    if final:
        in_specs = [pl.BlockSpec((L, W), lambda b, c: (rb(b, c), z_blk))] + in_specs + [
            full((1, W)), full((1, W)), pl.BlockSpec((L, W), lambda b, c: (rb(b, c), 0))]
        args = [proj] + args + [extra["d_skip"], extra["norm_w"], extra["y_other"]]
    body = functools.partial(_ssd_body, reverse=reverse, final=final, heads=heads, col_off=col_off)
    return pl.pallas_call(
        body,
        grid=(batch, nc),
        in_specs=in_specs,
        out_specs=pl.BlockSpec((L, W), lambda b, c: (rb(b, c), 0)),
        out_shape=jax.ShapeDtypeStruct((batch * seq, W), BF16 if final else F32),
        scratch_shapes=[
            pltpu.VMEM((L + 2 * SUBLANES, CC), F32),
            pltpu.VMEM((L, CC), F32),
            pltpu.VMEM((SSM_GROUPS, SSM_STATE, GW), F32),
        ],
        compiler_params=_params("parallel", "arbitrary"),
        name="ssd_final" if final else "ssd_scan",
    )(*args)


def _router_body(x_ref, nw_ref, wr_ref, br_ref, h_ref, idx_ref, gate_ref, cnt_ref, run_ref):
    tm, d = x_ref.shape
    half = d // 2

    @pl.when(pl.program_id(0) == 0)
    def _():
        run_ref[...] = jnp.zeros(run_ref.shape, F32)

    x = x_ref[...]
    ms = jnp.mean(x * x, axis=-1, keepdims=True)
    h = x * lax.rsqrt(ms + EPS) * nw_ref[...]
    h_ref[...] = _pack_pairs(h[:, :half], h[:, half:])
    logits = jnp.dot(h, wr_ref[...], precision=lax.Precision.HIGHEST, preferred_element_type=F32) + br_ref[...]
    lane = lax.broadcasted_iota(jnp.int32, logits.shape, 1).astype(F32)
    ninf = -jnp.inf
    is_g = lane < N_EXPERT_GROUPS
    gl = jnp.where(is_g, logits, ninf)
    gmax = jnp.max(gl, axis=-1, keepdims=True)
    gsum = jnp.sum(jnp.where(is_g, jnp.exp(gl - gmax), 0.0), axis=-1, keepdims=True)
    p_group = 1.0 / gsum
    g_sel = jnp.min(jnp.where(gl == gmax, lane, float(LANES)), axis=-1, keepdims=True)
    lo = N_EXPERT_GROUPS + EXPERTS_PER_GROUP * g_sel
    el = jnp.where((lane >= lo) & (lane < lo + EXPERTS_PER_GROUP), logits, ninf)
    m1 = jnp.max(el, axis=-1, keepdims=True)
    i1 = jnp.min(jnp.where(el == m1, lane, float(LANES)), axis=-1, keepdims=True)
    el2 = jnp.where(lane == i1, ninf, el)
    m2 = jnp.max(el2, axis=-1, keepdims=True)
    i2 = jnp.min(jnp.where(el2 == m2, lane, float(LANES)), axis=-1, keepdims=True)
    r = jnp.exp(m2 - m1)
    g1 = p_group / (1.0 + r)
    g2 = p_group * r / (1.0 + r)
    e1 = i1 - N_EXPERT_GROUPS
    e2 = i2 - N_EXPERT_GROUPS
    oh1 = jnp.where(lane == e1, 1.0, 0.0)
    oh2 = jnp.where(lane == e2, 1.0, 0.0)
    both = oh1 + oh2
    ri = lax.broadcasted_iota(jnp.int32, (tm, tm), 0)
    ci = lax.broadcasted_iota(jnp.int32, (tm, tm), 1)
    earlier = jnp.where(ci < ri, 1.0, 0.0).astype(BF16)
    before = jnp.dot(earlier, both.astype(BF16), preferred_element_type=F32) + run_ref[...]
    rank1 = jnp.sum(oh1 * before, axis=-1, keepdims=True)
    rank2 = jnp.sum(oh2 * before, axis=-1, keepdims=True)
    run_ref[...] = run_ref[...] + jnp.sum(both, axis=0, keepdims=True)
    cnt_ref[...] = run_ref[...]
    packed = jnp.where(lane == 0, e1, jnp.where(lane == 1, e2, jnp.where(lane == 2, rank1, jnp.where(lane == 3, rank2, 0.0))))
    idx_ref[...] = packed.astype(jnp.int32)
    gate_ref[...] = jnp.where(lane == 0, g1, jnp.where(lane == 1, g2, 0.0))


def norm_and_route(x, norm_w, w_router, b_router, tm):
    t, d = x.shape
    row = lambda w: pl.BlockSpec((tm, w), lambda i: (i, 0))
    one = lambda w: pl.BlockSpec((1, w), lambda i: (0, 0))
    return pl.pallas_call(
        _router_body,
        grid=(t // tm,),
        in_specs=[row(d), one(d), pl.BlockSpec((d, LANES), lambda i: (0, 0)), one(LANES)],
        out_specs=[row(d // 2), row(LANES), row(LANES), one(LANES)],
        out_shape=[jax.ShapeDtypeStruct((t, d // 2), U32), jax.ShapeDtypeStruct((t, LANES), jnp.int32),
                   jax.ShapeDtypeStruct((t, LANES), F32), jax.ShapeDtypeStruct((1, LANES), F32)],
        scratch_shapes=[pltpu.VMEM((1, LANES), F32)],
        compiler_params=_params("arbitrary"),
        name="router",
    )(x, norm_w.reshape(1, d).astype(F32), w_router, b_router)


def _moe_plan(idx, counts, tm):
    t = idx.shape[0]
    n = t * TOP_K
    experts = jnp.arange(N_EXPERTS, dtype=I32)
    counts = counts[0, :N_EXPERTS].astype(I32)
    padded = (counts + tm - 1) // tm * tm
    pend = jnp.cumsum(padded)
    pstart = pend - padded
    e = idx[:, :TOP_K]
    rank = idx[:, TOP_K:2 * TOP_K]
    dest = rank + jnp.sum(jnp.where(e[:, :, None] == experts, pstart, 0), axis=-1)
    nblk = n // tm + N_EXPERTS
    n_used = pend[-1] // tm
    blk = jnp.arange(nblk, dtype=I32)
    blk_e = jnp.minimum(jnp.sum((pend[None, :] <= (blk * tm)[:, None]).astype(I32), axis=-1), N_EXPERTS - 1)
    last_e = jnp.sum(jnp.where(blk == n_used - 1, blk_e, 0))
    blk_e = jnp.where(blk < n_used, blk_e, last_e)
    return dest.astype(I32), pend.astype(I32), blk_e.astype(I32), n_used.reshape(1).astype(I32), nblk * tm


def _row_copy_wait(src_row, dst_row, sem, n):
    def body(_, c):
        pltpu.make_async_copy(src_row, dst_row, sem).wait()
        return c
    lax.fori_loop(0, n, body, 0)


def _dispatch_body(pend_ref, nu_ref, dest_ref, h_ref, xs_hbm, idx_smem, zero_ref, sem, isem, *, tt, tm):
    i = pl.program_id(0)

    @pl.when(i == 0)
    def _():
        zero_ref[...] = jnp.zeros(zero_ref.shape, zero_ref.dtype)

        def zero_block(row0):
            cp = pltpu.make_async_copy(zero_ref, xs_hbm.at[pl.ds(pl.multiple_of(row0, tm), tm), :], sem)
            cp.start()
            cp.wait()

        def fill(e, c):
            end = pend_ref[e]
            prev = jnp.where(e > 0, pend_ref[jnp.maximum(e - 1, 0)], 0)

            @pl.when(end > prev)
            def _():
                zero_block(end - tm)
            return c

        lax.fori_loop(0, N_EXPERTS, fill, 0)

        def fill_unused(b, c):
            zero_block(b * tm)
            return c

        lax.fori_loop(nu_ref[0], xs_hbm.shape[0] // tm, fill_unused, 0)

    cp = pltpu.make_async_copy(dest_ref.at[0, 0], idx_smem, isem)
    cp.start()
    cp.wait()

    def issue(r, c):
        for k in range(TOP_K):
            d = idx_smem[TOP_K * r + k]
            pltpu.make_async_copy(h_ref.at[pl.ds(r, 1), :], xs_hbm.at[pl.ds(d, 1), :], sem).start()
        return c

    lax.fori_loop(0, tt, issue, 0)
    _row_copy_wait(h_ref.at[pl.ds(0, 1), :], xs_hbm.at[pl.ds(0, 1), :], sem, TOP_K * tt)


def dispatch_rows(pend, n_used, dest, h_packed, rows, tt, tm):
    t, w = h_packed.shape
    return pl.pallas_call(
        functools.partial(_dispatch_body, tt=tt, tm=tm),
        grid_spec=pltpu.PrefetchScalarGridSpec(
            num_scalar_prefetch=2,
            grid=(t // tt,),
            in_specs=[
                pl.BlockSpec((1, 1, TOP_K * tt), lambda i, pe, nu: (i, 0, 0)),
                pl.BlockSpec((tt, w), lambda i, pe, nu: (i, 0)),
            ],
            out_specs=pl.BlockSpec(memory_space=pl.ANY),
            scratch_shapes=[
                pltpu.SMEM((TOP_K * tt,), I32),
                pltpu.VMEM((tm, w), U32),
                pltpu.SemaphoreType.DMA,
                pltpu.SemaphoreType.DMA,
            ],
        ),
        out_shape=jax.ShapeDtypeStruct((rows, w), U32),
        compiler_params=_params("arbitrary"),
        name="dispatch_rows",
    )(pend, n_used, dest.reshape(t // tt, 1, TOP_K * tt), h_packed)


def _expert_changed(be_ref, i):
    return (i == 0) | (be_ref[i] != be_ref[jnp.maximum(i - 1, 0)])


def _gate_up_body(be_ref, nu_ref, x_ref, wg_ref, wu_ref, o_ref, wgb_ref, wub_ref):
    i = pl.program_id(1)
    half = x_ref.shape[1]

    @pl.when(_expert_changed(be_ref, i))
    def _():
        wgb_ref[...] = wg_ref[0].astype(BF16)
        wub_ref[...] = wu_ref[0].astype(BF16)

    @pl.when(i < nu_ref[0])
    def _():
        x_lo, x_hi = _unpack_pairs(x_ref[...])
        x_lo = x_lo.astype(BF16)
        x_hi = x_hi.astype(BF16)

        def mm(w_ref):
            return (jnp.dot(x_lo, w_ref[:half, :], preferred_element_type=F32)
                    + jnp.dot(x_hi, w_ref[half:, :], preferred_element_type=F32))

        a = mm(wgb_ref)
        b = mm(wub_ref)
        o_ref[...] = (a * _sigmoid(a) * b).astype(o_ref.dtype)

    @pl.when(i >= nu_ref[0])
    def _():
        o_ref[...] = jnp.zeros(o_ref.shape, o_ref.dtype)


def expert_gate_up(blk_e, n_used, xs, w_gate, w_up, tm, tf):
    rows = xs.shape[0]
    d, f = w_gate.shape[1], w_gate.shape[2]
    nblk = rows // tm
    last = lambda i, nu: jnp.minimum(i, nu[0] - 1)
    wspec = pl.BlockSpec((1, d, tf), lambda j, i, be, nu: (be[i], 0, j))
    return pl.pallas_call(
        _gate_up_body,
        grid_spec=pltpu.PrefetchScalarGridSpec(
            num_scalar_prefetch=2,
            grid=(f // tf, nblk),
            in_specs=[pl.BlockSpec((tm, d // 2), lambda j, i, be, nu: (last(i, nu), 0)), wspec, wspec],
            out_specs=pl.BlockSpec((tm, tf), lambda j, i, be, nu: (i, j)),
            scratch_shapes=[pltpu.VMEM((d, tf), BF16), pltpu.VMEM((d, tf), BF16)],
        ),
        out_shape=jax.ShapeDtypeStruct((rows, f), BF16),
        compiler_params=_params("arbitrary", "arbitrary"),
        name="expert_gate_up",
    )(blk_e, n_used, xs, w_gate, w_up)


def _down_body(be_ref, nu_ref, a_ref, wd_ref, o_ref, wdb_ref):
    i = pl.program_id(0)
    half = o_ref.shape[1]

    @pl.when(_expert_changed(be_ref, i))
    def _():
        wdb_ref[...] = wd_ref[0].astype(BF16)

    @pl.when(i < nu_ref[0])
    def _():
        y = jnp.dot(a_ref[...], wdb_ref[...], preferred_element_type=F32)
        o_ref[...] = _pack_pairs(y[:, :half], y[:, half:])

    @pl.when(i >= nu_ref[0])
    def _():
        o_ref[...] = jnp.zeros(o_ref.shape, o_ref.dtype)


def expert_down(blk_e, n_used, act, w_down, tm):
    rows, f = act.shape
    d = w_down.shape[2]
    nblk = rows // tm
    last = lambda i, nu: jnp.minimum(i, nu[0] - 1)
    return pl.pallas_call(
        _down_body,
        grid_spec=pltpu.PrefetchScalarGridSpec(
            num_scalar_prefetch=2,
            grid=(nblk,),
            in_specs=[
                pl.BlockSpec((tm, f), lambda i, be, nu: (last(i, nu), 0)),
                pl.BlockSpec((1, f, d), lambda i, be, nu: (be[i], 0, 0)),
            ],
            out_specs=pl.BlockSpec((tm, d // 2), lambda i, be, nu: (i, 0)),
            scratch_shapes=[pltpu.VMEM((f, d), BF16)],
        ),
        out_shape=jax.ShapeDtypeStruct((rows, d // 2), U32),
        compiler_params=_params("arbitrary"),
        name="expert_down",
    )(blk_e, n_used, act, w_down)


def _combine_body(dest_ref, x_ref, g_ref, w_ref, y_hbm, o_ref, idx_smem, buf_ref, sem, isem, *, tt):
    d = x_ref.shape[1]
    half = d // 2
    cp = pltpu.make_async_copy(dest_ref.at[0, 0], idx_smem, isem)
    cp.start()
    cp.wait()

    def issue(r, c):
        for k in range(TOP_K):
            src = idx_smem[TOP_K * r + k]
            pltpu.make_async_copy(y_hbm.at[pl.ds(src, 1), :], buf_ref.at[k, pl.ds(r, 1), :], sem).start()
        return c

    lax.fori_loop(0, tt, issue, 0)
    _row_copy_wait(y_hbm.at[pl.ds(0, 1), :], buf_ref.at[0, pl.ds(0, 1), :], sem, TOP_K * tt)

    y0_lo, y0_hi = _unpack_pairs(buf_ref[0])
    y1_lo, y1_hi = _unpack_pairs(buf_ref[1])
    g0 = g_ref[:, 0:1]
    g1 = g_ref[:, 1:2]
    x_lo = x_ref[:, :half] + (g0 * y0_lo + g1 * y1_lo)
    x_hi = x_ref[:, half:] + (g0 * y0_hi + g1 * y1_hi)
    ms = (jnp.sum(x_lo * x_lo, axis=-1, keepdims=True) + jnp.sum(x_hi * x_hi, axis=-1, keepdims=True)) / d
    inv = lax.rsqrt(ms + EPS)
    o_ref[:, :half] = x_lo * inv * w_ref[:, :half]
    o_ref[:, half:] = x_hi * inv * w_ref[:, half:]


def combine_and_norm(x, y_packed, dest, gate, w, tt):
    t, d = x.shape
    row = lambda wd: pl.BlockSpec((tt, wd), lambda i: (i, 0))
    return pl.pallas_call(
        functools.partial(_combine_body, tt=tt),
        grid=(t // tt,),
        in_specs=[
            pl.BlockSpec((1, 1, TOP_K * tt), lambda i: (i, 0, 0)),
            row(d), row(LANES), pl.BlockSpec((1, d), lambda i: (0, 0)),
            pl.BlockSpec(memory_space=pl.ANY),
        ],
        out_specs=row(d),
        out_shape=jax.ShapeDtypeStruct((t, d), F32),
        scratch_shapes=[
            pltpu.SMEM((TOP_K * tt,), I32),
            pltpu.VMEM((TOP_K, tt, d // 2), U32),
            pltpu.SemaphoreType.DMA,
            pltpu.SemaphoreType.DMA,
        ],
        compiler_params=_params("arbitrary"),
        name="combine_norm",
    )(dest.reshape(t // tt, 1, TOP_K * tt), x, gate, w.reshape(1, d).astype(F32), y_packed)


def _tiles(batch, seq):
    t = batch * seq
    return dict(
        norm_tm=min(256, t),
        mm_tm=min(1024, t),
        mm_tn=1024,
        out_tn=512,
        attn_tq=min(512, seq),
        attn_tk=min(512, seq),
        moe_tm=min(256, t),
        moe_tf=512,
        disp_tt=min(512, t),
        comb_tt=min(256, t),
    )


def _lambda_init_at(layer):
    return 0.8 - 0.6 * math.exp(-0.3 * layer)


def kernel(x, rel_bias, norm1_w, w_in, lambda_q1, lambda_k1, lambda_q2, lambda_k2, subln_w, conv_w, conv_b,
           dt_bias_f, dt_bias_b, a_log_f, a_log_b, d_skip, ssm_norm_w, w_out, norm2_w, w_group_router,
           b_group_router, w_expert_router, b_expert_router, w_gate, w_up, w_down, final_norm_w):
    batch, seq, d = x.shape
    t = batch * seq
    depth = norm1_w.shape[0]
    attn_w = d // 2
    ssm_w = d - attn_w
    dv = 2 * ATTN_HEAD_DIM
    a_heads = attn_w // dv
    s_heads = ssm_w // SSM_HEAD_DIM
    cc = ssm_w + 2 * SSM_GROUPS * SSM_STATE
    main_w = 3 * attn_w + ssm_w + cc
    assert ssm_w == attn_w and cc == 2 * ssm_w and 2 * s_heads <= LANES
    tl = _tiles(batch, seq)
    band = _band_table(rel_bias, tl["attn_tq"], tl["attn_tk"])

    def pad_lanes(v):
        return jnp.pad(v.astype(F32), (0, LANES - v.shape[0])).reshape(1, LANES)

    xf = x.reshape(t, d)
    for layer in range(depth):
        lam_init = _lambda_init_at(layer)
        wl = w_in[layer]
        w_main = wl[:, :main_w].astype(BF16)
        w_dt = jnp.pad(wl[:, main_w:], ((0, 0), (0, LANES - 2 * s_heads))).astype(BF16)
        h = rmsnorm_rows(xf, norm1_w[layer], BF16, tl["norm_tm"])
        proj = matmul(h, w_main, BF16, tl["mm_tm"], tl["mm_tn"], "in_proj")
        dt = matmul(h, w_dt, F32, tl["mm_tm"], LANES, "dt_proj")

        vec = lambda v: v[layer].reshape(1, -1).astype(F32)
        attn = diff_attention(proj, band, vec(lambda_q1), vec(lambda_k1), vec(lambda_q2), vec(lambda_k2),
                              vec(subln_w), batch=batch, seq=seq, heads=a_heads, lam_init=lam_init,
                              tq=tl["attn_tq"], tk=tl["attn_tk"])

        dt_bias = pad_lanes(jnp.concatenate([dt_bias_f[layer], dt_bias_b[layer]]))
        a_log = pad_lanes(jnp.concatenate([a_log_f[layer], a_log_b[layer]]))
        cw = conv_w[layer].astype(F32)
        cb = conv_b[layer].reshape(1, cc).astype(F32)
        common = dict(batch=batch, seq=seq, heads=s_heads, z_blk=3 * attn_w // ssm_w, xbc_blk=(3 * attn_w + ssm_w) // cc)
        y_bwd = ssd_pass(proj, dt, cw, cb, dt_bias, a_log, None, reverse=True, final=False, col_off=s_heads, **common)
        extra = dict(d_skip=jnp.repeat(d_skip[layer].astype(F32), SSM_HEAD_DIM).reshape(1, ssm_w),
                     norm_w=vec(ssm_norm_w), y_other=y_bwd)
        ssm = ssd_pass(proj, dt, cw, cb, dt_bias, a_log, extra, reverse=False, final=True, col_off=0, **common)

        x1 = out_projection(attn, ssm, w_out[layer].astype(BF16), xf, tl["mm_tm"], tl["out_tn"])

        w_router = jnp.pad(jnp.concatenate([w_group_router[layer], w_expert_router[layer]], axis=1).astype(F32),
                           ((0, 0), (0, LANES - N_EXPERT_GROUPS - N_EXPERTS)))
        b_router = pad_lanes(jnp.concatenate([b_group_router[layer], b_expert_router[layer]]))
        h2, idx, gate, counts = norm_and_route(x1, norm2_w[layer], w_router, b_router, tl["norm_tm"])
        tm = tl["moe_tm"]
        dest, pend, blk_e, n_used, rows = _moe_plan(idx, counts, tm)
        xs = dispatch_rows(pend, n_used, dest, h2, rows, tl["disp_tt"], tm)
        act = expert_gate_up(blk_e, n_used, xs, w_gate[layer], w_up[layer], tm, tl["moe_tf"])
        y = expert_down(blk_e, n_used, act, w_down[layer], tm)
        if layer + 1 < depth:
            raise NotImplementedError("multi-layer stacking needs an un-normalised combine")
        xf = combine_and_norm(x1, y, dest, gate, final_norm_w, tl["comb_tt"])
    return xf.reshape(batch, seq, d)
```

```python
import functools
import math

import jax
import jax.numpy as jnp
from jax import lax
from jax.experimental import pallas as pl
from jax.experimental.pallas import tpu as pltpu

F32 = jnp.float32
BF16 = jnp.bfloat16
U32 = jnp.uint32
I32 = jnp.int32
EPS = 1e-6

ATTN_HEAD_DIM = 128
NUM_BUCKETS = 32
MAX_DISTANCE = 128
SSM_HEAD_DIM = 64
SSM_GROUPS = 8
SSM_STATE = 128
CONV_WIDTH = 5
CHUNK = 128
N_EXPERT_GROUPS = 4
EXPERTS_PER_GROUP = 8
N_EXPERTS = N_EXPERT_GROUPS * EXPERTS_PER_GROUP
TOP_K = 2

LANES = 128
SUBLANES = 8
BF16_ROWS = 16
VMEM_LIMIT = 56 * 1024 * 1024
HI16 = 0xFFFF0000


def _params(*sem):
    return pltpu.CompilerParams(dimension_semantics=sem, vmem_limit_bytes=VMEM_LIMIT)


def _sigmoid(x):
    return 1.0 / (1.0 + jnp.exp(-x))


def _pack_pairs(lo, hi):
    lo_b = lax.bitcast_convert_type(lo.astype(BF16).astype(F32), U32)
    hi_b = lax.bitcast_convert_type(hi.astype(BF16).astype(F32), U32)
    return (lo_b >> 16) | (hi_b & U32(HI16))


def _unpack_pairs(w):
    return lax.bitcast_convert_type(w << 16, F32), lax.bitcast_convert_type(w & U32(HI16), F32)


def _rmsnorm_body(x_ref, w_ref, o_ref):
    x = x_ref[...]
    ms = jnp.mean(x * x, axis=-1, keepdims=True)
    o_ref[...] = (x * lax.rsqrt(ms + EPS) * w_ref[...]).astype(o_ref.dtype)


def rmsnorm_rows(x, w, out_dtype, tm):
    t, d = x.shape
    return pl.pallas_call(
        _rmsnorm_body,
        grid=(t // tm,),
        in_specs=[pl.BlockSpec((tm, d), lambda i: (i, 0)), pl.BlockSpec((1, d), lambda i: (0, 0))],
        out_specs=pl.BlockSpec((tm, d), lambda i: (i, 0)),
        out_shape=jax.ShapeDtypeStruct((t, d), out_dtype),
        compiler_params=_params("parallel"),
        name="rmsnorm",
    )(x, w.reshape(1, d).astype(F32))


def _matmul_body(x_ref, w_ref, o_ref):
    o_ref[...] = jnp.dot(x_ref[...], w_ref[...], preferred_element_type=F32).astype(o_ref.dtype)


def matmul(x, w, out_dtype, tm, tn, name):
    m, k = x.shape
    n = w.shape[1]
    return pl.pallas_call(
        _matmul_body,
        grid=(m // tm, n // tn),
        in_specs=[pl.BlockSpec((tm, k), lambda i, j: (i, 0)), pl.BlockSpec((k, tn), lambda i, j: (0, j))],
        out_specs=pl.BlockSpec((tm, tn), lambda i, j: (i, j)),
        out_shape=jax.ShapeDtypeStruct((m, n), out_dtype),
        compiler_params=_params("parallel", "parallel"),
        name=name,
    )(x, w)


def _outproj_body(a_ref, s_ref, wa_ref, ws_ref, r_ref, o_ref):
    acc = jnp.dot(a_ref[...], wa_ref[...], preferred_element_type=F32)
    acc = acc + jnp.dot(s_ref[...], ws_ref[...], preferred_element_type=F32)
    o_ref[...] = r_ref[...] + acc


def out_projection(attn, ssm, w, resid, tm, tn):
    m, ka = attn.shape
    ks = ssm.shape[1]
    assert ka == ks
    n = w.shape[1]
    return pl.pallas_call(
        _outproj_body,
        grid=(m // tm, n // tn),
        in_specs=[
            pl.BlockSpec((tm, ka), lambda i, j: (i, 0)),
            pl.BlockSpec((tm, ks), lambda i, j: (i, 0)),
            pl.BlockSpec((ka, tn), lambda i, j: (0, j)),
            pl.BlockSpec((ks, tn), lambda i, j: (1, j)),
            pl.BlockSpec((tm, tn), lambda i, j: (i, j)),
        ],
        out_specs=pl.BlockSpec((tm, tn), lambda i, j: (i, j)),
        out_shape=jax.ShapeDtypeStruct((m, n), F32),
        compiler_params=_params("parallel", "parallel"),
        name="out_proj",
    )(attn, ssm, w, w, resid)


def _t5_bucket(rel):
    half = NUM_BUCKETS // 2
    max_exact = half // 2
    n = jnp.abs(rel)
    large = max_exact + (
        jnp.log(jnp.maximum(n, 1).astype(F32) / max_exact) / math.log(MAX_DISTANCE / max_exact) * (half - max_exact)
    ).astype(jnp.int32)
    large = jnp.minimum(large, half - 1)
    return jnp.where(rel > 0, half, 0) + jnp.where(n < max_exact, n, large)


def _band_body(v_ref, o_ref, *, tq, w):
    npad = v_ref.shape[-1]
    x = jnp.broadcast_to(v_ref[0], (tq, npad))
    y = pltpu.roll(x, npad - (tq - 1), 1, stride=1, stride_axis=0)
    o_ref[0] = y[:, :w].T


def _band_table(rel_bias, tq, tk, scale):
    assert tk + 1 >= MAX_DISTANCE
    heads = rel_bias.shape[1]
    w = tq + 4 * tk
    n = w + tq - 1
    npad = -(-n // LANES) * LANES
    rel = jnp.arange(npad, dtype=jnp.int32) - (2 * tk + tq - 1)
    v = (rel_bias[_t5_bucket(rel)].astype(F32) * scale).T.reshape(heads, 1, npad)
    return pl.pallas_call(
        functools.partial(_band_body, tq=tq, w=w),
        grid=(heads,),
        in_specs=[pl.BlockSpec((1, 1, npad), lambda h: (h, 0, 0))],
        out_specs=pl.BlockSpec((1, w, tq), lambda h: (h, 0, 0)),
        out_shape=jax.ShapeDtypeStruct((heads, w, tq), F32),
        compiler_params=_params("parallel"),
        name="band_table",
    )(v)


ATTN_STRIP = 32


def _attn_body(q_ref, k_ref, v_ref, band_ref, lq1_ref, lk1_ref, lq2_ref, lk2_ref, subw_ref, o_ref,
               vt_ref, m_ref, l_ref, acc_ref, sa_ref, sb_ref, p_ref, *, tq, tk, nk, lam_init):
    dh = ATTN_HEAD_DIM
    rs = ATTN_STRIP
    qi = pl.program_id(2)

    @pl.when(qi == 0)
    def _():
        def tr(c, carry):
            r0 = pl.multiple_of(c * tk, tk)
            vt_ref[:, pl.ds(r0, tk)] = v_ref[pl.ds(r0, tk), :].astype(F32).T.astype(BF16)
            return carry
        lax.fori_loop(0, nk, tr, 0)

    m_ref[...] = jnp.full(m_ref.shape, -jnp.inf, F32)
    l_ref[...] = jnp.zeros(l_ref.shape, F32)
    acc_ref[...] = jnp.zeros(acc_ref.shape, F32)

    def scores(kc, s_ref):
        k0 = pl.multiple_of(kc * tk, tk)
        start = pl.multiple_of(jnp.clip(kc * tk - qi * tq + 2 * tk, 0, tq + 3 * tk), LANES)
        for mi in range(2):
            kk = k_ref[pl.ds(k0, tk), mi * dh:(mi + 1) * dh]
            q = q_ref[:, mi * dh:(mi + 1) * dh]
            s_ref[mi] = (lax.dot_general(kk, q, (((1,), (1,)), ((), ())), preferred_element_type=F32)
                         + band_ref[0, pl.ds(start, tk), :])

    def softmax_pv(kc, s_ref):
        k0 = pl.multiple_of(kc * tk, tk)
        for mi in range(2):
            def strip_max(i, mx):
                return jnp.maximum(mx, s_ref[mi, pl.ds(i * rs, rs), :])

            mx = lax.fori_loop(0, tk // rs, strip_max, jnp.full((rs, tq), -jnp.inf, F32), unroll=True)
            m_prev = m_ref[mi]
            m_new = jnp.maximum(m_prev, jnp.max(mx, axis=0, keepdims=True))
            alpha = jnp.exp2(m_prev - m_new)

            def strip_exp(i, ls):
                p = jnp.exp2(s_ref[mi, pl.ds(i * rs, rs), :] - m_new)
                p_ref[mi, pl.ds(i * rs, rs), :] = p.astype(BF16)
                return ls + p

            ls = lax.fori_loop(0, tk // rs, strip_exp, jnp.zeros((rs, tq), F32), unroll=True)
            l_ref[mi] = alpha * l_ref[mi] + jnp.sum(ls, axis=0, keepdims=True)
            m_ref[mi] = m_new
            pv = jnp.dot(vt_ref[:, pl.ds(k0, tk)], p_ref[mi], preferred_element_type=F32)
            acc_ref[mi] = acc_ref[mi] * alpha + pv

    scores(0, sa_ref)
    if nk > 1:
        assert nk % 2 == 0

        def pair(j, carry):
            scores(2 * j + 1, sb_ref)
            softmax_pv(2 * j, sa_ref)
            scores(2 * j + 2, sa_ref)
            softmax_pv(2 * j + 1, sb_ref)
            return carry

        lax.fori_loop(0, nk // 2 - 1, pair, 0)
        scores(nk - 1, sb_ref)
        softmax_pv(nk - 2, sa_ref)
        softmax_pv(nk - 1, sb_ref)
    else:
        softmax_pv(0, sa_ref)

    lam = (jnp.exp(jnp.sum(lq1_ref[...] * lk1_ref[...], axis=-1, keepdims=True))
           - jnp.exp(jnp.sum(lq2_ref[...] * lk2_ref[...], axis=-1, keepdims=True)) + lam_init)
    o = acc_ref[0] / l_ref[0] - lam * (acc_ref[1] / l_ref[1])
    ms = jnp.mean(o * o, axis=0, keepdims=True)
    o = (o * lax.rsqrt(ms + EPS)).T * subw_ref[...]
    o_ref[...] = (o * (1.0 - lam_init)).astype(o_ref.dtype)


def diff_attention(proj, band, lq1, lk1, lq2, lk2, subw, *, batch, seq, heads, lam_init, tq, tk):
    dv = 2 * ATTN_HEAD_DIM
    nq, nk = seq // tq, seq // tk
    assert tk % ATTN_STRIP == 0
    vec = pl.BlockSpec((1, ATTN_HEAD_DIM), lambda b, h, i: (0, 0))
    body = functools.partial(_attn_body, tq=tq, tk=tk, nk=nk, lam_init=lam_init)
    return pl.pallas_call(
        body,
        grid=(batch, heads, nq),
        in_specs=[
            pl.BlockSpec((tq, dv), lambda b, h, i: (b * nq + i, h)),
            pl.BlockSpec((seq, dv), lambda b, h, i: (b, heads + h)),
            pl.BlockSpec((seq, dv), lambda b, h, i: (b, 2 * heads + h)),
            pl.BlockSpec((1, tq + 4 * tk, tq), lambda b, h, i: (h, 0, 0)),
            vec, vec, vec, vec,
            pl.BlockSpec((1, dv), lambda b, h, i: (0, 0)),
        ],
        out_specs=pl.BlockSpec((tq, dv), lambda b, h, i: (b * nq + i, h)),
        out_shape=jax.ShapeDtypeStruct((batch * seq, heads * dv), BF16),
        scratch_shapes=[
            pltpu.VMEM((dv, seq), BF16),
            pltpu.VMEM((2, 1, tq), F32),
            pltpu.VMEM((2, 1, tq), F32),
            pltpu.VMEM((2, dv, tq), F32),
            pltpu.VMEM((2, tk, tq), F32),
            pltpu.VMEM((2, tk, tq), F32),
            pltpu.VMEM((2, tk, tq), BF16),
        ],
        compiler_params=_params("parallel", "parallel", "arbitrary"),
        name="diff_attention",
    )(proj, proj, proj, band, lq1, lk1, lq2, lk2, subw)


def _ssd_body(*refs, reverse, final, heads, col_off):
    if final:
        (z_ref, act_in_ref, dt_ref, dtb_ref, alog_ref, dskip_ref, nw_ref, yb_ref, o_ref, state_ref) = refs
    else:
        (xbc_ref, hp_ref, hn_ref, dt_ref, cw_ref, cb_ref, dtb_ref, alog_ref, o_ref, act_out_ref, u_ref, act_ref,
         state_ref) = refs
    L = CHUNK
    G, N, P = SSM_GROUPS, SSM_STATE, SSM_HEAD_DIM
    R = heads // G
    GW = R * P
    W = heads * P
    CC = W + 2 * G * N
    HALO = SUBLANES
    c = pl.program_id(1)
    nc = pl.num_programs(1)
    pos = (nc - 1 - c) if reverse else c

    @pl.when(c == 0)
    def _():
        state_ref[...] = jnp.zeros(state_ref.shape, F32)

    if final:
        def chan(lo, hi):
            return act_in_ref[:, lo:hi].astype(F32)
    else:
        prev = hp_ref[...].astype(F32)[BF16_ROWS - HALO:, :]
        nxt = hn_ref[...].astype(F32)[:HALO, :]
        u_ref[0:HALO, :] = jnp.where(pos > 0, prev, 0.0)
        u_ref[HALO:HALO + L, :] = xbc_ref[...].astype(F32)
        u_ref[HALO + L:, :] = jnp.where(pos < nc - 1, nxt, 0.0)
        SLAB = 2 * LANES

        def conv_slab(j, carry):
            lo = pl.multiple_of(j * SLAB, SLAB)
            acc = jnp.broadcast_to(cb_ref[:, pl.ds(lo, SLAB)], (L, SLAB))
            for t in range(CONV_WIDTH):
                r0 = HALO - CONV_WIDTH // 2 + t
                acc = acc + cw_ref[t:t + 1, pl.ds(lo, SLAB)] * u_ref[r0:r0 + L, pl.ds(lo, SLAB)]
            act = acc * _sigmoid(acc)
            act_ref[:, pl.ds(lo, SLAB)] = act
            act_out_ref[:, pl.ds(lo, SLAB)] = act.astype(act_out_ref.dtype)
            return carry

        lax.fori_loop(0, CC // SLAB, conv_slab, 0)

        def chan(lo, hi):
            return act_ref[:, lo:hi]

    xdt = dt_ref[...] + dtb_ref[...]
    dtv = jnp.maximum(xdt, 0.0) + jnp.log1p(jnp.exp(-jnp.abs(xdt)))
    a = -jnp.exp(alog_ref[...]) * dtv
    row = lax.broadcasted_iota(jnp.int32, (L, L), 0)
    col = lax.broadcasted_iota(jnp.int32, (L, L), 1)
    keep = (row <= col) if reverse else (row >= col)
    cum = jnp.dot(keep.astype(F32), a, precision=lax.Precision.HIGHEST, preferred_element_type=F32)
    cum_t = cum.T
    total = cum[0:1, :] if reverse else cum[L - 1:L, :]
    seg = lax.broadcasted_iota(jnp.int32, (L, GW), 1) // P
    seg1 = lax.broadcasted_iota(jnp.int32, (1, GW), 1) // P

    for g in range(G):
        xs = chan(g * GW, (g + 1) * GW)
        bmat = chan(W + g * N, W + (g + 1) * N)
        cmat = chan(W + (G + g) * N, W + (G + g + 1) * N)
        b_bf = bmat.astype(BF16)
        c_bf = cmat.astype(BF16)
        cb = lax.dot_general(c_bf, b_bf, (((1,), (1,)), ((), ())), preferred_element_type=F32)
        bt_bf = bmat.T.astype(BF16)
        dt_e = jnp.zeros((L, GW), F32)
        cum_e = jnp.zeros((L, GW), F32)
        tot_e = jnp.zeros((1, GW), F32)
        for r in range(R):
            hc = col_off + g * R + r
            dt_e = jnp.where(seg == r, dtv[:, hc:hc + 1], dt_e)
            cum_e = jnp.where(seg == r, cum[:, hc:hc + 1], cum_e)
            tot_e = jnp.where(seg1 == r, total[:, hc:hc + 1], tot_e)
        x_dt = xs * dt_e
        y = jnp.zeros((L, GW), F32)
        for r in range(R):
            hc = col_off + g * R + r
            decay = jnp.exp(jnp.where(keep, cum[:, hc:hc + 1] - cum_t[hc:hc + 1, :], -jnp.inf))
            m_h = (cb * decay).astype(BF16)
            x_h = jnp.where(seg == r, x_dt, 0.0).astype(BF16)
            y = y + jnp.dot(m_h, x_h, preferred_element_type=F32)
        s_prev = state_ref[g]
        y = y + jnp.dot(c_bf, s_prev.astype(BF16), preferred_element_type=F32) * jnp.exp(cum_e)
        x_end = (x_dt * jnp.exp(tot_e - cum_e)).astype(BF16)
        state_ref[g] = s_prev * jnp.exp(tot_e) + jnp.dot(bt_bf, x_end, preferred_element_type=F32)
        if final:
            y = y + yb_ref[:, g * GW:(g + 1) * GW] + dskip_ref[:, g * GW:(g + 1) * GW] * xs
            zg = z_ref[:, g * GW:(g + 1) * GW].astype(F32)
            y = y * (zg * _sigmoid(zg))
            ms = jnp.mean(y * y, axis=-1, keepdims=True)
            y = y * lax.rsqrt(ms + EPS) * nw_ref[:, g * GW:(g + 1) * GW]
        o_ref[:, g * GW:(g + 1) * GW] = y.astype(o_ref.dtype)


def ssd_pass(proj, dt, conv_w, conv_b, dt_bias, a_log, extra, *, batch, seq, heads, z_blk, xbc_blk, reverse, final,
             col_off):
    L = CHUNK
    W = heads * SSM_HEAD_DIM
    CC = W + 2 * SSM_GROUPS * SSM_STATE
    GW = W // SSM_GROUPS
    nc = seq // L
    nhalo = batch * seq // BF16_ROWS
    per = L // BF16_ROWS

    def rb(b, c):
        return b * nc + ((nc - 1 - c) if reverse else c)

    full = lambda shape: pl.BlockSpec(shape, lambda b, c: (0,) * len(shape))
    rows = lambda w, blk: pl.BlockSpec((L, w), lambda b, c: (rb(b, c), blk))
    state = pltpu.VMEM((SSM_GROUPS, SSM_STATE, GW), F32)
    body = functools.partial(_ssd_body, reverse=reverse, final=final, heads=heads, col_off=col_off)
    t = batch * seq
    if final:
        in_specs = [rows(W, z_blk), rows(CC, 0), rows(LANES, 0), full((1, LANES)), full((1, LANES)),
                    full((1, W)), full((1, W)), rows(W, 0)]
        args = [proj, extra["act"], dt, dt_bias, a_log, extra["d_skip"], extra["norm_w"], extra["y_other"]]
        out_specs = rows(W, 0)
        out_shape = jax.ShapeDtypeStruct((t, W), BF16)
        scratch = [state]
    else:
        in_specs = [
            rows(CC, xbc_blk),
            pl.BlockSpec((BF16_ROWS, CC), lambda b, c: (jnp.maximum(rb(b, c) * per - 1, 0), xbc_blk)),
            pl.BlockSpec((BF16_ROWS, CC), lambda b, c: (jnp.minimum((rb(b, c) + 1) * per, nhalo - 1), xbc_blk)),
            rows(LANES, 0),
            full((CONV_WIDTH, CC)), full((1, CC)), full((1, LANES)), full((1, LANES)),
        ]
        args = [proj, proj, proj, dt, conv_w, conv_b, dt_bias, a_log]
        out_specs = [rows(W, 0), rows(CC, 0)]
        out_shape = [jax.ShapeDtypeStruct((t, W), F32), jax.ShapeDtypeStruct((t, CC), BF16)]
        scratch = [pltpu.VMEM((L + 2 * SUBLANES, CC), F32), pltpu.VMEM((L, CC), F32), state]
    return pl.pallas_call(
        body,
        grid=(batch, nc),
        in_specs=in_specs,
        out_specs=out_specs,
        out_shape=out_shape,
        scratch_shapes=scratch,
        compiler_params=_params("parallel", "arbitrary"),
        name="ssd_final" if final else "ssd_scan",
    )(*args)


def _router_body(x_ref, nw_ref, wr_ref, br_ref, h_ref, idx_ref, gate_ref, cnt_ref, run_ref):
    tm, d = x_ref.shape
    half = d // 2

    @pl.when(pl.program_id(0) == 0)
    def _():
        run_ref[...] = jnp.zeros(run_ref.shape, F32)

    x = x_ref[...]
    ms = jnp.mean(x * x, axis=-1, keepdims=True)
    h = x * lax.rsqrt(ms + EPS) * nw_ref[...]
    h_ref[...] = _pack_pairs(h[:, :half], h[:, half:])
    logits = jnp.dot(h, wr_ref[...], precision=lax.Precision.HIGHEST, preferred_element_type=F32) + br_ref[...]
    lane = lax.broadcasted_iota(jnp.int32, logits.shape, 1).astype(F32)
    ninf = -jnp.inf
    is_g = lane < N_EXPERT_GROUPS
    gl = jnp.where(is_g, logits, ninf)
    gmax = jnp.max(gl, axis=-1, keepdims=True)
    gsum = jnp.sum(jnp.where(is_g, jnp.exp(gl - gmax), 0.0), axis=-1, keepdims=True)
    p_group = 1.0 / gsum
    g_sel = jnp.min(jnp.where(gl == gmax, lane, float(LANES)), axis=-1, keepdims=True)
    lo = N_EXPERT_GROUPS + EXPERTS_PER_GROUP * g_sel
    el = jnp.where((lane >= lo) & (lane < lo + EXPERTS_PER_GROUP), logits, ninf)
    m1 = jnp.max(el, axis=-1, keepdims=True)
    i1 = jnp.min(jnp.where(el == m1, lane, float(LANES)), axis=-1, keepdims=True)
    el2 = jnp.where(lane == i1, ninf, el)
    m2 = jnp.max(el2, axis=-1, keepdims=True)
    i2 = jnp.min(jnp.where(el2 == m2, lane, float(LANES)), axis=-1, keepdims=True)
    r = jnp.exp(m2 - m1)
    g1 = p_group / (1.0 + r)
    g2 = p_group * r / (1.0 + r)
    e1 = i1 - N_EXPERT_GROUPS
    e2 = i2 - N_EXPERT_GROUPS
    oh1 = jnp.where(lane == e1, 1.0, 0.0)
    oh2 = jnp.where(lane == e2, 1.0, 0.0)
    both = oh1 + oh2
    ri = lax.broadcasted_iota(jnp.int32, (tm, tm), 0)
    ci = lax.broadcasted_iota(jnp.int32, (tm, tm), 1)
    earlier = jnp.where(ci < ri, 1.0, 0.0).astype(BF16)
    before = jnp.dot(earlier, both.astype(BF16), preferred_element_type=F32) + run_ref[...]
    rank1 = jnp.sum(oh1 * before, axis=-1, keepdims=True)
    rank2 = jnp.sum(oh2 * before, axis=-1, keepdims=True)
    run_ref[...] = run_ref[...] + jnp.sum(both, axis=0, keepdims=True)
    cnt_ref[...] = run_ref[...]
    packed = jnp.where(lane == 0, e1, jnp.where(lane == 1, e2, jnp.where(lane == 2, rank1, jnp.where(lane == 3, rank2, 0.0))))
    idx_ref[...] = packed.astype(jnp.int32)
    gate_ref[...] = jnp.where(lane == 0, g1, jnp.where(lane == 1, g2, 0.0))


def norm_and_route(x, norm_w, w_router, b_router, tm):
    t, d = x.shape
    row = lambda w: pl.BlockSpec((tm, w), lambda i: (i, 0))
    one = lambda w: pl.BlockSpec((1, w), lambda i: (0, 0))
    return pl.pallas_call(
        _router_body,
        grid=(t // tm,),
        in_specs=[row(d), one(d), pl.BlockSpec((d, LANES), lambda i: (0, 0)), one(LANES)],
        out_specs=[row(d // 2), row(LANES), row(LANES), one(LANES)],
        out_shape=[jax.ShapeDtypeStruct((t, d // 2), U32), jax.ShapeDtypeStruct((t, LANES), jnp.int32),
                   jax.ShapeDtypeStruct((t, LANES), F32), jax.ShapeDtypeStruct((1, LANES), F32)],
        scratch_shapes=[pltpu.VMEM((1, LANES), F32)],
        compiler_params=_params("arbitrary"),
        name="router",
    )(x, norm_w.reshape(1, d).astype(F32), w_router, b_router)


def _moe_plan(idx, counts, tm):
    t = idx.shape[0]
    n = t * TOP_K
    experts = jnp.arange(N_EXPERTS, dtype=I32)
    counts = counts[0, :N_EXPERTS].astype(I32)
    padded = (counts + tm - 1) // tm * tm
    pend = jnp.cumsum(padded)
    pstart = pend - padded
    e = idx[:, :TOP_K]
    rank = idx[:, TOP_K:2 * TOP_K]
    dest = rank + jnp.sum(jnp.where(e[:, :, None] == experts, pstart, 0), axis=-1)
    nblk = n // tm + N_EXPERTS
    n_used = pend[-1] // tm
    blk = jnp.arange(nblk, dtype=I32)
    blk_e = jnp.minimum(jnp.sum((pend[None, :] <= (blk * tm)[:, None]).astype(I32), axis=-1), N_EXPERTS - 1)
    last_e = jnp.sum(jnp.where(blk == n_used - 1, blk_e, 0))
    blk_e = jnp.where(blk < n_used, blk_e, last_e)
    later = (counts[None, :] > 0) & (experts[None, :] > experts[:, None])
    next_of = jnp.min(jnp.where(later, experts[None, :], N_EXPERTS), axis=1)
    next_of = jnp.where(next_of < N_EXPERTS, next_of, -1)
    blk_next = jnp.sum(jnp.where(blk_e[:, None] == experts, next_of[None, :], 0), axis=1)
    return (dest.astype(I32), pend.astype(I32), blk_e.astype(I32), blk_next.astype(I32),
            n_used.reshape(1).astype(I32), nblk * tm)


def _dispatch_body(pend_ref, nu_ref, dest_ref, h_ref, xs_hbm, idx_smem, zero_ref, sem, isem, *, tt, tm):
    i = pl.program_id(0)

    @pl.when(i == 0)
    def _():
        zero_ref[...] = jnp.zeros(zero_ref.shape, zero_ref.dtype)

        def zero_block(row0):
            cp = pltpu.make_async_copy(zero_ref, xs_hbm.at[pl.ds(pl.multiple_of(row0, tm), tm), :], sem)
            cp.start()
            cp.wait()

        def fill(e, c):
            end = pend_ref[e]
            prev = jnp.where(e > 0, pend_ref[jnp.maximum(e - 1, 0)], 0)

            @pl.when(end > prev)
            def _():
                zero_block(end - tm)
            return c

        lax.fori_loop(0, N_EXPERTS, fill, 0)

        def fill_unused(b, c):
            zero_block(b * tm)
            return c

        lax.fori_loop(nu_ref[0], xs_hbm.shape[0] // tm, fill_unused, 0)

    cp = pltpu.make_async_copy(dest_ref.at[0, 0], idx_smem, isem)
    cp.start()
    cp.wait()

    def issue(r, c):
        for k in range(TOP_K):
            d = idx_smem[TOP_K * r + k]
            pltpu.make_async_copy(h_ref.at[pl.ds(r, 1), :], xs_hbm.at[pl.ds(d, 1), :], sem).start(priority=k)
        return c

    lax.fori_loop(0, tt, issue, 0, unroll=8)
    all_rows = xs_hbm.at[pl.ds(0, TOP_K * tt), :]
    pltpu.make_async_copy(all_rows, all_rows, sem).wait()


def dispatch_rows(pend, n_used, dest, h_packed, rows, tt, tm):
    t, w = h_packed.shape
    return pl.pallas_call(
        functools.partial(_dispatch_body, tt=tt, tm=tm),
        grid_spec=pltpu.PrefetchScalarGridSpec(
            num_scalar_prefetch=2,
            grid=(t // tt,),
            in_specs=[
                pl.BlockSpec((1, 1, TOP_K * tt), lambda i, pe, nu: (i, 0, 0)),
                pl.BlockSpec((tt, w), lambda i, pe, nu: (i, 0)),
            ],
            out_specs=pl.BlockSpec(memory_space=pl.ANY),
            scratch_shapes=[
                pltpu.SMEM((TOP_K * tt,), I32),
                pltpu.VMEM((tm, w), U32),
                pltpu.SemaphoreType.DMA,
                pltpu.SemaphoreType.DMA,
            ],
        ),
        out_shape=jax.ShapeDtypeStruct((rows, w), U32),
        compiler_params=_params("arbitrary"),
        name="dispatch_rows",
    )(pend, n_used, dest.reshape(t // tt, 1, TOP_K * tt), h_packed)


def _expert_changed(be_ref, i):
    return (i == 0) | (be_ref[i] != be_ref[jnp.maximum(i - 1, 0)])


CAST_ROWS = 256


def _cast_rows(src_ref, dst_ref):
    def body(r, carry):
        rows = pl.ds(pl.multiple_of(r * CAST_ROWS, CAST_ROWS), CAST_ROWS)
        dst_ref[rows, :] = src_ref[rows, :].astype(dst_ref.dtype)
        return carry
    lax.fori_loop(0, src_ref.shape[0] // CAST_ROWS, body, 0)


def _stream_weights(be_ref, nx_ref, i, sweep, n_sweeps, copies, on_ready, cnt_ref):
    @pl.when((sweep == 0) & (i == 0))
    def _():
        cnt_ref[0] = 0
        for cp in copies(be_ref[0], 0, 0):
            cp.start()

    @pl.when(_expert_changed(be_ref, i))
    def _():
        slot = cnt_ref[0] & 1
        for cp in copies(be_ref[i], sweep, slot):
            cp.wait()
        on_ready(slot)
        nxt = nx_ref[i]

        @pl.when(nxt >= 0)
        def _():
            for cp in copies(nxt, sweep, 1 - slot):
                cp.start()

        @pl.when((nxt < 0) & (sweep + 1 < n_sweeps))
        def _():
            for cp in copies(be_ref[0], sweep + 1, 1 - slot):
                cp.start()

        cnt_ref[0] = cnt_ref[0] + 1


def _gate_up_body(be_ref, nx_ref, nu_ref, x_ref, wg_hbm, wu_hbm, o_ref, wbuf_ref, wgb_ref, wub_ref, sem_ref, cnt_ref,
                  *, tf):
    j = pl.program_id(0)
    i = pl.program_id(1)
    half = x_ref.shape[1]

    def copies(e, jj, slot):
        cols = pl.ds(pl.multiple_of(jj * tf, tf), tf)
        return (pltpu.make_async_copy(wg_hbm.at[e, :, cols], wbuf_ref.at[slot, 0], sem_ref.at[slot]),
                pltpu.make_async_copy(wu_hbm.at[e, :, cols], wbuf_ref.at[slot, 1], sem_ref.at[slot]))

    def on_ready(slot):
        _cast_rows(wbuf_ref.at[slot, 0], wgb_ref)
        _cast_rows(wbuf_ref.at[slot, 1], wub_ref)

    _stream_weights(be_ref, nx_ref, i, j, pl.num_programs(0), copies, on_ready, cnt_ref)

    @pl.when(i < nu_ref[0])
    def _():
        x_lo, x_hi = _unpack_pairs(x_ref[...])
        x_lo = x_lo.astype(BF16)
        x_hi = x_hi.astype(BF16)

        def mm(w_ref):
            return (jnp.dot(x_lo, w_ref[:half, :], preferred_element_type=F32)
                    + jnp.dot(x_hi, w_ref[half:, :], preferred_element_type=F32))

        a = mm(wgb_ref)
        b = mm(wub_ref)
        o_ref[...] = (a * _sigmoid(a) * b).astype(o_ref.dtype)

    @pl.when(i >= nu_ref[0])
    def _():
        o_ref[...] = jnp.zeros(o_ref.shape, o_ref.dtype)


def expert_gate_up(blk_e, blk_next, n_used, xs, w_gate, w_up, tm, tf):
    rows = xs.shape[0]
    d, f = w_gate.shape[1], w_gate.shape[2]
    nblk = rows // tm
    last = lambda i, nu: jnp.minimum(i, nu[0] - 1)
    hbm = pl.BlockSpec(memory_space=pl.ANY)
    return pl.pallas_call(
        functools.partial(_gate_up_body, tf=tf),
        grid_spec=pltpu.PrefetchScalarGridSpec(
            num_scalar_prefetch=3,
            grid=(f // tf, nblk),
            in_specs=[pl.BlockSpec((tm, d // 2), lambda j, i, be, nx, nu: (last(i, nu), 0)), hbm, hbm],
            out_specs=pl.BlockSpec((tm, tf), lambda j, i, be, nx, nu: (i, j)),
            scratch_shapes=[
                pltpu.VMEM((2, 2, d, tf), F32),
                pltpu.VMEM((d, tf), BF16),
                pltpu.VMEM((d, tf), BF16),
                pltpu.SemaphoreType.DMA((2,)),
                pltpu.SMEM((1,), I32),
            ],
        ),
        out_shape=jax.ShapeDtypeStruct((rows, f), BF16),
        compiler_params=_params("arbitrary", "arbitrary"),
        name="expert_gate_up",
    )(blk_e, blk_next, n_used, xs, w_gate, w_up)


def _down_body(be_ref, nx_ref, nu_ref, a_ref, wd_hbm, o_ref, wbuf_ref, wdb_ref, sem_ref, cnt_ref):
    i = pl.program_id(0)
    half = o_ref.shape[1]

    def copies(e, sweep, slot):
        return (pltpu.make_async_copy(wd_hbm.at[e], wbuf_ref.at[slot], sem_ref.at[slot]),)

    def on_ready(slot):
        _cast_rows(wbuf_ref.at[slot], wdb_ref)

    _stream_weights(be_ref, nx_ref, i, 0, 1, copies, on_ready, cnt_ref)

    @pl.when(i < nu_ref[0])
    def _():
        y = jnp.dot(a_ref[...], wdb_ref[...], preferred_element_type=F32)
        o_ref[...] = _pack_pairs(y[:, :half], y[:, half:])

    @pl.when(i >= nu_ref[0])
    def _():
        o_ref[...] = jnp.zeros(o_ref.shape, o_ref.dtype)


def expert_down(blk_e, blk_next, n_used, act, w_down, tm):
    rows, f = act.shape
    d = w_down.shape[2]
    nblk = rows // tm
    last = lambda i, nu: jnp.minimum(i, nu[0] - 1)
    return pl.pallas_call(
        _down_body,
        grid_spec=pltpu.PrefetchScalarGridSpec(
            num_scalar_prefetch=3,
            grid=(nblk,),
            in_specs=[
                pl.BlockSpec((tm, f), lambda i, be, nx, nu: (last(i, nu), 0)),
                pl.BlockSpec(memory_space=pl.ANY),
            ],
            out_specs=pl.BlockSpec((tm, d // 2), lambda i, be, nx, nu: (i, 0)),
            scratch_shapes=[
                pltpu.VMEM((2, f, d), F32),
                pltpu.VMEM((f, d), BF16),
                pltpu.SemaphoreType.DMA((2,)),
                pltpu.SMEM((1,), I32),
            ],
        ),
        out_shape=jax.ShapeDtypeStruct((rows, d // 2), U32),
        compiler_params=_params("arbitrary"),
        name="expert_down",
    )(blk_e, blk_next, n_used, act, w_down)


def _combine_body(dest_ref, x_ref, g_ref, w_ref, y_hbm, o_ref, idx_smem, buf_ref, sem, isem, *, tt):
    d = x_ref.shape[1]
    half = d // 2
    cp = pltpu.make_async_copy(dest_ref.at[0, 0], idx_smem, isem)
    cp.start()
    cp.wait()

    def issue(r, c):
        for k in range(TOP_K):
            src = idx_smem[TOP_K * r + k]
            pltpu.make_async_copy(y_hbm.at[pl.ds(src, 1), :], buf_ref.at[k, pl.ds(r, 1), :], sem).start(priority=k)
        return c

    lax.fori_loop(0, tt, issue, 0, unroll=8)
    pltpu.make_async_copy(buf_ref, buf_ref, sem).wait()

    y0_lo, y0_hi = _unpack_pairs(buf_ref[0])
    y1_lo, y1_hi = _unpack_pairs(buf_ref[1])
    g0 = g_ref[:, 0:1]
    g1 = g_ref[:, 1:2]
    x_lo = x_ref[:, :half] + (g0 * y0_lo + g1 * y1_lo)
    x_hi = x_ref[:, half:] + (g0 * y0_hi + g1 * y1_hi)
    ms = (jnp.sum(x_lo * x_lo, axis=-1, keepdims=True) + jnp.sum(x_hi * x_hi, axis=-1, keepdims=True)) / d
    inv = lax.rsqrt(ms + EPS)
    o_ref[:, :half] = x_lo * inv * w_ref[:, :half]
    o_ref[:, half:] = x_hi * inv * w_ref[:, half:]


def combine_and_norm(x, y_packed, dest, gate, w, tt):
    t, d = x.shape
    row = lambda wd: pl.BlockSpec((tt, wd), lambda i: (i, 0))
    return pl.pallas_call(
        functools.partial(_combine_body, tt=tt),
        grid=(t // tt,),
        in_specs=[
            pl.BlockSpec((1, 1, TOP_K * tt), lambda i: (i, 0, 0)),
            row(d), row(LANES), pl.BlockSpec((1, d), lambda i: (0, 0)),
            pl.BlockSpec(memory_space=pl.ANY),
        ],
        out_specs=row(d),
        out_shape=jax.ShapeDtypeStruct((t, d), F32),
        scratch_shapes=[
            pltpu.SMEM((TOP_K * tt,), I32),
            pltpu.VMEM((TOP_K, tt, d // 2), U32),
            pltpu.SemaphoreType.DMA,
            pltpu.SemaphoreType.DMA,
        ],
        compiler_params=_params("arbitrary"),
        name="combine_norm",
    )(dest.reshape(t // tt, 1, TOP_K * tt), x, gate, w.reshape(1, d).astype(F32), y_packed)


def _tiles(batch, seq):
    t = batch * seq
    return dict(
        norm_tm=min(256, t),
        mm_tm=min(1024, t),
        mm_tn=1024,
        out_tn=512,
        attn_tq=min(512, seq),
        attn_tk=min(512, seq),
        moe_tm=min(256, t),
        moe_tf=512,
        disp_tt=min(512, t),
        comb_tt=min(256, t),
    )


def _lambda_init_at(layer):
    return 0.8 - 0.6 * math.exp(-0.3 * layer)


def kernel(x, rel_bias, norm1_w, w_in, lambda_q1, lambda_k1, lambda_q2, lambda_k2, subln_w, conv_w, conv_b,
           dt_bias_f, dt_bias_b, a_log_f, a_log_b, d_skip, ssm_norm_w, w_out, norm2_w, w_group_router,
           b_group_router, w_expert_router, b_expert_router, w_gate, w_up, w_down, final_norm_w):
    batch, seq, d = x.shape
    t = batch * seq
    depth = norm1_w.shape[0]
    attn_w = d // 2
    ssm_w = d - attn_w
    dv = 2 * ATTN_HEAD_DIM
    a_heads = attn_w // dv
    s_heads = ssm_w // SSM_HEAD_DIM
    cc = ssm_w + 2 * SSM_GROUPS * SSM_STATE
    main_w = 3 * attn_w + ssm_w + cc
    assert ssm_w == attn_w and cc == 2 * ssm_w and 2 * s_heads <= LANES
    tl = _tiles(batch, seq)
    log2e = math.log2(math.e)
    band = _band_table(rel_bias, tl["attn_tq"], tl["attn_tk"], log2e)
    q_scale = jnp.where(jnp.arange(main_w) < attn_w, ATTN_HEAD_DIM ** -0.5 * log2e, 1.0).astype(F32)

    def pad_lanes(v):
        return jnp.pad(v.astype(F32), (0, LANES - v.shape[0])).reshape(1, LANES)

    xf = x.reshape(t, d)
    for layer in range(depth):
        lam_init = _lambda_init_at(layer)
        wl = w_in[layer]
        w_main = (wl[:, :main_w] * q_scale).astype(BF16)
        w_dt = jnp.pad(wl[:, main_w:], ((0, 0), (0, LANES - 2 * s_heads))).astype(BF16)
        h = rmsnorm_rows(xf, norm1_w[layer], BF16, tl["norm_tm"])
        proj = matmul(h, w_main, BF16, tl["mm_tm"], tl["mm_tn"], "in_proj")
        dt = matmul(h, w_dt, F32, tl["mm_tm"], LANES, "dt_proj")

        vec = lambda v: v[layer].reshape(1, -1).astype(F32)
        attn = diff_attention(proj, band, vec(lambda_q1), vec(lambda_k1), vec(lambda_q2), vec(lambda_k2),
                              vec(subln_w), batch=batch, seq=seq, heads=a_heads, lam_init=lam_init,
                              tq=tl["attn_tq"], tk=tl["attn_tk"])

        dt_bias = pad_lanes(jnp.concatenate([dt_bias_f[layer], dt_bias_b[layer]]))
        a_log = pad_lanes(jnp.concatenate([a_log_f[layer], a_log_b[layer]]))
        cw = conv_w[layer].astype(F32)
        cb = conv_b[layer].reshape(1, cc).astype(F32)
        common = dict(batch=batch, seq=seq, heads=s_heads, z_blk=3 * attn_w // ssm_w, xbc_blk=(3 * attn_w + ssm_w) // cc)
        y_bwd, act = ssd_pass(proj, dt, cw, cb, dt_bias, a_log, None, reverse=True, final=False, col_off=s_heads,
                              **common)
        extra = dict(d_skip=jnp.repeat(d_skip[layer].astype(F32), SSM_HEAD_DIM).reshape(1, ssm_w),
                     norm_w=vec(ssm_norm_w), y_other=y_bwd, act=act)
        ssm = ssd_pass(proj, dt, cw, cb, dt_bias, a_log, extra, reverse=False, final=True, col_off=0, **common)

        x1 = out_projection(attn, ssm, w_out[layer].astype(BF16), xf, tl["mm_tm"], tl["out_tn"])

        w_router = jnp.pad(jnp.concatenate([w_group_router[layer], w_expert_router[layer]], axis=1).astype(F32),
                           ((0, 0), (0, LANES - N_EXPERT_GROUPS - N_EXPERTS)))
        b_router = pad_lanes(jnp.concatenate([b_group_router[layer], b_expert_router[layer]]))
        h2, idx, gate, counts = norm_and_route(x1, norm2_w[layer], w_router, b_router, tl["norm_tm"])
        tm = tl["moe_tm"]
        dest, pend, blk_e, blk_next, n_used, rows = _moe_plan(idx, counts, tm)
        xs = dispatch_rows(pend, n_used, dest, h2, rows, tl["disp_tt"], tm)
        act = expert_gate_up(blk_e, blk_next, n_used, xs, w_gate[layer], w_up[layer], tm, tl["moe_tf"])
        y = expert_down(blk_e, blk_next, n_used, act, w_down[layer], tm)
        if layer + 1 < depth:
            raise NotImplementedError("multi-layer stacking needs an un-normalised combine")
        xf = combine_and_norm(x1, y, dest, gate, final_norm_w, tl["comb_tt"])
    return xf.reshape(batch, seq, d)
```

```python
import functools
import math

import jax
import jax.numpy as jnp
from jax import lax
from jax.experimental import pallas as pl
from jax.experimental.pallas import tpu as pltpu

F32 = jnp.float32
BF16 = jnp.bfloat16
U32 = jnp.uint32
I32 = jnp.int32
EPS = 1e-6

ATTN_HEAD_DIM = 128
NUM_BUCKETS = 32
MAX_DISTANCE = 128
SSM_HEAD_DIM = 64
SSM_GROUPS = 8
SSM_STATE = 128
CONV_WIDTH = 5
CHUNK = 128
N_EXPERT_GROUPS = 4
EXPERTS_PER_GROUP = 8
N_EXPERTS = N_EXPERT_GROUPS * EXPERTS_PER_GROUP
TOP_K = 2

LANES = 128
SUBLANES = 8
BF16_ROWS = 16
VMEM_LIMIT = 56 * 1024 * 1024
HI16 = 0xFFFF0000


def _params(*sem):
    return pltpu.CompilerParams(dimension_semantics=sem, vmem_limit_bytes=VMEM_LIMIT)


def _sigmoid(x):
    return 1.0 / (1.0 + jnp.exp(-x))


def _pack_pairs(lo, hi):
    lo_b = lax.bitcast_convert_type(lo.astype(BF16).astype(F32), U32)
    hi_b = lax.bitcast_convert_type(hi.astype(BF16).astype(F32), U32)
    return (lo_b >> 16) | (hi_b & U32(HI16))


def _unpack_pairs(w):
    return lax.bitcast_convert_type(w << 16, F32), lax.bitcast_convert_type(w & U32(HI16), F32)


def _rmsnorm_body(x_ref, w_ref, o_ref):
    x = x_ref[...]
    ms = jnp.mean(x * x, axis=-1, keepdims=True)
    o_ref[...] = (x * lax.rsqrt(ms + EPS) * w_ref[...]).astype(o_ref.dtype)


def rmsnorm_rows(x, w, out_dtype, tm):
    t, d = x.shape
    return pl.pallas_call(
        _rmsnorm_body,
        grid=(t // tm,),
        in_specs=[pl.BlockSpec((tm, d), lambda i: (i, 0)), pl.BlockSpec((1, d), lambda i: (0, 0))],
        out_specs=pl.BlockSpec((tm, d), lambda i: (i, 0)),
        out_shape=jax.ShapeDtypeStruct((t, d), out_dtype),
        compiler_params=_params("parallel"),
        name="rmsnorm",
    )(x, w.reshape(1, d).astype(F32))


NT_DIMS = (((1,), (1,)), ((), ()))


def _matmul_nt_body(a_ref, b_ref, o_ref):
    o_ref[...] = lax.dot_general(a_ref[...], b_ref[...], NT_DIMS, preferred_element_type=F32).astype(o_ref.dtype)


def matmul_nt(a, b, out_dtype, m, n, tm, tn, name, a_blk0=0, b_blk0=0):
    k = a.shape[1]
    assert b.shape[1] == k and m % tm == 0 and n % tn == 0
    return pl.pallas_call(
        _matmul_nt_body,
        grid=(m // tm, n // tn),
        in_specs=[pl.BlockSpec((tm, k), lambda i, j: (a_blk0 + i, 0)),
                  pl.BlockSpec((tn, k), lambda i, j: (b_blk0 + j, 0))],
        out_specs=pl.BlockSpec((tm, tn), lambda i, j: (i, j)),
        out_shape=jax.ShapeDtypeStruct((m, n), out_dtype),
        compiler_params=_params("parallel", "parallel"),
        name=name,
    )(a, b)


def _outproj_body(a_ref, s_ref, wa_ref, ws_ref, r_ref, o_ref):
    acc = jnp.dot(a_ref[...], wa_ref[...], preferred_element_type=F32)
    acc = acc + jnp.dot(s_ref[...], ws_ref[...], preferred_element_type=F32)
    o_ref[...] = r_ref[...] + acc


def out_projection(attn, ssm, w, resid, tm, tn):
    m, ka = attn.shape
    ks = ssm.shape[1]
    assert ka == ks
    n = w.shape[1]
    return pl.pallas_call(
        _outproj_body,
        grid=(m // tm, n // tn),
        in_specs=[
            pl.BlockSpec((tm, ka), lambda i, j: (i, 0)),
            pl.BlockSpec((tm, ks), lambda i, j: (i, 0)),
            pl.BlockSpec((ka, tn), lambda i, j: (0, j)),
            pl.BlockSpec((ks, tn), lambda i, j: (1, j)),
            pl.BlockSpec((tm, tn), lambda i, j: (i, j)),
        ],
        out_specs=pl.BlockSpec((tm, tn), lambda i, j: (i, j)),
        out_shape=jax.ShapeDtypeStruct((m, n), F32),
        compiler_params=_params("parallel", "parallel"),
        name="out_proj",
    )(attn, ssm, w, w, resid)


def _t5_bucket(rel):
    half = NUM_BUCKETS // 2
    max_exact = half // 2
    n = jnp.abs(rel)
    large = max_exact + (
        jnp.log(jnp.maximum(n, 1).astype(F32) / max_exact) / math.log(MAX_DISTANCE / max_exact) * (half - max_exact)
    ).astype(jnp.int32)
    large = jnp.minimum(large, half - 1)
    return jnp.where(rel > 0, half, 0) + jnp.where(n < max_exact, n, large)


def _band_body(v_ref, o_ref, *, tq, w):
    npad = v_ref.shape[-1]
    x = jnp.broadcast_to(v_ref[0], (tq, npad))
    y = pltpu.roll(x, npad - (tq - 1), 1, stride=1, stride_axis=0)
    o_ref[0] = y[:, :w].T


def _band_table(rel_bias, tq, tk, scale):
    assert tk + 1 >= MAX_DISTANCE
    heads = rel_bias.shape[1]
    w = tq + 4 * tk
    n = w + tq - 1
    npad = -(-n // LANES) * LANES
    rel = jnp.arange(npad, dtype=jnp.int32) - (2 * tk + tq - 1)
    v = (rel_bias[_t5_bucket(rel)].astype(F32) * scale).T.reshape(heads, 1, npad)
    return pl.pallas_call(
        functools.partial(_band_body, tq=tq, w=w),
        grid=(heads,),
        in_specs=[pl.BlockSpec((1, 1, npad), lambda h: (h, 0, 0))],
        out_specs=pl.BlockSpec((1, w, tq), lambda h: (h, 0, 0)),
        out_shape=jax.ShapeDtypeStruct((heads, w, tq), F32),
        compiler_params=_params("parallel"),
        name="band_table",
    )(v)


ATTN_STRIP = 32


def _attn_body(q_ref, k_ref, vt_ref, band_ref, eye_ref, lq1_ref, lk1_ref, lq2_ref, lk2_ref, subw_ref, o_ref,
               m_ref, l_ref, acc_ref, sa_ref, sb_ref, p_ref, *, tq, tk, nk, lam_init):
    dh = ATTN_HEAD_DIM
    rs = ATTN_STRIP
    qi = pl.program_id(2)

    m_ref[...] = jnp.full(m_ref.shape, -jnp.inf, F32)
    l_ref[...] = jnp.zeros(l_ref.shape, F32)
    acc_ref[...] = jnp.zeros(acc_ref.shape, F32)

    def scores(kc, s_ref):
        k0 = pl.multiple_of(kc * tk, tk)
        start = pl.multiple_of(jnp.clip(kc * tk - qi * tq + 2 * tk, 0, tq + 3 * tk), LANES)
        for mi in range(2):
            kk = k_ref[pl.ds(k0, tk), mi * dh:(mi + 1) * dh]
            q = q_ref[:, mi * dh:(mi + 1) * dh]
            s_ref[mi] = (lax.dot_general(kk, q, (((1,), (1,)), ((), ())), preferred_element_type=F32)
                         + band_ref[0, pl.ds(start, tk), :])

    def softmax_pv(kc, s_ref):
        k0 = pl.multiple_of(kc * tk, tk)
        for mi in range(2):
            def strip_max(i, mx):
                return jnp.maximum(mx, s_ref[mi, pl.ds(i * rs, rs), :])

            mx = lax.fori_loop(0, tk // rs, strip_max, jnp.full((rs, tq), -jnp.inf, F32), unroll=True)
            m_prev = m_ref[mi]
            m_new = jnp.maximum(m_prev, jnp.max(mx, axis=0, keepdims=True))
            alpha = jnp.exp2(m_prev - m_new)

            def strip_exp(i, ls):
                p = jnp.exp2(s_ref[mi, pl.ds(i * rs, rs), :] - m_new)
                p_ref[mi, pl.ds(i * rs, rs), :] = p.astype(BF16)
                return ls + p

            ls = lax.fori_loop(0, tk // rs, strip_exp, jnp.zeros((rs, tq), F32), unroll=True)
            l_ref[mi] = alpha * l_ref[mi] + jnp.sum(ls, axis=0, keepdims=True)
            m_ref[mi] = m_new
            pv = jnp.dot(vt_ref[:, pl.ds(k0, tk)], p_ref[mi], preferred_element_type=F32)
            acc_ref[mi] = acc_ref[mi] * alpha + pv

    scores(0, sa_ref)
    if nk > 1:
        assert nk % 2 == 0

        def pair(j, carry):
            scores(2 * j + 1, sb_ref)
            softmax_pv(2 * j, sa_ref)
            scores(2 * j + 2, sa_ref)
            softmax_pv(2 * j + 1, sb_ref)
            return carry

        lax.fori_loop(0, nk // 2 - 1, pair, 0)
        scores(nk - 1, sb_ref)
        softmax_pv(nk - 2, sa_ref)
        softmax_pv(nk - 1, sb_ref)
    else:
        softmax_pv(0, sa_ref)

    lam = (jnp.exp(jnp.sum(lq1_ref[...] * lk1_ref[...], axis=-1, keepdims=True))
           - jnp.exp(jnp.sum(lq2_ref[...] * lk2_ref[...], axis=-1, keepdims=True)) + lam_init)
    o = acc_ref[0] / l_ref[0] - lam * (acc_ref[1] / l_ref[1])
    ms = jnp.mean(o * o, axis=0, keepdims=True)
    o = (o * lax.rsqrt(ms + EPS) * subw_ref[...] * (1.0 - lam_init)).astype(BF16)
    o_ref[...] = lax.dot_general(eye_ref[...], o, NT_DIMS, preferred_element_type=F32).astype(o_ref.dtype)


def diff_attention(qk, vt, band, lq1, lk1, lq2, lk2, subw, *, batch, seq, heads, lam_init, tq, tk):
    dv = 2 * ATTN_HEAD_DIM
    nq, nk = seq // tq, seq // tk
    assert tk % ATTN_STRIP == 0
    vec = pl.BlockSpec((1, ATTN_HEAD_DIM), lambda b, h, i: (0, 0))
    body = functools.partial(_attn_body, tq=tq, tk=tk, nk=nk, lam_init=lam_init)
    eye = jnp.eye(tq, dtype=BF16)
    return pl.pallas_call(
        body,
        grid=(batch, heads, nq),
        in_specs=[
            pl.BlockSpec((tq, dv), lambda b, h, i: (b * nq + i, h)),
            pl.BlockSpec((seq, dv), lambda b, h, i: (b, heads + h)),
            pl.BlockSpec((dv, seq), lambda b, h, i: (h, b)),
            pl.BlockSpec((1, tq + 4 * tk, tq), lambda b, h, i: (h, 0, 0)),
            pl.BlockSpec((tq, tq), lambda b, h, i: (0, 0)),
            vec, vec, vec, vec,
            pl.BlockSpec((dv, 1), lambda b, h, i: (0, 0)),
        ],
        out_specs=pl.BlockSpec((tq, dv), lambda b, h, i: (b * nq + i, h)),
        out_shape=jax.ShapeDtypeStruct((batch * seq, heads * dv), BF16),
        scratch_shapes=[
            pltpu.VMEM((2, 1, tq), F32),
            pltpu.VMEM((2, 1, tq), F32),
            pltpu.VMEM((2, dv, tq), F32),
            pltpu.VMEM((2, tk, tq), F32),
            pltpu.VMEM((2, tk, tq), F32),
            pltpu.VMEM((2, tk, tq), BF16),
        ],
        compiler_params=_params("parallel", "parallel", "parallel"),
        name="diff_attention",
    )(qk, qk, vt, band, eye, lq1, lk1, lq2, lk2, subw)


def _ssd_body(*refs, reverse, final, heads, col_off):
    if final:
        (z_ref, act_in_ref, dt_ref, dtt_ref, dtb_ref, alog_ref, dtbt_ref, alogt_ref, dskip_ref, nw_ref, yb_ref, o_ref,
         state_ref) = refs
    else:
        (xbc_ref, hp_ref, hn_ref, dt_ref, dtt_ref, cw_ref, cb_ref, dtb_ref, alog_ref, dtbt_ref, alogt_ref, o_ref,
         act_out_ref, u_ref, act_ref, state_ref) = refs
    L = CHUNK
    assert SSM_STATE == CHUNK
    G, N, P = SSM_GROUPS, SSM_STATE, SSM_HEAD_DIM
    R = heads // G
    GW = R * P
    W = heads * P
    CC = W + 2 * G * N
    HALO = SUBLANES
    c = pl.program_id(1)
    nc = pl.num_programs(1)
    pos = (nc - 1 - c) if reverse else c

    @pl.when(c == 0)
    def _():
        state_ref[...] = jnp.zeros(state_ref.shape, F32)

    if final:
        def chan(lo, hi):
            return act_in_ref[:, lo:hi].astype(F32)
    else:
        prev = hp_ref[...].astype(F32)[BF16_ROWS - HALO:, :]
        nxt = hn_ref[...].astype(F32)[:HALO, :]
        u_ref[0:HALO, :] = jnp.where(pos > 0, prev, 0.0)
        u_ref[HALO:HALO + L, :] = xbc_ref[...].astype(F32)
        u_ref[HALO + L:, :] = jnp.where(pos < nc - 1, nxt, 0.0)
        SLAB = 2 * LANES

        def conv_slab(j, carry):
            lo = pl.multiple_of(j * SLAB, SLAB)
            acc = jnp.broadcast_to(cb_ref[:, pl.ds(lo, SLAB)], (L, SLAB))
            for t in range(CONV_WIDTH):
                r0 = HALO - CONV_WIDTH // 2 + t
                acc = acc + cw_ref[t:t + 1, pl.ds(lo, SLAB)] * u_ref[r0:r0 + L, pl.ds(lo, SLAB)]
            act = acc * _sigmoid(acc)
            act_ref[:, pl.ds(lo, SLAB)] = act
            act_out_ref[:, pl.ds(lo, SLAB)] = act.astype(act_out_ref.dtype)
            return carry

        lax.fori_loop(0, CC // SLAB, conv_slab, 0)

        def chan(lo, hi):
            return act_ref[:, lo:hi]

    def softplus(v):
        return jnp.maximum(v, 0.0) + jnp.log1p(jnp.exp(-jnp.abs(v)))

    dtv = softplus(dt_ref[...] + dtb_ref[...])
    a = -jnp.exp(alog_ref[...]) * dtv
    a_t = -jnp.exp(alogt_ref[...]) * softplus(dtt_ref[...] + dtbt_ref[...])
    row = lax.broadcasted_iota(jnp.int32, (L, L), 0)
    col = lax.broadcasted_iota(jnp.int32, (L, L), 1)
    keep = (row <= col) if reverse else (row >= col)
    keep_t = (row >= col) if reverse else (row <= col)
    cum = jnp.dot(keep.astype(F32), a, precision=lax.Precision.HIGHEST, preferred_element_type=F32)
    cum_t = jnp.dot(a_t, keep_t.astype(F32), precision=lax.Precision.HIGHEST, preferred_element_type=F32)
    total = cum[0:1, :] if reverse else cum[L - 1:L, :]
    eye_bf = jnp.where(row == col, 1.0, 0.0).astype(BF16)
    seg = lax.broadcasted_iota(jnp.int32, (L, GW), 1) // P
    seg1 = lax.broadcasted_iota(jnp.int32, (1, GW), 1) // P

    for g in range(G):
        xs = chan(g * GW, (g + 1) * GW)
        bmat = chan(W + g * N, W + (g + 1) * N)
        cmat = chan(W + (G + g) * N, W + (G + g + 1) * N)
        b_bf = bmat.astype(BF16)
        c_bf = cmat.astype(BF16)
        cb = lax.dot_general(c_bf, b_bf, NT_DIMS, preferred_element_type=F32)
        bt_bf = lax.dot_general(eye_bf, b_bf, NT_DIMS, preferred_element_type=F32).astype(BF16)
        dt_e = jnp.zeros((L, GW), F32)
        cum_e = jnp.zeros((L, GW), F32)
        tot_e = jnp.zeros((1, GW), F32)
        for r in range(R):
            hc = col_off + g * R + r
            dt_e = jnp.where(seg == r, dtv[:, hc:hc + 1], dt_e)
            cum_e = jnp.where(seg == r, cum[:, hc:hc + 1], cum_e)
            tot_e = jnp.where(seg1 == r, total[:, hc:hc + 1], tot_e)
        x_dt = xs * dt_e
        y = jnp.zeros((L, GW), F32)
        for r in range(R):
            hc = col_off + g * R + r
            decay = jnp.exp(jnp.where(keep, cum[:, hc:hc + 1] - cum_t[hc:hc + 1, :], -jnp.inf))
            m_h = (cb * decay).astype(BF16)
            x_h = jnp.where(seg == r, x_dt, 0.0).astype(BF16)
            y = y + jnp.dot(m_h, x_h, preferred_element_type=F32)
        s_prev = state_ref[g]
        y = y + jnp.dot(c_bf, s_prev.astype(BF16), preferred_element_type=F32) * jnp.exp(cum_e)
        x_end = (x_dt * jnp.exp(tot_e - cum_e)).astype(BF16)
        state_ref[g] = s_prev * jnp.exp(tot_e) + jnp.dot(bt_bf, x_end, preferred_element_type=F32)
        if final:
            y = y + yb_ref[:, g * GW:(g + 1) * GW] + dskip_ref[:, g * GW:(g + 1) * GW] * xs
            zg = z_ref[:, g * GW:(g + 1) * GW].astype(F32)
            y = y * (zg * _sigmoid(zg))
            ms = jnp.mean(y * y, axis=-1, keepdims=True)
            y = y * lax.rsqrt(ms + EPS) * nw_ref[:, g * GW:(g + 1) * GW]
        o_ref[:, g * GW:(g + 1) * GW] = y.astype(o_ref.dtype)


def ssd_pass(xbc, dt, dt_t, conv_w, conv_b, dt_bias, a_log, extra, *, batch, seq, heads, reverse, final, col_off):
    L = CHUNK
    W = heads * SSM_HEAD_DIM
    CC = W + 2 * SSM_GROUPS * SSM_STATE
    GW = W // SSM_GROUPS
    nc = seq // L
    nhalo = batch * seq // BF16_ROWS
    per = L // BF16_ROWS

    def rb(b, c):
        return b * nc + ((nc - 1 - c) if reverse else c)

    full = lambda shape: pl.BlockSpec(shape, lambda b, c: (0,) * len(shape))
    rows = lambda w: pl.BlockSpec((L, w), lambda b, c: (rb(b, c), 0))
    cols = pl.BlockSpec((LANES, L), lambda b, c: (0, rb(b, c)))
    state = pltpu.VMEM((SSM_GROUPS, SSM_STATE, GW), F32)
    body = functools.partial(_ssd_body, reverse=reverse, final=final, heads=heads, col_off=col_off)
    t = batch * seq
    head_specs = [full((1, LANES)), full((1, LANES)), full((LANES, 1)), full((LANES, 1))]
    head_args = [dt_bias, a_log, dt_bias.reshape(LANES, 1), a_log.reshape(LANES, 1)]
    if final:
        in_specs = [rows(W), rows(CC), rows(LANES), cols] + head_specs + [full((1, W)), full((1, W)), rows(W)]
        args = [extra["z"], extra["act"], dt, dt_t] + head_args + [extra["d_skip"], extra["norm_w"], extra["y_other"]]
        out_specs = rows(W)
        out_shape = jax.ShapeDtypeStruct((t, W), BF16)
        scratch = [state]
    else:
        in_specs = [
            rows(CC),
            pl.BlockSpec((BF16_ROWS, CC), lambda b, c: (jnp.maximum(rb(b, c) * per - 1, 0), 0)),
            pl.BlockSpec((BF16_ROWS, CC), lambda b, c: (jnp.minimum((rb(b, c) + 1) * per, nhalo - 1), 0)),
            rows(LANES), cols,
            full((CONV_WIDTH, CC)), full((1, CC)),
        ] + head_specs
        args = [xbc, xbc, xbc, dt, dt_t, conv_w, conv_b] + head_args
        out_specs = [rows(W), rows(CC)]
        out_shape = [jax.ShapeDtypeStruct((t, W), F32), jax.ShapeDtypeStruct((t, CC), BF16)]
        scratch = [pltpu.VMEM((L + 2 * SUBLANES, CC), F32), pltpu.VMEM((L, CC), F32), state]
    return pl.pallas_call(
        body,
        grid=(batch, nc),
        in_specs=in_specs,
        out_specs=out_specs,
        out_shape=out_shape,
        scratch_shapes=scratch,
        compiler_params=_params("parallel", "arbitrary"),
        name="ssd_final" if final else "ssd_scan",
    )(*args)


def _router_body(x_ref, nw_ref, wr_ref, br_ref, h_ref, idx_ref, gate_ref, cnt_ref, run_ref):
    tm, d = x_ref.shape
    half = d // 2

    @pl.when(pl.program_id(0) == 0)
    def _():
        run_ref[...] = jnp.zeros(run_ref.shape, F32)

    x = x_ref[...]
    ms = jnp.mean(x * x, axis=-1, keepdims=True)
    h = x * lax.rsqrt(ms + EPS) * nw_ref[...]
    h_ref[...] = _pack_pairs(h[:, :half], h[:, half:])
    logits = jnp.dot(h, wr_ref[...], precision=lax.Precision.HIGHEST, preferred_element_type=F32) + br_ref[...]
    lane = lax.broadcasted_iota(jnp.int32, logits.shape, 1).astype(F32)
    ninf = -jnp.inf
    is_g = lane < N_EXPERT_GROUPS
    gl = jnp.where(is_g, logits, ninf)
    gmax = jnp.max(gl, axis=-1, keepdims=True)
    gsum = jnp.sum(jnp.where(is_g, jnp.exp(gl - gmax), 0.0), axis=-1, keepdims=True)
    p_group = 1.0 / gsum
    g_sel = jnp.min(jnp.where(gl == gmax, lane, float(LANES)), axis=-1, keepdims=True)
    lo = N_EXPERT_GROUPS + EXPERTS_PER_GROUP * g_sel
    el = jnp.where((lane >= lo) & (lane < lo + EXPERTS_PER_GROUP), logits, ninf)
    m1 = jnp.max(el, axis=-1, keepdims=True)
    i1 = jnp.min(jnp.where(el == m1, lane, float(LANES)), axis=-1, keepdims=True)
    el2 = jnp.where(lane == i1, ninf, el)
    m2 = jnp.max(el2, axis=-1, keepdims=True)
    i2 = jnp.min(jnp.where(el2 == m2, lane, float(LANES)), axis=-1, keepdims=True)
    r = jnp.exp(m2 - m1)
    g1 = p_group / (1.0 + r)
    g2 = p_group * r / (1.0 + r)
    e1 = i1 - N_EXPERT_GROUPS
    e2 = i2 - N_EXPERT_GROUPS
    oh1 = jnp.where(lane == e1, 1.0, 0.0)
    oh2 = jnp.where(lane == e2, 1.0, 0.0)
    both = oh1 + oh2
    ri = lax.broadcasted_iota(jnp.int32, (tm, tm), 0)
    ci = lax.broadcasted_iota(jnp.int32, (tm, tm), 1)
    earlier = jnp.where(ci < ri, 1.0, 0.0).astype(BF16)
    before = jnp.dot(earlier, both.astype(BF16), preferred_element_type=F32) + run_ref[...]
    rank1 = jnp.sum(oh1 * before, axis=-1, keepdims=True)
    rank2 = jnp.sum(oh2 * before, axis=-1, keepdims=True)
    run_ref[...] = run_ref[...] + jnp.sum(both, axis=0, keepdims=True)
    cnt_ref[...] = run_ref[...]
    packed = jnp.where(lane == 0, e1, jnp.where(lane == 1, e2, jnp.where(lane == 2, rank1, jnp.where(lane == 3, rank2, 0.0))))
    idx_ref[...] = packed.astype(jnp.int32)
    gate_ref[...] = jnp.where(lane == 0, g1, jnp.where(lane == 1, g2, 0.0))


def norm_and_route(x, norm_w, w_router, b_router, tm):
    t, d = x.shape
    row = lambda w: pl.BlockSpec((tm, w), lambda i: (i, 0))
    one = lambda w: pl.BlockSpec((1, w), lambda i: (0, 0))
    return pl.pallas_call(
        _router_body,
        grid=(t // tm,),
        in_specs=[row(d), one(d), pl.BlockSpec((d, LANES), lambda i: (0, 0)), one(LANES)],
        out_specs=[row(d // 2), row(LANES), row(LANES), one(LANES)],
        out_shape=[jax.ShapeDtypeStruct((t, d // 2), U32), jax.ShapeDtypeStruct((t, LANES), jnp.int32),
                   jax.ShapeDtypeStruct((t, LANES), F32), jax.ShapeDtypeStruct((1, LANES), F32)],
        scratch_shapes=[pltpu.VMEM((1, LANES), F32)],
        compiler_params=_params("arbitrary"),
        name="router",
    )(x, norm_w.reshape(1, d).astype(F32), w_router, b_router)


def _moe_plan(idx, counts, tm):
    t = idx.shape[0]
    n = t * TOP_K
    experts = jnp.arange(N_EXPERTS, dtype=I32)
    counts = counts[0, :N_EXPERTS].astype(I32)
    padded = (counts + tm - 1) // tm * tm
    pend = jnp.cumsum(padded)
    pstart = pend - padded
    e = idx[:, :TOP_K]
    rank = idx[:, TOP_K:2 * TOP_K]
    dest = rank + jnp.sum(jnp.where(e[:, :, None] == experts, pstart, 0), axis=-1)
    nblk = n // tm + N_EXPERTS
    n_used = pend[-1] // tm
    blk = jnp.arange(nblk, dtype=I32)
    blk_e = jnp.minimum(jnp.sum((pend[None, :] <= (blk * tm)[:, None]).astype(I32), axis=-1), N_EXPERTS - 1)
    last_e = jnp.sum(jnp.where(blk == n_used - 1, blk_e, 0))
    blk_e = jnp.where(blk < n_used, blk_e, last_e)
    later = (counts[None, :] > 0) & (experts[None, :] > experts[:, None])
    next_of = jnp.min(jnp.where(later, experts[None, :], N_EXPERTS), axis=1)
    next_of = jnp.where(next_of < N_EXPERTS, next_of, -1)
    blk_next = jnp.sum(jnp.where(blk_e[:, None] == experts, next_of[None, :], 0), axis=1)
    return (dest.astype(I32), pend.astype(I32), blk_e.astype(I32), blk_next.astype(I32),
            n_used.reshape(1).astype(I32), nblk * tm)


def _dispatch_body(pend_ref, nu_ref, dest_ref, h_ref, xs_hbm, idx_smem, zero_ref, sem, isem, *, tt, tm):
    i = pl.program_id(0)

    @pl.when(i == 0)
    def _():
        zero_ref[...] = jnp.zeros(zero_ref.shape, zero_ref.dtype)

        def zero_block(row0):
            cp = pltpu.make_async_copy(zero_ref, xs_hbm.at[pl.ds(pl.multiple_of(row0, tm), tm), :], sem)
            cp.start()
            cp.wait()

        def fill(e, c):
            end = pend_ref[e]
            prev = jnp.where(e > 0, pend_ref[jnp.maximum(e - 1, 0)], 0)

            @pl.when(end > prev)
            def _():
                zero_block(end - tm)
            return c

        lax.fori_loop(0, N_EXPERTS, fill, 0)

        def fill_unused(b, c):
            zero_block(b * tm)
            return c

        lax.fori_loop(nu_ref[0], xs_hbm.shape[0] // tm, fill_unused, 0)

    cp = pltpu.make_async_copy(dest_ref.at[0, 0], idx_smem, isem)
    cp.start()
    cp.wait()

    def issue(r, c):
        for k in range(TOP_K):
            d = idx_smem[TOP_K * r + k]
            pltpu.make_async_copy(h_ref.at[pl.ds(r, 1), :], xs_hbm.at[pl.ds(d, 1), :], sem).start(priority=k)
        return c

    lax.fori_loop(0, tt, issue, 0, unroll=8)
    all_rows = xs_hbm.at[pl.ds(0, TOP_K * tt), :]
    pltpu.make_async_copy(all_rows, all_rows, sem).wait()


def dispatch_rows(pend, n_used, dest, h_packed, rows, tt, tm):
    t, w = h_packed.shape
    return pl.pallas_call(
        functools.partial(_dispatch_body, tt=tt, tm=tm),
        grid_spec=pltpu.PrefetchScalarGridSpec(
            num_scalar_prefetch=2,
            grid=(t // tt,),
            in_specs=[
                pl.BlockSpec((1, 1, TOP_K * tt), lambda i, pe, nu: (i, 0, 0)),
                pl.BlockSpec((tt, w), lambda i, pe, nu: (i, 0)),
            ],
            out_specs=pl.BlockSpec(memory_space=pl.ANY),
            scratch_shapes=[
                pltpu.SMEM((TOP_K * tt,), I32),
                pltpu.VMEM((tm, w), U32),
                pltpu.SemaphoreType.DMA,
                pltpu.SemaphoreType.DMA,
            ],
        ),
        out_shape=jax.ShapeDtypeStruct((rows, w), U32),
        compiler_params=_params("arbitrary"),
        name="dispatch_rows",
    )(pend, n_used, dest.reshape(t // tt, 1, TOP_K * tt), h_packed)


def _expert_changed(be_ref, i):
    return (i == 0) | (be_ref[i] != be_ref[jnp.maximum(i - 1, 0)])


CAST_ROWS = 256


def _cast_rows(src_ref, dst_ref):
    def body(r, carry):
        rows = pl.ds(pl.multiple_of(r * CAST_ROWS, CAST_ROWS), CAST_ROWS)
        dst_ref[rows, :] = src_ref[rows, :].astype(dst_ref.dtype)
        return carry
    lax.fori_loop(0, src_ref.shape[0] // CAST_ROWS, body, 0)


def _stream_weights(be_ref, nx_ref, i, sweep, n_sweeps, copies, on_ready, cnt_ref):
    @pl.when((sweep == 0) & (i == 0))
    def _():
        cnt_ref[0] = 0
        for cp in copies(be_ref[0], 0, 0):
            cp.start()

    @pl.when(_expert_changed(be_ref, i))
    def _():
        slot = cnt_ref[0] & 1
        for cp in copies(be_ref[i], sweep, slot):
            cp.wait()
        on_ready(slot)
        nxt = nx_ref[i]

        @pl.when(nxt >= 0)
        def _():
            for cp in copies(nxt, sweep, 1 - slot):
                cp.start()

        @pl.when((nxt < 0) & (sweep + 1 < n_sweeps))
        def _():
            for cp in copies(be_ref[0], sweep + 1, 1 - slot):
                cp.start()

        cnt_ref[0] = cnt_ref[0] + 1


def _gate_up_body(be_ref, nx_ref, nu_ref, x_ref, wg_hbm, wu_hbm, o_ref, wbuf_ref, wgb_ref, wub_ref, sem_ref, cnt_ref,
                  *, tf):
    j = pl.program_id(0)
    i = pl.program_id(1)
    half = x_ref.shape[1]

    def copies(e, jj, slot):
        cols = pl.ds(pl.multiple_of(jj * tf, tf), tf)
        return (pltpu.make_async_copy(wg_hbm.at[e, :, cols], wbuf_ref.at[slot, 0], sem_ref.at[slot]),
                pltpu.make_async_copy(wu_hbm.at[e, :, cols], wbuf_ref.at[slot, 1], sem_ref.at[slot]))

    def on_ready(slot):
        _cast_rows(wbuf_ref.at[slot, 0], wgb_ref)
        _cast_rows(wbuf_ref.at[slot, 1], wub_ref)

    _stream_weights(be_ref, nx_ref, i, j, pl.num_programs(0), copies, on_ready, cnt_ref)

    @pl.when(i < nu_ref[0])
    def _():
        x_lo, x_hi = _unpack_pairs(x_ref[...])
        x_lo = x_lo.astype(BF16)
        x_hi = x_hi.astype(BF16)

        def mm(w_ref):
            return (jnp.dot(x_lo, w_ref[:half, :], preferred_element_type=F32)
                    + jnp.dot(x_hi, w_ref[half:, :], preferred_element_type=F32))

        a = mm(wgb_ref)
        b = mm(wub_ref)
        o_ref[...] = (a * _sigmoid(a) * b).astype(o_ref.dtype)

    @pl.when(i >= nu_ref[0])
    def _():
        o_ref[...] = jnp.zeros(o_ref.shape, o_ref.dtype)


def expert_gate_up(blk_e, blk_next, n_used, xs, w_gate, w_up, tm, tf):
    rows = xs.shape[0]
    d, f = w_gate.shape[1], w_gate.shape[2]
    nblk = rows // tm
    last = lambda i, nu: jnp.minimum(i, nu[0] - 1)
    hbm = pl.BlockSpec(memory_space=pl.ANY)
    return pl.pallas_call(
        functools.partial(_gate_up_body, tf=tf),
        grid_spec=pltpu.PrefetchScalarGridSpec(
            num_scalar_prefetch=3,
            grid=(f // tf, nblk),
            in_specs=[pl.BlockSpec((tm, d // 2), lambda j, i, be, nx, nu: (last(i, nu), 0)), hbm, hbm],
            out_specs=pl.BlockSpec((tm, tf), lambda j, i, be, nx, nu: (i, j)),
            scratch_shapes=[
                pltpu.VMEM((2, 2, d, tf), F32),
                pltpu.VMEM((d, tf), BF16),
                pltpu.VMEM((d, tf), BF16),
                pltpu.SemaphoreType.DMA((2,)),
                pltpu.SMEM((1,), I32),
            ],
        ),
        out_shape=jax.ShapeDtypeStruct((rows, f), BF16),
        compiler_params=_params("arbitrary", "arbitrary"),
        name="expert_gate_up",
    )(blk_e, blk_next, n_used, xs, w_gate, w_up)


def _down_body(be_ref, nx_ref, nu_ref, a_ref, wd_hbm, o_ref, wbuf_ref, wdb_ref, sem_ref, cnt_ref):
    i = pl.program_id(0)
    half = o_ref.shape[1]

    def copies(e, sweep, slot):
        return (pltpu.make_async_copy(wd_hbm.at[e], wbuf_ref.at[slot], sem_ref.at[slot]),)

    def on_ready(slot):
        _cast_rows(wbuf_ref.at[slot], wdb_ref)

    _stream_weights(be_ref, nx_ref, i, 0, 1, copies, on_ready, cnt_ref)

    @pl.when(i < nu_ref[0])
    def _():
        y = jnp.dot(a_ref[...], wdb_ref[...], preferred_element_type=F32)
        o_ref[...] = _pack_pairs(y[:, :half], y[:, half:])

    @pl.when(i >= nu_ref[0])
    def _():
        o_ref[...] = jnp.zeros(o_ref.shape, o_ref.dtype)


def expert_down(blk_e, blk_next, n_used, act, w_down, tm):
    rows, f = act.shape
    d = w_down.shape[2]
    nblk = rows // tm
    last = lambda i, nu: jnp.minimum(i, nu[0] - 1)
    return pl.pallas_call(
        _down_body,
        grid_spec=pltpu.PrefetchScalarGridSpec(
            num_scalar_prefetch=3,
            grid=(nblk,),
            in_specs=[
                pl.BlockSpec((tm, f), lambda i, be, nx, nu: (last(i, nu), 0)),
                pl.BlockSpec(memory_space=pl.ANY),
            ],
            out_specs=pl.BlockSpec((tm, d // 2), lambda i, be, nx, nu: (i, 0)),
            scratch_shapes=[
                pltpu.VMEM((2, f, d), F32),
                pltpu.VMEM((f, d), BF16),
                pltpu.SemaphoreType.DMA((2,)),
                pltpu.SMEM((1,), I32),
            ],
        ),
        out_shape=jax.ShapeDtypeStruct((rows, d // 2), U32),
        compiler_params=_params("arbitrary"),
        name="expert_down",
    )(blk_e, blk_next, n_used, act, w_down)


def _combine_body(dest_ref, x_ref, g_ref, w_ref, y_hbm, o_ref, idx_smem, buf_ref, sem, isem, *, tt):
    d = x_ref.shape[1]
    half = d // 2
    cp = pltpu.make_async_copy(dest_ref.at[0, 0], idx_smem, isem)
    cp.start()
    cp.wait()

    def issue(r, c):
        for k in range(TOP_K):
            src = idx_smem[TOP_K * r + k]
            pltpu.make_async_copy(y_hbm.at[pl.ds(src, 1), :], buf_ref.at[k, pl.ds(r, 1), :], sem).start(priority=k)
        return c

    lax.fori_loop(0, tt, issue, 0, unroll=8)
    pltpu.make_async_copy(buf_ref, buf_ref, sem).wait()

    y0_lo, y0_hi = _unpack_pairs(buf_ref[0])
    y1_lo, y1_hi = _unpack_pairs(buf_ref[1])
    g0 = g_ref[:, 0:1]
    g1 = g_ref[:, 1:2]
    x_lo = x_ref[:, :half] + (g0 * y0_lo + g1 * y1_lo)
    x_hi = x_ref[:, half:] + (g0 * y0_hi + g1 * y1_hi)
    ms = (jnp.sum(x_lo * x_lo, axis=-1, keepdims=True) + jnp.sum(x_hi * x_hi, axis=-1, keepdims=True)) / d
    inv = lax.rsqrt(ms + EPS)
    o_ref[:, :half] = x_lo * inv * w_ref[:, :half]
    o_ref[:, half:] = x_hi * inv * w_ref[:, half:]


def combine_and_norm(x, y_packed, dest, gate, w, tt):
    t, d = x.shape
    row = lambda wd: pl.BlockSpec((tt, wd), lambda i: (i, 0))
    return pl.pallas_call(
        functools.partial(_combine_body, tt=tt),
        grid=(t // tt,),
        in_specs=[
            pl.BlockSpec((1, 1, TOP_K * tt), lambda i: (i, 0, 0)),
            row(d), row(LANES), pl.BlockSpec((1, d), lambda i: (0, 0)),
            pl.BlockSpec(memory_space=pl.ANY),
        ],
        out_specs=row(d),
        out_shape=jax.ShapeDtypeStruct((t, d), F32),
        scratch_shapes=[
            pltpu.SMEM((TOP_K * tt,), I32),
            pltpu.VMEM((TOP_K, tt, d // 2), U32),
            pltpu.SemaphoreType.DMA,
            pltpu.SemaphoreType.DMA,
        ],
        compiler_params=_params("arbitrary"),
        name="combine_norm",
    )(dest.reshape(t // tt, 1, TOP_K * tt), x, gate, w.reshape(1, d).astype(F32), y_packed)


def _tiles(batch, seq):
    t = batch * seq
    return dict(
        norm_tm=min(256, t),
        mm_tm=min(1024, t),
        mm_tn=1024,
        out_tn=512,
        attn_tq=min(512, seq),
        attn_tk=min(512, seq),
        moe_tm=min(256, t),
        moe_tf=512,
        disp_tt=min(512, t),
        comb_tt=min(256, t),
    )


def _lambda_init_at(layer):
    return 0.8 - 0.6 * math.exp(-0.3 * layer)


def kernel(x, rel_bias, norm1_w, w_in, lambda_q1, lambda_k1, lambda_q2, lambda_k2, subln_w, conv_w, conv_b,
           dt_bias_f, dt_bias_b, a_log_f, a_log_b, d_skip, ssm_norm_w, w_out, norm2_w, w_group_router,
           b_group_router, w_expert_router, b_expert_router, w_gate, w_up, w_down, final_norm_w):
    batch, seq, d = x.shape
    t = batch * seq
    depth = norm1_w.shape[0]
    attn_w = d // 2
    ssm_w = d - attn_w
    dv = 2 * ATTN_HEAD_DIM
    a_heads = attn_w // dv
    s_heads = ssm_w // SSM_HEAD_DIM
    cc = ssm_w + 2 * SSM_GROUPS * SSM_STATE
    main_w = 3 * attn_w + ssm_w + cc
    assert ssm_w == attn_w and cc == 2 * ssm_w and 2 * s_heads <= LANES
    tl = _tiles(batch, seq)
    log2e = math.log2(math.e)
    band = _band_table(rel_bias, tl["attn_tq"], tl["attn_tk"], log2e)
    q_scale = jnp.where(jnp.arange(w_in.shape[2]) < attn_w, ATTN_HEAD_DIM ** -0.5 * log2e, 1.0).astype(F32)

    def pad_lanes(v):
        return jnp.pad(v.astype(F32), (0, LANES - v.shape[0])).reshape(1, LANES)

    xf = x.reshape(t, d)
    for layer in range(depth):
        lam_init = _lambda_init_at(layer)
        w_t = (jnp.transpose(w_in[layer]) * q_scale[:, None]).astype(BF16)
        w_dt = jnp.pad(w_t[main_w:], ((0, LANES - 2 * s_heads), (0, 0)))
        h = rmsnorm_rows(xf, norm1_w[layer], BF16, tl["norm_tm"])
        tm, tn = tl["mm_tm"], tl["mm_tn"]
        qk = matmul_nt(h, w_t, BF16, t, 2 * attn_w, tm, tn, "in_proj_qk")
        v_t = matmul_nt(w_t, h, BF16, attn_w, t, tn, tm, "in_proj_v", a_blk0=2 * attn_w // tn)
        z = matmul_nt(h, w_t, BF16, t, ssm_w, tm, tn, "in_proj_z", b_blk0=3 * attn_w // tn)
        xbc = matmul_nt(h, w_t, BF16, t, cc, tm, tn, "in_proj_xbc", b_blk0=(3 * attn_w + ssm_w) // tn)
        dt = matmul_nt(h, w_dt, F32, t, LANES, tm, LANES, "dt_proj")
        dt_t = matmul_nt(w_dt, h, F32, LANES, t, LANES, tm, "dt_proj_t")

        vec = lambda v: v[layer].reshape(1, -1).astype(F32)
        attn = diff_attention(qk, v_t, band, vec(lambda_q1), vec(lambda_k1), vec(lambda_q2), vec(lambda_k2),
                              subln_w[layer].reshape(dv, 1).astype(F32), batch=batch, seq=seq, heads=a_heads,
                              lam_init=lam_init, tq=tl["attn_tq"], tk=tl["attn_tk"])

        dt_bias = pad_lanes(jnp.concatenate([dt_bias_f[layer], dt_bias_b[layer]]))
        a_log = pad_lanes(jnp.concatenate([a_log_f[layer], a_log_b[layer]]))
        cw = conv_w[layer].astype(F32)
        cb = conv_b[layer].reshape(1, cc).astype(F32)
        common = dict(batch=batch, seq=seq, heads=s_heads)
        y_bwd, act = ssd_pass(xbc, dt, dt_t, cw, cb, dt_bias, a_log, None, reverse=True, final=False,
                              col_off=s_heads, **common)
        extra = dict(d_skip=jnp.repeat(d_skip[layer].astype(F32), SSM_HEAD_DIM).reshape(1, ssm_w),
                     norm_w=vec(ssm_norm_w), y_other=y_bwd, act=act, z=z)
        ssm = ssd_pass(None, dt, dt_t, cw, cb, dt_bias, a_log, extra, reverse=False, final=True, col_off=0, **common)

        x1 = out_projection(attn, ssm, w_out[layer].astype(BF16), xf, tl["mm_tm"], tl["out_tn"])

        w_router = jnp.pad(jnp.concatenate([w_group_router[layer], w_expert_router[layer]], axis=1).astype(F32),
                           ((0, 0), (0, LANES - N_EXPERT_GROUPS - N_EXPERTS)))
        b_router = pad_lanes(jnp.concatenate([b_group_router[layer], b_expert_router[layer]]))
        h2, idx, gate, counts = norm_and_route(x1, norm2_w[layer], w_router, b_router, tl["norm_tm"])
        tm = tl["moe_tm"]
        dest, pend, blk_e, blk_next, n_used, rows = _moe_plan(idx, counts, tm)
        xs = dispatch_rows(pend, n_used, dest, h2, rows, tl["disp_tt"], tm)
        act = expert_gate_up(blk_e, blk_next, n_used, xs, w_gate[layer], w_up[layer], tm, tl["moe_tf"])
        y = expert_down(blk_e, blk_next, n_used, act, w_down[layer], tm)
        if layer + 1 < depth:
            raise NotImplementedError("multi-layer stacking needs an un-normalised combine")
        xf = combine_and_norm(x1, y, dest, gate, final_norm_w, tl["comb_tt"])
    return xf.reshape(batch, seq, d)
```

```python
import functools
import math

import jax
import jax.numpy as jnp
from jax import lax
from jax.experimental import pallas as pl
from jax.experimental.pallas import tpu as pltpu

F32 = jnp.float32
BF16 = jnp.bfloat16
U32 = jnp.uint32
I32 = jnp.int32
EPS = 1e-6

ATTN_HEAD_DIM = 128
NUM_BUCKETS = 32
MAX_DISTANCE = 128
SSM_HEAD_DIM = 64
SSM_GROUPS = 8
SSM_STATE = 128
CONV_WIDTH = 5
CHUNK = 128
N_EXPERT_GROUPS = 4
EXPERTS_PER_GROUP = 8
N_EXPERTS = N_EXPERT_GROUPS * EXPERTS_PER_GROUP
TOP_K = 2

LANES = 128
SUBLANES = 8
BF16_ROWS = 16
VMEM_LIMIT = 56 * 1024 * 1024
HI16 = 0xFFFF0000


def _params(*sem):
    return pltpu.CompilerParams(dimension_semantics=sem, vmem_limit_bytes=VMEM_LIMIT)


def _sigmoid(x):
    return 1.0 / (1.0 + jnp.exp(-x))


def _pack_pairs(lo, hi):
    lo_b = lax.bitcast_convert_type(lo.astype(BF16).astype(F32), U32)
    hi_b = lax.bitcast_convert_type(hi.astype(BF16).astype(F32), U32)
    return (lo_b >> 16) | (hi_b & U32(HI16))


def _unpack_pairs(w):
    return lax.bitcast_convert_type(w << 16, F32), lax.bitcast_convert_type(w & U32(HI16), F32)


def _rmsnorm_body(x_ref, w_ref, o_ref):
    x = x_ref[...]
    ms = jnp.mean(x * x, axis=-1, keepdims=True)
    o_ref[...] = (x * lax.rsqrt(ms + EPS) * w_ref[...]).astype(o_ref.dtype)


def rmsnorm_rows(x, w, out_dtype, tm):
    t, d = x.shape
    return pl.pallas_call(
        _rmsnorm_body,
        grid=(t // tm,),
        in_specs=[pl.BlockSpec((tm, d), lambda i: (i, 0)), pl.BlockSpec((1, d), lambda i: (0, 0))],
        out_specs=pl.BlockSpec((tm, d), lambda i: (i, 0)),
        out_shape=jax.ShapeDtypeStruct((t, d), out_dtype),
        compiler_params=_params("parallel"),
        name="rmsnorm",
    )(x, w.reshape(1, d).astype(F32))


NT_DIMS = (((1,), (1,)), ((), ()))


def _matmul_nt_body(a_ref, b_ref, o_ref):
    o_ref[...] = lax.dot_general(a_ref[...], b_ref[...], NT_DIMS, preferred_element_type=F32).astype(o_ref.dtype)


def matmul_nt(a, b, out_dtype, m, n, tm, tn, name, a_blk0=0, b_blk0=0):
    k = a.shape[1]
    assert b.shape[1] == k and m % tm == 0 and n % tn == 0
    return pl.pallas_call(
        _matmul_nt_body,
        grid=(m // tm, n // tn),
        in_specs=[pl.BlockSpec((tm, k), lambda i, j: (a_blk0 + i, 0)),
                  pl.BlockSpec((tn, k), lambda i, j: (b_blk0 + j, 0))],
        out_specs=pl.BlockSpec((tm, tn), lambda i, j: (i, j)),
        out_shape=jax.ShapeDtypeStruct((m, n), out_dtype),
        compiler_params=_params("parallel", "parallel"),
        name=name,
    )(a, b)


def _proj_with_dt_body(a_ref, b_ref, wdt_ref, o_ref, dt_ref, dtt_ref):
    o_ref[...] = lax.dot_general(a_ref[...], b_ref[...], NT_DIMS, preferred_element_type=F32).astype(o_ref.dtype)

    @pl.when(pl.program_id(1) == 0)
    def _():
        dt_ref[...] = lax.dot_general(a_ref[...], wdt_ref[...], NT_DIMS, preferred_element_type=F32)
        dtt_ref[...] = lax.dot_general(wdt_ref[...], a_ref[...], NT_DIMS, preferred_element_type=F32)


def projection_with_dt(a, b, w_dt, out_dtype, m, n, tm, tn, name, b_blk0):
    k = a.shape[1]
    nd = w_dt.shape[0]
    return pl.pallas_call(
        _proj_with_dt_body,
        grid=(m // tm, n // tn),
        in_specs=[pl.BlockSpec((tm, k), lambda i, j: (i, 0)),
                  pl.BlockSpec((tn, k), lambda i, j: (b_blk0 + j, 0)),
                  pl.BlockSpec((nd, k), lambda i, j: (0, 0))],
        out_specs=[pl.BlockSpec((tm, tn), lambda i, j: (i, j)),
                   pl.BlockSpec((tm, nd), lambda i, j: (i, 0)),
                   pl.BlockSpec((nd, tm), lambda i, j: (0, i))],
        out_shape=[jax.ShapeDtypeStruct((m, n), out_dtype), jax.ShapeDtypeStruct((m, nd), F32),
                   jax.ShapeDtypeStruct((nd, m), F32)],
        compiler_params=_params("parallel", "arbitrary"),
        name=name,
    )(a, b, w_dt)


def _outproj_body(a_ref, s_ref, wa_ref, ws_ref, r_ref, o_ref):
    acc = jnp.dot(a_ref[...], wa_ref[...], preferred_element_type=F32)
    acc = acc + jnp.dot(s_ref[...], ws_ref[...], preferred_element_type=F32)
    o_ref[...] = r_ref[...] + acc


def out_projection(attn, ssm, w, resid, tm, tn):
    m, ka = attn.shape
    ks = ssm.shape[1]
    assert ka == ks
    n = w.shape[1]
    return pl.pallas_call(
        _outproj_body,
        grid=(n // tn, m // tm),
        in_specs=[
            pl.BlockSpec((tm, ka), lambda j, i: (i, 0)),
            pl.BlockSpec((tm, ks), lambda j, i: (i, 0)),
            pl.BlockSpec((ka, tn), lambda j, i: (0, j)),
            pl.BlockSpec((ks, tn), lambda j, i: (1, j)),
            pl.BlockSpec((tm, tn), lambda j, i: (i, j)),
        ],
        out_specs=pl.BlockSpec((tm, tn), lambda j, i: (i, j)),
        out_shape=jax.ShapeDtypeStruct((m, n), F32),
        compiler_params=_params("parallel", "parallel"),
        name="out_proj",
    )(attn, ssm, w, w, resid)


def _t5_bucket(rel):
    half = NUM_BUCKETS // 2
    max_exact = half // 2
    n = jnp.abs(rel)
    large = max_exact + (
        jnp.log(jnp.maximum(n, 1).astype(F32) / max_exact) / math.log(MAX_DISTANCE / max_exact) * (half - max_exact)
    ).astype(jnp.int32)
    large = jnp.minimum(large, half - 1)
    return jnp.where(rel > 0, half, 0) + jnp.where(n < max_exact, n, large)


def _band_body(v_ref, o_ref, *, tq, w):
    npad = v_ref.shape[-1]
    x = jnp.broadcast_to(v_ref[0], (tq, npad))
    y = pltpu.roll(x, npad - (tq - 1), 1, stride=1, stride_axis=0)
    o_ref[0] = y[:, :w].T


def _band_table(rel_bias, tq, tk, scale):
    assert tk + 1 >= MAX_DISTANCE
    heads = rel_bias.shape[1]
    w = tq + 4 * tk
    n = w + tq - 1
    npad = -(-n // LANES) * LANES
    rel = jnp.arange(npad, dtype=jnp.int32) - (2 * tk + tq - 1)
    v = (rel_bias[_t5_bucket(rel)].astype(F32) * scale).T.reshape(heads, 1, npad)
    return pl.pallas_call(
        functools.partial(_band_body, tq=tq, w=w),
        grid=(heads,),
        in_specs=[pl.BlockSpec((1, 1, npad), lambda h: (h, 0, 0))],
        out_specs=pl.BlockSpec((1, w, tq), lambda h: (h, 0, 0)),
        out_shape=jax.ShapeDtypeStruct((heads, w, tq), F32),
        compiler_params=_params("parallel"),
        name="band_table",
    )(v)


ATTN_STRIP = 32


def _attn_body(q_ref, k_ref, vt_ref, band_ref, eye_ref, lq1_ref, lk1_ref, lq2_ref, lk2_ref, subw_ref, o_ref,
               m_ref, l_ref, acc_ref, sa_ref, sb_ref, mxa_ref, mxb_ref, p_ref, *, tq, tk, nk, lam_init):
    dh = ATTN_HEAD_DIM
    rs = ATTN_STRIP
    qi = pl.program_id(2)

    m_ref[...] = jnp.full(m_ref.shape, -jnp.inf, F32)
    l_ref[...] = jnp.zeros(l_ref.shape, F32)
    acc_ref[...] = jnp.zeros(acc_ref.shape, F32)

    def scores(kc, s_ref, mx_ref):
        k0 = pl.multiple_of(kc * tk, tk)
        start = pl.multiple_of(jnp.clip(kc * tk - qi * tq + 2 * tk, 0, tq + 3 * tk), LANES)
        for mi in range(2):
            kk = k_ref[pl.ds(k0, tk), mi * dh:(mi + 1) * dh]
            q = q_ref[:, mi * dh:(mi + 1) * dh]
            s = lax.dot_general(kk, q, NT_DIMS, preferred_element_type=F32) + band_ref[0, pl.ds(start, tk), :]
            s_ref[mi] = s
            mx = s[0:SUBLANES]
            for i in range(1, tk // SUBLANES):
                mx = jnp.maximum(mx, s[i * SUBLANES:(i + 1) * SUBLANES])
            mx_ref[mi] = mx

    def softmax_pv(kc, s_ref, mx_ref):
        k0 = pl.multiple_of(kc * tk, tk)
        for mi in range(2):
            m_prev = m_ref[mi]
            m_new = jnp.maximum(m_prev, jnp.max(mx_ref[mi], axis=0, keepdims=True))
            alpha = jnp.exp2(m_prev - m_new)

            def strip_exp(i, ls):
                p = jnp.exp2(s_ref[mi, pl.ds(i * rs, rs), :] - m_new)
                p_ref[mi, pl.ds(i * rs, rs), :] = p.astype(BF16)
                return ls + p

            ls = lax.fori_loop(0, tk // rs, strip_exp, jnp.zeros((rs, tq), F32), unroll=True)
            l_ref[mi] = alpha * l_ref[mi] + jnp.sum(ls, axis=0, keepdims=True)
            m_ref[mi] = m_new
            pv = jnp.dot(vt_ref[:, pl.ds(k0, tk)], p_ref[mi], preferred_element_type=F32)
            acc_ref[mi] = acc_ref[mi] * alpha + pv

    scores(0, sa_ref, mxa_ref)
    if nk > 1:
        assert nk % 2 == 0

        def pair(j, carry):
            scores(2 * j + 1, sb_ref, mxb_ref)
            softmax_pv(2 * j, sa_ref, mxa_ref)
            scores(2 * j + 2, sa_ref, mxa_ref)
            softmax_pv(2 * j + 1, sb_ref, mxb_ref)
            return carry

        lax.fori_loop(0, nk // 2 - 1, pair, 0)
        scores(nk - 1, sb_ref, mxb_ref)
        softmax_pv(nk - 2, sa_ref, mxa_ref)
        softmax_pv(nk - 1, sb_ref, mxb_ref)
    else:
        softmax_pv(0, sa_ref, mxa_ref)

    lam = (jnp.exp(jnp.sum(lq1_ref[...] * lk1_ref[...], axis=-1, keepdims=True))
           - jnp.exp(jnp.sum(lq2_ref[...] * lk2_ref[...], axis=-1, keepdims=True)) + lam_init)
    o = acc_ref[0] / l_ref[0] - lam * (acc_ref[1] / l_ref[1])
    ms = jnp.mean(o * o, axis=0, keepdims=True)
    o = (o * lax.rsqrt(ms + EPS) * subw_ref[...] * (1.0 - lam_init)).astype(BF16)
    o_ref[...] = lax.dot_general(eye_ref[...], o, NT_DIMS, preferred_element_type=F32).astype(o_ref.dtype)


def diff_attention(qk, vt, band, lq1, lk1, lq2, lk2, subw, *, batch, seq, heads, lam_init, tq, tk):
    dv = 2 * ATTN_HEAD_DIM
    nq, nk = seq // tq, seq // tk
    assert tk % ATTN_STRIP == 0
    vec = pl.BlockSpec((1, ATTN_HEAD_DIM), lambda b, h, i: (0, 0))
    body = functools.partial(_attn_body, tq=tq, tk=tk, nk=nk, lam_init=lam_init)
    eye = jnp.eye(tq, dtype=BF16)
    return pl.pallas_call(
        body,
        grid=(batch, heads, nq),
        in_specs=[
            pl.BlockSpec((tq, dv), lambda b, h, i: (b * nq + i, h)),
            pl.BlockSpec((seq, dv), lambda b, h, i: (b, heads + h)),
            pl.BlockSpec((dv, seq), lambda b, h, i: (h, b)),
            pl.BlockSpec((1, tq + 4 * tk, tq), lambda b, h, i: (h, 0, 0)),
            pl.BlockSpec((tq, tq), lambda b, h, i: (0, 0)),
            vec, vec, vec, vec,
            pl.BlockSpec((dv, 1), lambda b, h, i: (0, 0)),
        ],
        out_specs=pl.BlockSpec((tq, dv), lambda b, h, i: (b * nq + i, h)),
        out_shape=jax.ShapeDtypeStruct((batch * seq, heads * dv), BF16),
        scratch_shapes=[
            pltpu.VMEM((2, 1, tq), F32),
            pltpu.VMEM((2, 1, tq), F32),
            pltpu.VMEM((2, dv, tq), F32),
            pltpu.VMEM((2, tk, tq), F32),
            pltpu.VMEM((2, tk, tq), F32),
            pltpu.VMEM((2, SUBLANES, tq), F32),
            pltpu.VMEM((2, SUBLANES, tq), F32),
            pltpu.VMEM((2, tk, tq), BF16),
        ],
        compiler_params=_params("parallel", "parallel", "parallel"),
        name="diff_attention",
    )(qk, qk, vt, band, eye, lq1, lk1, lq2, lk2, subw)


def _ssd_body(*refs, reverse, final, heads, col_off):
    if final:
        (z_ref, act_in_ref, dt_ref, dtt_ref, dtb_ref, alog_ref, dtbt_ref, alogt_ref, dskip_ref, nw_ref, yb_ref, o_ref,
         state_ref) = refs
    else:
        (xbc_ref, hp_ref, hn_ref, dt_ref, dtt_ref, cw_ref, cb_ref, dtb_ref, alog_ref, dtbt_ref, alogt_ref, o_ref,
         act_out_ref, u_ref, act_ref, state_ref) = refs
    L = CHUNK
    assert SSM_STATE == CHUNK
    G, N, P = SSM_GROUPS, SSM_STATE, SSM_HEAD_DIM
    R = heads // G
    GW = R * P
    W = heads * P
    CC = W + 2 * G * N
    HALO = SUBLANES
    c = pl.program_id(1)
    nc = pl.num_programs(1)
    pos = (nc - 1 - c) if reverse else c

    @pl.when(c == 0)
    def _():
        state_ref[...] = jnp.zeros(state_ref.shape, F32)

    if final:
        def chan(lo, hi):
            return act_in_ref[:, lo:hi].astype(F32)
    else:
        prev = hp_ref[...].astype(F32)[BF16_ROWS - HALO:, :]
        nxt = hn_ref[...].astype(F32)[:HALO, :]
        u_ref[0:HALO, :] = jnp.where(pos > 0, prev, 0.0)
        u_ref[HALO:HALO + L, :] = xbc_ref[...].astype(F32)
        u_ref[HALO + L:, :] = jnp.where(pos < nc - 1, nxt, 0.0)
        SLAB = 2 * LANES

        def conv_slab(j, carry):
            lo = pl.multiple_of(j * SLAB, SLAB)
            acc = jnp.broadcast_to(cb_ref[:, pl.ds(lo, SLAB)], (L, SLAB))
            for t in range(CONV_WIDTH):
                r0 = HALO - CONV_WIDTH // 2 + t
                acc = acc + cw_ref[t:t + 1, pl.ds(lo, SLAB)] * u_ref[r0:r0 + L, pl.ds(lo, SLAB)]
            act = acc * _sigmoid(acc)
            act_ref[:, pl.ds(lo, SLAB)] = act
            act_out_ref[:, pl.ds(lo, SLAB)] = act.astype(act_out_ref.dtype)
            return carry

        lax.fori_loop(0, CC // SLAB, conv_slab, 0)

        def chan(lo, hi):
            return act_ref[:, lo:hi]

    def softplus(v):
        return jnp.maximum(v, 0.0) + jnp.log1p(jnp.exp(-jnp.abs(v)))

    dtv = softplus(dt_ref[...] + dtb_ref[...])
    a = -jnp.exp(alog_ref[...]) * dtv
    a_t = -jnp.exp(alogt_ref[...]) * softplus(dtt_ref[...] + dtbt_ref[...])
    row = lax.broadcasted_iota(jnp.int32, (L, L), 0)
    col = lax.broadcasted_iota(jnp.int32, (L, L), 1)
    keep = (row <= col) if reverse else (row >= col)
    keep_t = (row >= col) if reverse else (row <= col)
    cum = jnp.dot(keep.astype(F32), a, precision=lax.Precision.HIGHEST, preferred_element_type=F32)
    cum_t = jnp.dot(a_t, keep_t.astype(F32), precision=lax.Precision.HIGHEST, preferred_element_type=F32)
    total = cum[0:1, :] if reverse else cum[L - 1:L, :]
    eye_bf = jnp.where(row == col, 1.0, 0.0).astype(BF16)
    seg = lax.broadcasted_iota(jnp.int32, (L, GW), 1) // P
    seg1 = lax.broadcasted_iota(jnp.int32, (1, GW), 1) // P

    for g in range(G):
        xs = chan(g * GW, (g + 1) * GW)
        bmat = chan(W + g * N, W + (g + 1) * N)
        cmat = chan(W + (G + g) * N, W + (G + g + 1) * N)
        b_bf = bmat.astype(BF16)
        c_bf = cmat.astype(BF16)
        cb = lax.dot_general(c_bf, b_bf, NT_DIMS, preferred_element_type=F32)
        bt_bf = lax.dot_general(eye_bf, b_bf, NT_DIMS, preferred_element_type=F32).astype(BF16)
        dt_e = jnp.zeros((L, GW), F32)
        cum_e = jnp.zeros((L, GW), F32)
        tot_e = jnp.zeros((1, GW), F32)
        for r in range(R):
            hc = col_off + g * R + r
            dt_e = jnp.where(seg == r, dtv[:, hc:hc + 1], dt_e)
            cum_e = jnp.where(seg == r, cum[:, hc:hc + 1], cum_e)
            tot_e = jnp.where(seg1 == r, total[:, hc:hc + 1], tot_e)
        x_dt = xs * dt_e
        y = jnp.zeros((L, GW), F32)
        for r in range(R):
            hc = col_off + g * R + r
            decay = jnp.exp(jnp.where(keep, cum[:, hc:hc + 1] - cum_t[hc:hc + 1, :], -jnp.inf))
            m_h = (cb * decay).astype(BF16)
            x_h = jnp.where(seg == r, x_dt, 0.0).astype(BF16)
            y = y + jnp.dot(m_h, x_h, preferred_element_type=F32)
        s_prev = state_ref[g]
        y = y + jnp.dot(c_bf, s_prev.astype(BF16), preferred_element_type=F32) * jnp.exp(cum_e)
        x_end = (x_dt * jnp.exp(tot_e - cum_e)).astype(BF16)
        state_ref[g] = s_prev * jnp.exp(tot_e) + jnp.dot(bt_bf, x_end, preferred_element_type=F32)
        if final:
            y = y + yb_ref[:, g * GW:(g + 1) * GW].astype(F32) + dskip_ref[:, g * GW:(g + 1) * GW] * xs
            zg = z_ref[:, g * GW:(g + 1) * GW].astype(F32)
            y = y * (zg * _sigmoid(zg))
            ms = jnp.mean(y * y, axis=-1, keepdims=True)
            y = y * lax.rsqrt(ms + EPS) * nw_ref[:, g * GW:(g + 1) * GW]
        o_ref[:, g * GW:(g + 1) * GW] = y.astype(o_ref.dtype)


def ssd_pass(xbc, dt, dt_t, conv_w, conv_b, dt_bias, a_log, extra, *, batch, seq, heads, reverse, final, col_off):
    L = CHUNK
    W = heads * SSM_HEAD_DIM
    CC = W + 2 * SSM_GROUPS * SSM_STATE
    GW = W // SSM_GROUPS
    nc = seq // L
    nhalo = batch * seq // BF16_ROWS
    per = L // BF16_ROWS

    def rb(b, c):
        return b * nc + ((nc - 1 - c) if reverse else c)

    full = lambda shape: pl.BlockSpec(shape, lambda b, c: (0,) * len(shape))
    rows = lambda w: pl.BlockSpec((L, w), lambda b, c: (rb(b, c), 0))
    cols = pl.BlockSpec((LANES, L), lambda b, c: (0, rb(b, c)))
    state = pltpu.VMEM((SSM_GROUPS, SSM_STATE, GW), F32)
    body = functools.partial(_ssd_body, reverse=reverse, final=final, heads=heads, col_off=col_off)
    t = batch * seq
    head_specs = [full((1, LANES)), full((1, LANES)), full((LANES, 1)), full((LANES, 1))]
    head_args = [dt_bias, a_log, dt_bias.reshape(LANES, 1), a_log.reshape(LANES, 1)]
    if final:
        in_specs = [rows(W), rows(CC), rows(LANES), cols] + head_specs + [full((1, W)), full((1, W)), rows(W)]
        args = [extra["z"], extra["act"], dt, dt_t] + head_args + [extra["d_skip"], extra["norm_w"], extra["y_other"]]
        out_specs = rows(W)
        out_shape = jax.ShapeDtypeStruct((t, W), BF16)
        scratch = [state]
    else:
        in_specs = [
            rows(CC),
            pl.BlockSpec((BF16_ROWS, CC), lambda b, c: (jnp.maximum(rb(b, c) * per - 1, 0), 0)),
            pl.BlockSpec((BF16_ROWS, CC), lambda b, c: (jnp.minimum((rb(b, c) + 1) * per, nhalo - 1), 0)),
            rows(LANES), cols,
            full((CONV_WIDTH, CC)), full((1, CC)),
        ] + head_specs
        args = [xbc, xbc, xbc, dt, dt_t, conv_w, conv_b] + head_args
        out_specs = [rows(W), rows(CC)]
        out_shape = [jax.ShapeDtypeStruct((t, W), BF16), jax.ShapeDtypeStruct((t, CC), BF16)]
        scratch = [pltpu.VMEM((L + 2 * SUBLANES, CC), F32), pltpu.VMEM((L, CC), F32), state]
    return pl.pallas_call(
        body,
        grid=(batch, nc),
        in_specs=in_specs,
        out_specs=out_specs,
        out_shape=out_shape,
        scratch_shapes=scratch,
        compiler_params=_params("parallel", "arbitrary"),
        name="ssd_final" if final else "ssd_scan",
    )(*args)


def _router_body(x_ref, nw_ref, wr_ref, br_ref, h_ref, idx_ref, gate_ref, cnt_ref, run_ref):
    tm, d = x_ref.shape
    half = d // 2

    @pl.when(pl.program_id(0) == 0)
    def _():
        run_ref[...] = jnp.zeros(run_ref.shape, F32)

    x = x_ref[...]
    ms = jnp.mean(x * x, axis=-1, keepdims=True)
    h = x * lax.rsqrt(ms + EPS) * nw_ref[...]
    h_ref[...] = _pack_pairs(h[:, :half], h[:, half:])
    h_hi = h.astype(BF16)
    h_lo = (h - h_hi.astype(F32)).astype(BF16)
    hw = jnp.dot(h_hi, wr_ref[...], preferred_element_type=F32)
    logits = (hw[:, :LANES] + (hw[:, LANES:] + jnp.dot(h_lo, wr_ref[:, :LANES], preferred_element_type=F32))
              + br_ref[...])
    lane = lax.broadcasted_iota(jnp.int32, logits.shape, 1).astype(F32)
    ninf = -jnp.inf
    is_g = lane < N_EXPERT_GROUPS
    gl = jnp.where(is_g, logits, ninf)
    gmax = jnp.max(gl, axis=-1, keepdims=True)
    gsum = jnp.sum(jnp.where(is_g, jnp.exp(gl - gmax), 0.0), axis=-1, keepdims=True)
    p_group = 1.0 / gsum
    g_sel = jnp.min(jnp.where(gl == gmax, lane, float(LANES)), axis=-1, keepdims=True)
    lo = N_EXPERT_GROUPS + EXPERTS_PER_GROUP * g_sel
    el = jnp.where((lane >= lo) & (lane < lo + EXPERTS_PER_GROUP), logits, ninf)
    m1 = jnp.max(el, axis=-1, keepdims=True)
    i1 = jnp.min(jnp.where(el == m1, lane, float(LANES)), axis=-1, keepdims=True)
    el2 = jnp.where(lane == i1, ninf, el)
    m2 = jnp.max(el2, axis=-1, keepdims=True)
    i2 = jnp.min(jnp.where(el2 == m2, lane, float(LANES)), axis=-1, keepdims=True)
    r = jnp.exp(m2 - m1)
    g1 = p_group / (1.0 + r)
    g2 = p_group * r / (1.0 + r)
    e1 = i1 - N_EXPERT_GROUPS
    e2 = i2 - N_EXPERT_GROUPS
    oh1 = jnp.where(lane == e1, 1.0, 0.0)
    oh2 = jnp.where(lane == e2, 1.0, 0.0)
    both = oh1 + oh2
    ri = lax.broadcasted_iota(jnp.int32, (tm, tm), 0)
    ci = lax.broadcasted_iota(jnp.int32, (tm, tm), 1)
    earlier = jnp.where(ci < ri, 1.0, 0.0).astype(BF16)
    before = jnp.dot(earlier, both.astype(BF16), preferred_element_type=F32) + run_ref[...]
    rank1 = jnp.sum(oh1 * before, axis=-1, keepdims=True)
    rank2 = jnp.sum(oh2 * before, axis=-1, keepdims=True)
    run_ref[...] = run_ref[...] + jnp.sum(both, axis=0, keepdims=True)
    cnt_ref[...] = run_ref[...]
    packed = jnp.where(lane == 0, e1, jnp.where(lane == 1, e2, jnp.where(lane == 2, rank1, jnp.where(lane == 3, rank2, 0.0))))
    idx_ref[...] = packed.astype(jnp.int32)
    gate_ref[...] = jnp.where(lane == 0, g1, jnp.where(lane == 1, g2, 0.0))


def norm_and_route(x, norm_w, w_router, b_router, tm):
    t, d = x.shape
    row = lambda w: pl.BlockSpec((tm, w), lambda i: (i, 0))
    one = lambda w: pl.BlockSpec((1, w), lambda i: (0, 0))
    w_hi = w_router.astype(BF16)
    w_router = jnp.concatenate([w_hi, (w_router - w_hi.astype(F32)).astype(BF16)], axis=1)
    return pl.pallas_call(
        _router_body,
        grid=(t // tm,),
        in_specs=[row(d), one(d), pl.BlockSpec((d, 2 * LANES), lambda i: (0, 0)), one(LANES)],
        out_specs=[row(d // 2), row(LANES), row(LANES), one(LANES)],
        out_shape=[jax.ShapeDtypeStruct((t, d // 2), U32), jax.ShapeDtypeStruct((t, LANES), jnp.int32),
                   jax.ShapeDtypeStruct((t, LANES), F32), jax.ShapeDtypeStruct((1, LANES), F32)],
        scratch_shapes=[pltpu.VMEM((1, LANES), F32)],
        compiler_params=_params("arbitrary"),
        name="router",
    )(x, norm_w.reshape(1, d).astype(F32), w_router, b_router)


def _moe_plan(idx, counts, tm):
    t = idx.shape[0]
    n = t * TOP_K
    experts = jnp.arange(N_EXPERTS, dtype=I32)
    counts = counts[0, :N_EXPERTS].astype(I32)
    padded = (counts + tm - 1) // tm * tm
    pend = jnp.cumsum(padded)
    pstart = pend - padded
    e = idx[:, :TOP_K]
    rank = idx[:, TOP_K:2 * TOP_K]
    dest = rank + jnp.sum(jnp.where(e[:, :, None] == experts, pstart, 0), axis=-1)
    nblk = n // tm + N_EXPERTS
    n_used = pend[-1] // tm
    blk = jnp.arange(nblk, dtype=I32)
    blk_e = jnp.minimum(jnp.sum((pend[None, :] <= (blk * tm)[:, None]).astype(I32), axis=-1), N_EXPERTS - 1)
    last_e = jnp.sum(jnp.where(blk == n_used - 1, blk_e, 0))
    blk_e = jnp.where(blk < n_used, blk_e, last_e)
    later = (counts[None, :] > 0) & (experts[None, :] > experts[:, None])
    next_of = jnp.min(jnp.where(later, experts[None, :], N_EXPERTS), axis=1)
    next_of = jnp.where(next_of < N_EXPERTS, next_of, -1)
    blk_next = jnp.sum(jnp.where(blk_e[:, None] == experts, next_of[None, :], 0), axis=1)
    return (dest.astype(I32), pend.astype(I32), blk_e.astype(I32), blk_next.astype(I32),
            n_used.reshape(1).astype(I32), nblk * tm)


def _dispatch_body(pend_ref, nu_ref, dest_ref, h_ref, xs_hbm, idx_smem, zero_ref, sem, isem, *, tt, tm):
    i = pl.program_id(0)

    @pl.when(i == 0)
    def _():
        zero_ref[...] = jnp.zeros(zero_ref.shape, zero_ref.dtype)

        def zero_block(row0):
            cp = pltpu.make_async_copy(zero_ref, xs_hbm.at[pl.ds(pl.multiple_of(row0, tm), tm), :], sem)
            cp.start()
            cp.wait()

        def fill(e, c):
            end = pend_ref[e]
            prev = jnp.where(e > 0, pend_ref[jnp.maximum(e - 1, 0)], 0)

            @pl.when(end > prev)
            def _():
                zero_block(end - tm)
            return c

        lax.fori_loop(0, N_EXPERTS, fill, 0)

        def fill_unused(b, c):
            zero_block(b * tm)
            return c

        lax.fori_loop(nu_ref[0], xs_hbm.shape[0] // tm, fill_unused, 0)

    cp = pltpu.make_async_copy(dest_ref.at[0, 0], idx_smem, isem)
    cp.start()
    cp.wait()

    def issue(r, c):
        for k in range(TOP_K):
            d = idx_smem[TOP_K * r + k]
            pltpu.make_async_copy(h_ref.at[pl.ds(r, 1), :], xs_hbm.at[pl.ds(d, 1), :], sem).start(priority=k)
        return c

    lax.fori_loop(0, tt, issue, 0, unroll=8)
    all_rows = xs_hbm.at[pl.ds(0, TOP_K * tt), :]
    pltpu.make_async_copy(all_rows, all_rows, sem).wait()


def dispatch_rows(pend, n_used, dest, h_packed, rows, tt, tm):
    t, w = h_packed.shape
    return pl.pallas_call(
        functools.partial(_dispatch_body, tt=tt, tm=tm),
        grid_spec=pltpu.PrefetchScalarGridSpec(
            num_scalar_prefetch=2,
            grid=(t // tt,),
            in_specs=[
                pl.BlockSpec((1, 1, TOP_K * tt), lambda i, pe, nu: (i, 0, 0)),
                pl.BlockSpec((tt, w), lambda i, pe, nu: (i, 0)),
            ],
            out_specs=pl.BlockSpec(memory_space=pl.ANY),
            scratch_shapes=[
                pltpu.SMEM((TOP_K * tt,), I32),
                pltpu.VMEM((tm, w), U32),
                pltpu.SemaphoreType.DMA,
                pltpu.SemaphoreType.DMA,
            ],
        ),
        out_shape=jax.ShapeDtypeStruct((rows, w), U32),
        compiler_params=_params("arbitrary"),
        name="dispatch_rows",
    )(pend, n_used, dest.reshape(t // tt, 1, TOP_K * tt), h_packed)


def _expert_changed(be_ref, i):
    return (i == 0) | (be_ref[i] != be_ref[jnp.maximum(i - 1, 0)])


CAST_ROWS = 256


def _cast_rows(src_ref, dst_ref):
    def body(r, carry):
        rows = pl.ds(pl.multiple_of(r * CAST_ROWS, CAST_ROWS), CAST_ROWS)
        dst_ref[rows, :] = src_ref[rows, :].astype(dst_ref.dtype)
        return carry
    lax.fori_loop(0, src_ref.shape[0] // CAST_ROWS, body, 0)


def _stream_weights(be_ref, nx_ref, i, sweep, n_sweeps, copies, on_ready, cnt_ref):
    @pl.when((sweep == 0) & (i == 0))
    def _():
        cnt_ref[0] = 0
        for cp in copies(be_ref[0], 0, 0):
            cp.start()

    @pl.when(_expert_changed(be_ref, i))
    def _():
        slot = cnt_ref[0] & 1
        for cp in copies(be_ref[i], sweep, slot):
            cp.wait()
        on_ready(slot)
        nxt = nx_ref[i]

        @pl.when(nxt >= 0)
        def _():
            for cp in copies(nxt, sweep, 1 - slot):
                cp.start()

        @pl.when((nxt < 0) & (sweep + 1 < n_sweeps))
        def _():
            for cp in copies(be_ref[0], sweep + 1, 1 - slot):
                cp.start()

        cnt_ref[0] = cnt_ref[0] + 1


def _gate_up_body(be_ref, nx_ref, nu_ref, x_ref, wg_hbm, wu_hbm, o_ref, wbuf_ref, wgb_ref, wub_ref, sem_ref, cnt_ref,
                  *, tf):
    j = pl.program_id(0)
    i = pl.program_id(1)
    half = x_ref.shape[1]

    def copies(e, jj, slot):
        cols = pl.ds(pl.multiple_of(jj * tf, tf), tf)
        return (pltpu.make_async_copy(wg_hbm.at[e, :, cols], wbuf_ref.at[slot, 0], sem_ref.at[slot]),
                pltpu.make_async_copy(wu_hbm.at[e, :, cols], wbuf_ref.at[slot, 1], sem_ref.at[slot]))

    def on_ready(slot):
        _cast_rows(wbuf_ref.at[slot, 0], wgb_ref)
        _cast_rows(wbuf_ref.at[slot, 1], wub_ref)

    _stream_weights(be_ref, nx_ref, i, j, pl.num_programs(0), copies, on_ready, cnt_ref)

    @pl.when(i < nu_ref[0])
    def _():
        x_lo, x_hi = _unpack_pairs(x_ref[...])
        x_lo = x_lo.astype(BF16)
        x_hi = x_hi.astype(BF16)

        def mm(w_ref):
            return (jnp.dot(x_lo, w_ref[:half, :], preferred_element_type=F32)
                    + jnp.dot(x_hi, w_ref[half:, :], preferred_element_type=F32))

        a = mm(wgb_ref)
        b = mm(wub_ref)
        o_ref[...] = (a * _sigmoid(a) * b).astype(o_ref.dtype)

    @pl.when(i >= nu_ref[0])
    def _():
        o_ref[...] = jnp.zeros(o_ref.shape, o_ref.dtype)


def expert_gate_up(blk_e, blk_next, n_used, xs, w_gate, w_up, tm, tf):
    rows = xs.shape[0]
    d, f = w_gate.shape[1], w_gate.shape[2]
    nblk = rows // tm
    last = lambda i, nu: jnp.minimum(i, nu[0] - 1)
    hbm = pl.BlockSpec(memory_space=pl.ANY)
    return pl.pallas_call(
        functools.partial(_gate_up_body, tf=tf),
        grid_spec=pltpu.PrefetchScalarGridSpec(
            num_scalar_prefetch=3,
            grid=(f // tf, nblk),
            in_specs=[pl.BlockSpec((tm, d // 2), lambda j, i, be, nx, nu: (last(i, nu), 0)), hbm, hbm],
            out_specs=pl.BlockSpec((tm, tf), lambda j, i, be, nx, nu: (i, j)),
            scratch_shapes=[
                pltpu.VMEM((2, 2, d, tf), F32),
                pltpu.VMEM((d, tf), BF16),
                pltpu.VMEM((d, tf), BF16),
                pltpu.SemaphoreType.DMA((2,)),
                pltpu.SMEM((1,), I32),
            ],
        ),
        out_shape=jax.ShapeDtypeStruct((rows, f), BF16),
        compiler_params=_params("arbitrary", "arbitrary"),
        name="expert_gate_up",
    )(blk_e, blk_next, n_used, xs, w_gate, w_up)


def _down_body(be_ref, nx_ref, nu_ref, a_ref, wd_hbm, o_ref, wbuf_ref, wdb_ref, sem_ref, cnt_ref):
    i = pl.program_id(0)
    half = o_ref.shape[1]

    def copies(e, sweep, slot):
        return (pltpu.make_async_copy(wd_hbm.at[e], wbuf_ref.at[slot], sem_ref.at[slot]),)

    def on_ready(slot):
        _cast_rows(wbuf_ref.at[slot], wdb_ref)

    _stream_weights(be_ref, nx_ref, i, 0, 1, copies, on_ready, cnt_ref)

    @pl.when(i < nu_ref[0])
    def _():
        y = jnp.dot(a_ref[...], wdb_ref[...], preferred_element_type=F32)
        o_ref[...] = _pack_pairs(y[:, :half], y[:, half:])

    @pl.when(i >= nu_ref[0])
    def _():
        o_ref[...] = jnp.zeros(o_ref.shape, o_ref.dtype)


def expert_down(blk_e, blk_next, n_used, act, w_down, tm):
    rows, f = act.shape
    d = w_down.shape[2]
    nblk = rows // tm
    last = lambda i, nu: jnp.minimum(i, nu[0] - 1)
    return pl.pallas_call(
        _down_body,
        grid_spec=pltpu.PrefetchScalarGridSpec(
            num_scalar_prefetch=3,
            grid=(nblk,),
            in_specs=[
                pl.BlockSpec((tm, f), lambda i, be, nx, nu: (last(i, nu), 0)),
                pl.BlockSpec(memory_space=pl.ANY),
            ],
            out_specs=pl.BlockSpec((tm, d // 2), lambda i, be, nx, nu: (i, 0)),
            scratch_shapes=[
                pltpu.VMEM((2, f, d), F32),
                pltpu.VMEM((f, d), BF16),
                pltpu.SemaphoreType.DMA((2,)),
                pltpu.SMEM((1,), I32),
            ],
        ),
        out_shape=jax.ShapeDtypeStruct((rows, d // 2), U32),
        compiler_params=_params("arbitrary"),
        name="expert_down",
    )(blk_e, blk_next, n_used, act, w_down)


def _combine_body(dest_ref, x_ref, g_ref, w_ref, y_hbm, o_ref, idx_smem, buf_ref, sem, isem, *, tt):
    d = x_ref.shape[1]
    half = d // 2
    cp = pltpu.make_async_copy(dest_ref.at[0, 0], idx_smem, isem)
    cp.start()
    cp.wait()

    def issue(r, c):
        for k in range(TOP_K):
            src = idx_smem[TOP_K * r + k]
            pltpu.make_async_copy(y_hbm.at[pl.ds(src, 1), :], buf_ref.at[k, pl.ds(r, 1), :], sem).start(priority=k)
        return c

    lax.fori_loop(0, tt, issue, 0, unroll=8)
    pltpu.make_async_copy(buf_ref, buf_ref, sem).wait()

    y0_lo, y0_hi = _unpack_pairs(buf_ref[0])
    y1_lo, y1_hi = _unpack_pairs(buf_ref[1])
    g0 = g_ref[:, 0:1]
    g1 = g_ref[:, 1:2]
    x_lo = x_ref[:, :half] + (g0 * y0_lo + g1 * y1_lo)
    x_hi = x_ref[:, half:] + (g0 * y0_hi + g1 * y1_hi)
    ms = (jnp.sum(x_lo * x_lo, axis=-1, keepdims=True) + jnp.sum(x_hi * x_hi, axis=-1, keepdims=True)) / d
    inv = lax.rsqrt(ms + EPS)
    o_ref[:, :half] = x_lo * inv * w_ref[:, :half]
    o_ref[:, half:] = x_hi * inv * w_ref[:, half:]


def combine_and_norm(x, y_packed, dest, gate, w, tt):
    t, d = x.shape
    row = lambda wd: pl.BlockSpec((tt, wd), lambda i: (i, 0))
    return pl.pallas_call(
        functools.partial(_combine_body, tt=tt),
        grid=(t // tt,),
        in_specs=[
            pl.BlockSpec((1, 1, TOP_K * tt), lambda i: (i, 0, 0)),
            row(d), row(LANES), pl.BlockSpec((1, d), lambda i: (0, 0)),
            pl.BlockSpec(memory_space=pl.ANY),
        ],
        out_specs=row(d),
        out_shape=jax.ShapeDtypeStruct((t, d), F32),
        scratch_shapes=[
            pltpu.SMEM((TOP_K * tt,), I32),
            pltpu.VMEM((TOP_K, tt, d // 2), U32),
            pltpu.SemaphoreType.DMA,
            pltpu.SemaphoreType.DMA,
        ],
        compiler_params=_params("arbitrary"),
        name="combine_norm",
    )(dest.reshape(t // tt, 1, TOP_K * tt), x, gate, w.reshape(1, d).astype(F32), y_packed)


def _tiles(batch, seq):
    t = batch * seq
    return dict(
        norm_tm=min(256, t),
        mm_tm=min(1024, t),
        mm_tn=1024,
        out_tm=min(256, t),
        out_tn=2048,
        attn_tq=min(512, seq),
        attn_tk=min(512, seq),
        moe_tm=min(256, t),
        moe_tf=512,
        disp_tt=min(512, t),
        comb_tt=min(256, t),
    )


def _lambda_init_at(layer):
    return 0.8 - 0.6 * math.exp(-0.3 * layer)


def kernel(x, rel_bias, norm1_w, w_in, lambda_q1, lambda_k1, lambda_q2, lambda_k2, subln_w, conv_w, conv_b,
           dt_bias_f, dt_bias_b, a_log_f, a_log_b, d_skip, ssm_norm_w, w_out, norm2_w, w_group_router,
           b_group_router, w_expert_router, b_expert_router, w_gate, w_up, w_down, final_norm_w):
    batch, seq, d = x.shape
    t = batch * seq
    depth = norm1_w.shape[0]
    attn_w = d // 2
    ssm_w = d - attn_w
    dv = 2 * ATTN_HEAD_DIM
    a_heads = attn_w // dv
    s_heads = ssm_w // SSM_HEAD_DIM
    cc = ssm_w + 2 * SSM_GROUPS * SSM_STATE
    main_w = 3 * attn_w + ssm_w + cc
    assert ssm_w == attn_w and cc == 2 * ssm_w and 2 * s_heads <= LANES
    tl = _tiles(batch, seq)
    log2e = math.log2(math.e)
    band = _band_table(rel_bias, tl["attn_tq"], tl["attn_tk"], log2e)
    q_scale = jnp.where(jnp.arange(w_in.shape[2]) < attn_w, ATTN_HEAD_DIM ** -0.5 * log2e, 1.0).astype(F32)

    def pad_lanes(v):
        return jnp.pad(v.astype(F32), (0, LANES - v.shape[0])).reshape(1, LANES)

    xf = x.reshape(t, d)
    for layer in range(depth):
        lam_init = _lambda_init_at(layer)
        w_t = (jnp.transpose(w_in[layer]) * q_scale[:, None]).astype(BF16)
        w_dt = jnp.pad(w_t[main_w:], ((0, LANES - 2 * s_heads), (0, 0)))
        h = rmsnorm_rows(xf, norm1_w[layer], BF16, tl["norm_tm"])
        tm, tn = tl["mm_tm"], tl["mm_tn"]
        qk = matmul_nt(h, w_t, BF16, t, 2 * attn_w, tm, tn, "in_proj_qk")
        v_t = matmul_nt(w_t, h, BF16, attn_w, t, tn, tm, "in_proj_v", a_blk0=2 * attn_w // tn)
        z, dt, dt_t = projection_with_dt(h, w_t, w_dt, BF16, t, ssm_w, tm, tn, "in_proj_z_dt",
                                         b_blk0=3 * attn_w // tn)
        xbc = matmul_nt(h, w_t, BF16, t, cc, tm, tn, "in_proj_xbc", b_blk0=(3 * attn_w + ssm_w) // tn)

        vec = lambda v: v[layer].reshape(1, -1).astype(F32)
        attn = diff_attention(qk, v_t, band, vec(lambda_q1), vec(lambda_k1), vec(lambda_q2), vec(lambda_k2),
                              subln_w[layer].reshape(dv, 1).astype(F32), batch=batch, seq=seq, heads=a_heads,
                              lam_init=lam_init, tq=tl["attn_tq"], tk=tl["attn_tk"])

        dt_bias = pad_lanes(jnp.concatenate([dt_bias_f[layer], dt_bias_b[layer]]))
        a_log = pad_lanes(jnp.concatenate([a_log_f[layer], a_log_b[layer]]))
        cw = conv_w[layer].astype(F32)
        cb = conv_b[layer].reshape(1, cc).astype(F32)
        common = dict(batch=batch, seq=seq, heads=s_heads)
        y_bwd, act = ssd_pass(xbc, dt, dt_t, cw, cb, dt_bias, a_log, None, reverse=True, final=False,
                              col_off=s_heads, **common)
        extra = dict(d_skip=jnp.repeat(d_skip[layer].astype(F32), SSM_HEAD_DIM).reshape(1, ssm_w),
                     norm_w=vec(ssm_norm_w), y_other=y_bwd, act=act, z=z)
        ssm = ssd_pass(None, dt, dt_t, cw, cb, dt_bias, a_log, extra, reverse=False, final=True, col_off=0, **common)

        x1 = out_projection(attn, ssm, w_out[layer].astype(BF16), xf, tl["out_tm"], tl["out_tn"])

        w_router = jnp.pad(jnp.concatenate([w_group_router[layer], w_expert_router[layer]], axis=1).astype(F32),
                           ((0, 0), (0, LANES - N_EXPERT_GROUPS - N_EXPERTS)))
        b_router = pad_lanes(jnp.concatenate([b_group_router[layer], b_expert_router[layer]]))
        h2, idx, gate, counts = norm_and_route(x1, norm2_w[layer], w_router, b_router, tl["norm_tm"])
        tm = tl["moe_tm"]
        dest, pend, blk_e, blk_next, n_used, rows = _moe_plan(idx, counts, tm)
        xs = dispatch_rows(pend, n_used, dest, h2, rows, tl["disp_tt"], tm)
        act = expert_gate_up(blk_e, blk_next, n_used, xs, w_gate[layer], w_up[layer], tm, tl["moe_tf"])
        y = expert_down(blk_e, blk_next, n_used, act, w_down[layer], tm)
        if layer + 1 < depth:
            raise NotImplementedError("multi-layer stacking needs an un-normalised combine")
        xf = combine_and_norm(x1, y, dest, gate, final_norm_w, tl["comb_tt"])
    return xf.reshape(batch, seq, d)
```

```python
import functools
import math

import jax
import jax.numpy as jnp
from jax import lax
from jax.experimental import pallas as pl
from jax.experimental.pallas import tpu as pltpu

F32 = jnp.float32
BF16 = jnp.bfloat16
U32 = jnp.uint32
I32 = jnp.int32
EPS = 1e-6

ATTN_HEAD_DIM = 128
NUM_BUCKETS = 32
MAX_DISTANCE = 128
SSM_HEAD_DIM = 64
SSM_GROUPS = 8
SSM_STATE = 128
CONV_WIDTH = 5
CHUNK = 128
N_EXPERT_GROUPS = 4
EXPERTS_PER_GROUP = 8
N_EXPERTS = N_EXPERT_GROUPS * EXPERTS_PER_GROUP
TOP_K = 2

LANES = 128
SUBLANES = 8
BF16_ROWS = 16
VMEM_LIMIT = 56 * 1024 * 1024
HI16 = 0xFFFF0000


def _params(*sem):
    return pltpu.CompilerParams(dimension_semantics=sem, vmem_limit_bytes=VMEM_LIMIT)


def _sigmoid(x):
    return 1.0 / (1.0 + jnp.exp(-x))


def _pack_pairs(lo, hi):
    lo_b = lax.bitcast_convert_type(lo.astype(BF16).astype(F32), U32)
    hi_b = lax.bitcast_convert_type(hi.astype(BF16).astype(F32), U32)
    return (lo_b >> 16) | (hi_b & U32(HI16))


def _unpack_pairs(w):
    return lax.bitcast_convert_type(w << 16, F32), lax.bitcast_convert_type(w & U32(HI16), F32)


def _rmsnorm_body(x_ref, w_ref, o_ref):
    x = x_ref[...]
    ms = jnp.mean(x * x, axis=-1, keepdims=True)
    o_ref[...] = (x * lax.rsqrt(ms + EPS) * w_ref[...]).astype(o_ref.dtype)


def rmsnorm_rows(x, w, out_dtype, tm):
    t, d = x.shape
    return pl.pallas_call(
        _rmsnorm_body,
        grid=(t // tm,),
        in_specs=[pl.BlockSpec((tm, d), lambda i: (i, 0)), pl.BlockSpec((1, d), lambda i: (0, 0))],
        out_specs=pl.BlockSpec((tm, d), lambda i: (i, 0)),
        out_shape=jax.ShapeDtypeStruct((t, d), out_dtype),
        compiler_params=_params("parallel"),
        name="rmsnorm",
    )(x, w.reshape(1, d).astype(F32))


NT_DIMS = (((1,), (1,)), ((), ()))


def _matmul_nt_body(a_ref, b_ref, o_ref):
    o_ref[...] = lax.dot_general(a_ref[...], b_ref[...], NT_DIMS, preferred_element_type=F32).astype(o_ref.dtype)


def matmul_nt(a, b, out_dtype, m, n, tm, tn, name, a_blk0=0, b_blk0=0):
    k = a.shape[1]
    assert b.shape[1] == k and m % tm == 0 and n % tn == 0
    return pl.pallas_call(
        _matmul_nt_body,
        grid=(m // tm, n // tn),
        in_specs=[pl.BlockSpec((tm, k), lambda i, j: (a_blk0 + i, 0)),
                  pl.BlockSpec((tn, k), lambda i, j: (b_blk0 + j, 0))],
        out_specs=pl.BlockSpec((tm, tn), lambda i, j: (i, j)),
        out_shape=jax.ShapeDtypeStruct((m, n), out_dtype),
        compiler_params=_params("parallel", "parallel"),
        name=name,
    )(a, b)


def _proj_with_dt_body(a_ref, b_ref, wdt_ref, o_ref, dt_ref, dtt_ref):
    o_ref[...] = lax.dot_general(a_ref[...], b_ref[...], NT_DIMS, preferred_element_type=F32).astype(o_ref.dtype)

    @pl.when(pl.program_id(1) == 0)
    def _():
        dt_ref[...] = lax.dot_general(a_ref[...], wdt_ref[...], NT_DIMS, preferred_element_type=F32)
        dtt_ref[...] = lax.dot_general(wdt_ref[...], a_ref[...], NT_DIMS, preferred_element_type=F32)


def projection_with_dt(a, b, w_dt, out_dtype, m, n, tm, tn, name, b_blk0):
    k = a.shape[1]
    nd = w_dt.shape[0]
    return pl.pallas_call(
        _proj_with_dt_body,
        grid=(m // tm, n // tn),
        in_specs=[pl.BlockSpec((tm, k), lambda i, j: (i, 0)),
                  pl.BlockSpec((tn, k), lambda i, j: (b_blk0 + j, 0)),
                  pl.BlockSpec((nd, k), lambda i, j: (0, 0))],
        out_specs=[pl.BlockSpec((tm, tn), lambda i, j: (i, j)),
                   pl.BlockSpec((tm, nd), lambda i, j: (i, 0)),
                   pl.BlockSpec((nd, tm), lambda i, j: (0, i))],
        out_shape=[jax.ShapeDtypeStruct((m, n), out_dtype), jax.ShapeDtypeStruct((m, nd), F32),
                   jax.ShapeDtypeStruct((nd, m), F32)],
        compiler_params=_params("parallel", "arbitrary"),
        name=name,
    )(a, b, w_dt)


def _outproj_body(a_ref, s_ref, wa_ref, ws_ref, r_ref, o_ref):
    acc = jnp.dot(a_ref[...], wa_ref[...], preferred_element_type=F32)
    acc = acc + jnp.dot(s_ref[...], ws_ref[...], preferred_element_type=F32)
    o_ref[...] = r_ref[...] + acc


def out_projection(attn, ssm, w, resid, tm, tn):
    m, ka = attn.shape
    ks = ssm.shape[1]
    assert ka == ks
    n = w.shape[1]
    return pl.pallas_call(
        _outproj_body,
        grid=(n // tn, m // tm),
        in_specs=[
            pl.BlockSpec((tm, ka), lambda j, i: (i, 0)),
            pl.BlockSpec((tm, ks), lambda j, i: (i, 0)),
            pl.BlockSpec((ka, tn), lambda j, i: (0, j)),
            pl.BlockSpec((ks, tn), lambda j, i: (1, j)),
            pl.BlockSpec((tm, tn), lambda j, i: (i, j)),
        ],
        out_specs=pl.BlockSpec((tm, tn), lambda j, i: (i, j)),
        out_shape=jax.ShapeDtypeStruct((m, n), F32),
        compiler_params=_params("parallel", "parallel"),
        name="out_proj",
    )(attn, ssm, w, w, resid)


def _t5_bucket(rel):
    half = NUM_BUCKETS // 2
    max_exact = half // 2
    n = jnp.abs(rel)
    large = max_exact + (
        jnp.log(jnp.maximum(n, 1).astype(F32) / max_exact) / math.log(MAX_DISTANCE / max_exact) * (half - max_exact)
    ).astype(jnp.int32)
    large = jnp.minimum(large, half - 1)
    return jnp.where(rel > 0, half, 0) + jnp.where(n < max_exact, n, large)


def _band_body(v_ref, o_ref, *, tq, w):
    npad = v_ref.shape[-1]
    x = jnp.broadcast_to(v_ref[0], (tq, npad))
    y = pltpu.roll(x, npad - (tq - 1), 1, stride=1, stride_axis=0)
    o_ref[0] = y[:, :w].T


def _band_table(rel_bias, tq, tk, scale):
    assert tk + 1 >= MAX_DISTANCE
    heads = rel_bias.shape[1]
    w = tq + 4 * tk
    n = w + tq - 1
    npad = -(-n // LANES) * LANES
    rel = jnp.arange(npad, dtype=jnp.int32) - (2 * tk + tq - 1)
    v = (rel_bias[_t5_bucket(rel)].astype(F32) * scale).T.reshape(heads, 1, npad)
    return pl.pallas_call(
        functools.partial(_band_body, tq=tq, w=w),
        grid=(heads,),
        in_specs=[pl.BlockSpec((1, 1, npad), lambda h: (h, 0, 0))],
        out_specs=pl.BlockSpec((1, w, tq), lambda h: (h, 0, 0)),
        out_shape=jax.ShapeDtypeStruct((heads, w, tq), F32),
        compiler_params=_params("parallel"),
        name="band_table",
    )(v)


ATTN_STRIP = 16


def _attn_body(q_ref, k_ref, vt_ref, band_ref, eye_ref, lq1_ref, lk1_ref, lq2_ref, lk2_ref, subw_ref, o_ref,
               m_ref, l_ref, acc_ref, sa_ref, sb_ref, mxa_ref, mxb_ref, p_ref, *, tq, tk, nk, lam_init):
    dh = ATTN_HEAD_DIM
    rs = ATTN_STRIP
    qi = pl.program_id(2)

    m_ref[...] = jnp.full(m_ref.shape, -jnp.inf, F32)
    l_ref[...] = jnp.zeros(l_ref.shape, F32)
    acc_ref[...] = jnp.zeros(acc_ref.shape, F32)

    def scores(kc, s_ref, mx_ref):
        k0 = pl.multiple_of(kc * tk, tk)
        start = pl.multiple_of(jnp.clip(kc * tk - qi * tq + 2 * tk, 0, tq + 3 * tk), LANES)
        for mi in range(2):
            kk = k_ref[pl.ds(k0, tk), mi * dh:(mi + 1) * dh]
            q = q_ref[:, mi * dh:(mi + 1) * dh]
            s = lax.dot_general(kk, q, NT_DIMS, preferred_element_type=F32) + band_ref[0, pl.ds(start, tk), :]
            s_ref[mi] = s
            mx = s[0:SUBLANES]
            for i in range(1, tk // SUBLANES):
                mx = jnp.maximum(mx, s[i * SUBLANES:(i + 1) * SUBLANES])
            mx_ref[mi] = mx

    def softmax_pv(kc, s_ref, mx_ref):
        k0 = pl.multiple_of(kc * tk, tk)
        for mi in range(2):
            m_prev = m_ref[mi]
            m_new = jnp.maximum(m_prev, jnp.max(mx_ref[mi], axis=0, keepdims=True))
            alpha = jnp.exp2(m_prev - m_new)

            def strip_exp(i, ls):
                p = jnp.exp2(s_ref[mi, pl.ds(i * rs, rs), :] - m_new)
                p_ref[mi, pl.ds(i * rs, rs), :] = p.astype(BF16)
                return ls + p

            ls = lax.fori_loop(0, tk // rs, strip_exp, jnp.zeros((rs, tq), F32), unroll=True)
            l_ref[mi] = alpha * l_ref[mi] + jnp.sum(ls, axis=0, keepdims=True)
            m_ref[mi] = m_new
            pv = jnp.dot(vt_ref[:, pl.ds(k0, tk)], p_ref[mi], preferred_element_type=F32)
            acc_ref[mi] = acc_ref[mi] * alpha + pv

    scores(0, sa_ref, mxa_ref)
    if nk > 1:
        assert nk % 2 == 0

        def pair(j, carry):
            scores(2 * j + 1, sb_ref, mxb_ref)
            softmax_pv(2 * j, sa_ref, mxa_ref)
            scores(2 * j + 2, sa_ref, mxa_ref)
            softmax_pv(2 * j + 1, sb_ref, mxb_ref)
            return carry

        lax.fori_loop(0, nk // 2 - 1, pair, 0)
        scores(nk - 1, sb_ref, mxb_ref)
        softmax_pv(nk - 2, sa_ref, mxa_ref)
        softmax_pv(nk - 1, sb_ref, mxb_ref)
    else:
        softmax_pv(0, sa_ref, mxa_ref)

    lam = (jnp.exp(jnp.sum(lq1_ref[...] * lk1_ref[...], axis=-1, keepdims=True))
           - jnp.exp(jnp.sum(lq2_ref[...] * lk2_ref[...], axis=-1, keepdims=True)) + lam_init)
    o = acc_ref[0] / l_ref[0] - lam * (acc_ref[1] / l_ref[1])
    ms = jnp.mean(o * o, axis=0, keepdims=True)
    o = (o * lax.rsqrt(ms + EPS) * subw_ref[...] * (1.0 - lam_init)).astype(BF16)
    o_ref[...] = lax.dot_general(eye_ref[...], o, NT_DIMS, preferred_element_type=F32).astype(o_ref.dtype)


def diff_attention(qk, vt, band, lq1, lk1, lq2, lk2, subw, *, batch, seq, heads, lam_init, tq, tk):
    dv = 2 * ATTN_HEAD_DIM
    nq, nk = seq // tq, seq // tk
    assert tk % ATTN_STRIP == 0
    vec = pl.BlockSpec((1, ATTN_HEAD_DIM), lambda b, h, i: (0, 0))
    body = functools.partial(_attn_body, tq=tq, tk=tk, nk=nk, lam_init=lam_init)
    eye = jnp.eye(tq, dtype=BF16)
    return pl.pallas_call(
        body,
        grid=(batch, heads, nq),
        in_specs=[
            pl.BlockSpec((tq, dv), lambda b, h, i: (b * nq + i, h)),
            pl.BlockSpec((seq, dv), lambda b, h, i: (b, heads + h)),
            pl.BlockSpec((dv, seq), lambda b, h, i: (h, b)),
            pl.BlockSpec((1, tq + 4 * tk, tq), lambda b, h, i: (h, 0, 0)),
            pl.BlockSpec((tq, tq), lambda b, h, i: (0, 0)),
            vec, vec, vec, vec,
            pl.BlockSpec((dv, 1), lambda b, h, i: (0, 0)),
        ],
        out_specs=pl.BlockSpec((tq, dv), lambda b, h, i: (b * nq + i, h)),
        out_shape=jax.ShapeDtypeStruct((batch * seq, heads * dv), BF16),
        scratch_shapes=[
            pltpu.VMEM((2, 1, tq), F32),
            pltpu.VMEM((2, 1, tq), F32),
            pltpu.VMEM((2, dv, tq), F32),
            pltpu.VMEM((2, tk, tq), F32),
            pltpu.VMEM((2, tk, tq), F32),
            pltpu.VMEM((2, SUBLANES, tq), F32),
            pltpu.VMEM((2, SUBLANES, tq), F32),
            pltpu.VMEM((2, tk, tq), BF16),
        ],
        compiler_params=_params("parallel", "parallel", "parallel"),
        name="diff_attention",
    )(qk, qk, vt, band, eye, lq1, lk1, lq2, lk2, subw)


def _ssd_body(*refs, reverse, final, heads, col_off):
    if final:
        (z_ref, act_in_ref, dt_ref, dtt_ref, dtb_ref, alog_ref, dtbt_ref, alogt_ref, dskip_ref, nw_ref, yb_ref, o_ref,
         state_ref) = refs
    else:
        (xbc_ref, hp_ref, hn_ref, dt_ref, dtt_ref, cw_ref, cb_ref, dtb_ref, alog_ref, dtbt_ref, alogt_ref, o_ref,
         act_out_ref, u_ref, act_ref, state_ref) = refs
    L = CHUNK
    assert SSM_STATE == CHUNK
    G, N, P = SSM_GROUPS, SSM_STATE, SSM_HEAD_DIM
    R = heads // G
    GW = R * P
    W = heads * P
    CC = W + 2 * G * N
    HALO = SUBLANES
    c = pl.program_id(1)
    nc = pl.num_programs(1)
    pos = (nc - 1 - c) if reverse else c

    @pl.when(c == 0)
    def _():
        state_ref[...] = jnp.zeros(state_ref.shape, F32)

    if final:
        def chan(lo, hi):
            return act_in_ref[:, lo:hi].astype(F32)
    else:
        prev = hp_ref[...].astype(F32)[BF16_ROWS - HALO:, :]
        nxt = hn_ref[...].astype(F32)[:HALO, :]
        u_ref[0:HALO, :] = jnp.where(pos > 0, prev, 0.0)
        u_ref[HALO:HALO + L, :] = xbc_ref[...].astype(F32)
        u_ref[HALO + L:, :] = jnp.where(pos < nc - 1, nxt, 0.0)
        SLAB = 2 * LANES

        def conv_slab(j, carry):
            lo = pl.multiple_of(j * SLAB, SLAB)
            acc = jnp.broadcast_to(cb_ref[:, pl.ds(lo, SLAB)], (L, SLAB))
            for t in range(CONV_WIDTH):
                r0 = HALO - CONV_WIDTH // 2 + t
                acc = acc + cw_ref[t:t + 1, pl.ds(lo, SLAB)] * u_ref[r0:r0 + L, pl.ds(lo, SLAB)]
            act = acc * _sigmoid(acc)
            act_ref[:, pl.ds(lo, SLAB)] = act
            act_out_ref[:, pl.ds(lo, SLAB)] = act.astype(act_out_ref.dtype)
            return carry

        lax.fori_loop(0, CC // SLAB, conv_slab, 0)

        def chan(lo, hi):
            return act_ref[:, lo:hi]

    def softplus(v):
        return jnp.maximum(v, 0.0) + jnp.log1p(jnp.exp(-jnp.abs(v)))

    dtv = softplus(dt_ref[...] + dtb_ref[...])
    a = -jnp.exp(alog_ref[...]) * dtv
    a_t = -jnp.exp(alogt_ref[...]) * softplus(dtt_ref[...] + dtbt_ref[...])
    row = lax.broadcasted_iota(jnp.int32, (L, L), 0)
    col = lax.broadcasted_iota(jnp.int32, (L, L), 1)
    keep = (row <= col) if reverse else (row >= col)
    keep_t = (row >= col) if reverse else (row <= col)
    cum = jnp.dot(keep.astype(F32), a, precision=lax.Precision.HIGHEST, preferred_element_type=F32)
    cum_t = jnp.dot(a_t, keep_t.astype(F32), precision=lax.Precision.HIGHEST, preferred_element_type=F32)
    total = cum[0:1, :] if reverse else cum[L - 1:L, :]
    eye_bf = jnp.where(row == col, 1.0, 0.0).astype(BF16)
    seg = lax.broadcasted_iota(jnp.int32, (L, GW), 1) // P
    seg1 = lax.broadcasted_iota(jnp.int32, (1, GW), 1) // P

    for g in range(G):
        xs = chan(g * GW, (g + 1) * GW)
        bmat = chan(W + g * N, W + (g + 1) * N)
        cmat = chan(W + (G + g) * N, W + (G + g + 1) * N)
        b_bf = bmat.astype(BF16)
        c_bf = cmat.astype(BF16)
        cb = lax.dot_general(c_bf, b_bf, NT_DIMS, preferred_element_type=F32)
        bt_bf = lax.dot_general(eye_bf, b_bf, NT_DIMS, preferred_element_type=F32).astype(BF16)
        dt_e = jnp.zeros((L, GW), F32)
        cum_e = jnp.zeros((L, GW), F32)
        tot_e = jnp.zeros((1, GW), F32)
        for r in range(R):
            hc = col_off + g * R + r
            dt_e = jnp.where(seg == r, dtv[:, hc:hc + 1], dt_e)
            cum_e = jnp.where(seg == r, cum[:, hc:hc + 1], cum_e)
            tot_e = jnp.where(seg1 == r, total[:, hc:hc + 1], tot_e)
        x_dt = xs * dt_e
        y = jnp.zeros((L, GW), F32)
        for r in range(R):
            hc = col_off + g * R + r
            decay = jnp.exp(jnp.where(keep, cum[:, hc:hc + 1] - cum_t[hc:hc + 1, :], -jnp.inf))
            m_h = (cb * decay).astype(BF16)
            x_h = jnp.where(seg == r, x_dt, 0.0).astype(BF16)
            y = y + jnp.dot(m_h, x_h, preferred_element_type=F32)
        s_prev = state_ref[g]
        y = y + jnp.dot(c_bf, s_prev.astype(BF16), preferred_element_type=F32) * jnp.exp(cum_e)
        x_end = (x_dt * jnp.exp(tot_e - cum_e)).astype(BF16)
        state_ref[g] = s_prev * jnp.exp(tot_e) + jnp.dot(bt_bf, x_end, preferred_element_type=F32)
        if final:
            y = y + yb_ref[:, g * GW:(g + 1) * GW].astype(F32) + dskip_ref[:, g * GW:(g + 1) * GW] * xs
            zg = z_ref[:, g * GW:(g + 1) * GW].astype(F32)
            y = y * (zg * _sigmoid(zg))
            ms = jnp.mean(y * y, axis=-1, keepdims=True)
            y = y * lax.rsqrt(ms + EPS) * nw_ref[:, g * GW:(g + 1) * GW]
        o_ref[:, g * GW:(g + 1) * GW] = y.astype(o_ref.dtype)


def ssd_pass(xbc, dt, dt_t, conv_w, conv_b, dt_bias, a_log, extra, *, batch, seq, heads, reverse, final, col_off):
    L = CHUNK
    W = heads * SSM_HEAD_DIM
    CC = W + 2 * SSM_GROUPS * SSM_STATE
    GW = W // SSM_GROUPS
    nc = seq // L
    nhalo = batch * seq // BF16_ROWS
    per = L // BF16_ROWS

    def rb(b, c):
        return b * nc + ((nc - 1 - c) if reverse else c)

    full = lambda shape: pl.BlockSpec(shape, lambda b, c: (0,) * len(shape))
    rows = lambda w: pl.BlockSpec((L, w), lambda b, c: (rb(b, c), 0))
    cols = pl.BlockSpec((LANES, L), lambda b, c: (0, rb(b, c)))
    state = pltpu.VMEM((SSM_GROUPS, SSM_STATE, GW), F32)
    body = functools.partial(_ssd_body, reverse=reverse, final=final, heads=heads, col_off=col_off)
    t = batch * seq
    head_specs = [full((1, LANES)), full((1, LANES)), full((LANES, 1)), full((LANES, 1))]
    head_args = [dt_bias, a_log, dt_bias.reshape(LANES, 1), a_log.reshape(LANES, 1)]
    if final:
        in_specs = [rows(W), rows(CC), rows(LANES), cols] + head_specs + [full((1, W)), full((1, W)), rows(W)]
        args = [extra["z"], extra["act"], dt, dt_t] + head_args + [extra["d_skip"], extra["norm_w"], extra["y_other"]]
        out_specs = rows(W)
        out_shape = jax.ShapeDtypeStruct((t, W), BF16)
        scratch = [state]
    else:
        in_specs = [
            rows(CC),
            pl.BlockSpec((BF16_ROWS, CC), lambda b, c: (jnp.maximum(rb(b, c) * per - 1, 0), 0)),
            pl.BlockSpec((BF16_ROWS, CC), lambda b, c: (jnp.minimum((rb(b, c) + 1) * per, nhalo - 1), 0)),
            rows(LANES), cols,
            full((CONV_WIDTH, CC)), full((1, CC)),
        ] + head_specs
        args = [xbc, xbc, xbc, dt, dt_t, conv_w, conv_b] + head_args
        out_specs = [rows(W), rows(CC)]
        out_shape = [jax.ShapeDtypeStruct((t, W), BF16), jax.ShapeDtypeStruct((t, CC), BF16)]
        scratch = [pltpu.VMEM((L + 2 * SUBLANES, CC), F32), pltpu.VMEM((L, CC), F32), state]
    return pl.pallas_call(
        body,
        grid=(batch, nc),
        in_specs=in_specs,
        out_specs=out_specs,
        out_shape=out_shape,
        scratch_shapes=scratch,
        compiler_params=_params("parallel", "arbitrary"),
        name="ssd_final" if final else "ssd_scan",
    )(*args)


def _router_body(x_ref, nw_ref, wr_ref, br_ref, h_ref, idx_ref, gate_ref, cnt_ref, run_ref):
    tm, d = x_ref.shape
    half = d // 2

    @pl.when(pl.program_id(0) == 0)
    def _():
        run_ref[...] = jnp.zeros(run_ref.shape, F32)

    x = x_ref[...]
    ms = jnp.mean(x * x, axis=-1, keepdims=True)
    h = x * lax.rsqrt(ms + EPS) * nw_ref[...]
    h_ref[...] = _pack_pairs(h[:, :half], h[:, half:])
    h_hi = h.astype(BF16)
    h_lo = (h - h_hi.astype(F32)).astype(BF16)
    hw = jnp.dot(h_hi, wr_ref[...], preferred_element_type=F32)
    logits = (hw[:, :LANES] + (hw[:, LANES:] + jnp.dot(h_lo, wr_ref[:, :LANES], preferred_element_type=F32))
              + br_ref[...])
    lane = lax.broadcasted_iota(jnp.int32, logits.shape, 1).astype(F32)
    ninf = -jnp.inf
    is_g = lane < N_EXPERT_GROUPS
    gl = jnp.where(is_g, logits, ninf)
    gmax = jnp.max(gl, axis=-1, keepdims=True)
    gsum = jnp.sum(jnp.where(is_g, jnp.exp(gl - gmax), 0.0), axis=-1, keepdims=True)
    p_group = 1.0 / gsum
    g_sel = jnp.min(jnp.where(gl == gmax, lane, float(LANES)), axis=-1, keepdims=True)
    lo = N_EXPERT_GROUPS + EXPERTS_PER_GROUP * g_sel
    el = jnp.where((lane >= lo) & (lane < lo + EXPERTS_PER_GROUP), logits, ninf)
    m1 = jnp.max(el, axis=-1, keepdims=True)
    i1 = jnp.min(jnp.where(el == m1, lane, float(LANES)), axis=-1, keepdims=True)
    el2 = jnp.where(lane == i1, ninf, el)
    m2 = jnp.max(el2, axis=-1, keepdims=True)
    i2 = jnp.min(jnp.where(el2 == m2, lane, float(LANES)), axis=-1, keepdims=True)
    r = jnp.exp(m2 - m1)
    g1 = p_group / (1.0 + r)
    g2 = p_group * r / (1.0 + r)
    e1 = i1 - N_EXPERT_GROUPS
    e2 = i2 - N_EXPERT_GROUPS
    oh1 = jnp.where(lane == e1, 1.0, 0.0)
    oh2 = jnp.where(lane == e2, 1.0, 0.0)
    both = oh1 + oh2
    ri = lax.broadcasted_iota(jnp.int32, (tm, tm), 0)
    ci = lax.broadcasted_iota(jnp.int32, (tm, tm), 1)
    earlier = jnp.where(ci < ri, 1.0, 0.0).astype(BF16)
    before = jnp.dot(earlier, both.astype(BF16), preferred_element_type=F32) + run_ref[...]
    rank1 = jnp.sum(oh1 * before, axis=-1, keepdims=True)
    rank2 = jnp.sum(oh2 * before, axis=-1, keepdims=True)
    run_ref[...] = run_ref[...] + jnp.sum(both, axis=0, keepdims=True)
    cnt_ref[...] = run_ref[...]
    packed = jnp.where(lane == 0, e1, jnp.where(lane == 1, e2, jnp.where(lane == 2, rank1, jnp.where(lane == 3, rank2, 0.0))))
    idx_ref[...] = packed.astype(jnp.int32)
    gate_ref[...] = jnp.where(lane == 0, g1, jnp.where(lane == 1, g2, 0.0))


def norm_and_route(x, norm_w, w_router, b_router, tm):
    t, d = x.shape
    row = lambda w: pl.BlockSpec((tm, w), lambda i: (i, 0))
    one = lambda w: pl.BlockSpec((1, w), lambda i: (0, 0))
    w_hi = w_router.astype(BF16)
    w_router = jnp.concatenate([w_hi, (w_router - w_hi.astype(F32)).astype(BF16)], axis=1)
    return pl.pallas_call(
        _router_body,
        grid=(t // tm,),
        in_specs=[row(d), one(d), pl.BlockSpec((d, 2 * LANES), lambda i: (0, 0)), one(LANES)],
        out_specs=[row(d // 2), row(LANES), row(LANES), one(LANES)],
        out_shape=[jax.ShapeDtypeStruct((t, d // 2), U32), jax.ShapeDtypeStruct((t, LANES), jnp.int32),
                   jax.ShapeDtypeStruct((t, LANES), F32), jax.ShapeDtypeStruct((1, LANES), F32)],
        scratch_shapes=[pltpu.VMEM((1, LANES), F32)],
        compiler_params=_params("arbitrary"),
        name="router",
    )(x, norm_w.reshape(1, d).astype(F32), w_router, b_router)


def _moe_plan(idx, counts, tm):
    t = idx.shape[0]
    n = t * TOP_K
    experts = jnp.arange(N_EXPERTS, dtype=I32)
    counts = counts[0, :N_EXPERTS].astype(I32)
    padded = (counts + tm - 1) // tm * tm
    pend = jnp.cumsum(padded)
    pstart = pend - padded
    e = idx[:, :TOP_K]
    rank = idx[:, TOP_K:2 * TOP_K]
    dest = rank + jnp.sum(jnp.where(e[:, :, None] == experts, pstart, 0), axis=-1)
    nblk = n // tm + N_EXPERTS
    n_used = pend[-1] // tm
    blk = jnp.arange(nblk, dtype=I32)
    blk_e = jnp.minimum(jnp.sum((pend[None, :] <= (blk * tm)[:, None]).astype(I32), axis=-1), N_EXPERTS - 1)
    last_e = jnp.sum(jnp.where(blk == n_used - 1, blk_e, 0))
    blk_e = jnp.where(blk < n_used, blk_e, last_e)
    later = (counts[None, :] > 0) & (experts[None, :] > experts[:, None])
    next_of = jnp.min(jnp.where(later, experts[None, :], N_EXPERTS), axis=1)
    next_of = jnp.where(next_of < N_EXPERTS, next_of, -1)
    blk_next = jnp.sum(jnp.where(blk_e[:, None] == experts, next_of[None, :], 0), axis=1)
    return (dest.astype(I32), pend.astype(I32), blk_e.astype(I32), blk_next.astype(I32),
            n_used.reshape(1).astype(I32), nblk * tm)


def _dispatch_body(pend_ref, nu_ref, dest_ref, h_ref, xs_hbm, idx_smem, zero_ref, sem, isem, *, tt, tm):
    i = pl.program_id(0)

    @pl.when(i == 0)
    def _():
        zero_ref[...] = jnp.zeros(zero_ref.shape, zero_ref.dtype)

        def zero_block(row0):
            cp = pltpu.make_async_copy(zero_ref, xs_hbm.at[pl.ds(pl.multiple_of(row0, tm), tm), :], sem)
            cp.start()
            cp.wait()

        def fill(e, c):
            end = pend_ref[e]
            prev = jnp.where(e > 0, pend_ref[jnp.maximum(e - 1, 0)], 0)

            @pl.when(end > prev)
            def _():
                zero_block(end - tm)
            return c

        lax.fori_loop(0, N_EXPERTS, fill, 0)

        def fill_unused(b, c):
            zero_block(b * tm)
            return c

        lax.fori_loop(nu_ref[0], xs_hbm.shape[0] // tm, fill_unused, 0)

    cp = pltpu.make_async_copy(dest_ref.at[0, 0], idx_smem, isem)
    cp.start()
    cp.wait()

    def issue(r, c):
        for k in range(TOP_K):
            d = idx_smem[TOP_K * r + k]
            pltpu.make_async_copy(h_ref.at[pl.ds(r, 1), :], xs_hbm.at[pl.ds(d, 1), :], sem).start(priority=k)
        return c

    lax.fori_loop(0, tt, issue, 0, unroll=8)
    all_rows = xs_hbm.at[pl.ds(0, TOP_K * tt), :]
    pltpu.make_async_copy(all_rows, all_rows, sem).wait()


def dispatch_rows(pend, n_used, dest, h_packed, rows, tt, tm):
    t, w = h_packed.shape
    return pl.pallas_call(
        functools.partial(_dispatch_body, tt=tt, tm=tm),
        grid_spec=pltpu.PrefetchScalarGridSpec(
            num_scalar_prefetch=2,
            grid=(t // tt,),
            in_specs=[
                pl.BlockSpec((1, 1, TOP_K * tt), lambda i, pe, nu: (i, 0, 0)),
                pl.BlockSpec((tt, w), lambda i, pe, nu: (i, 0)),
            ],
            out_specs=pl.BlockSpec(memory_space=pl.ANY),
            scratch_shapes=[
                pltpu.SMEM((TOP_K * tt,), I32),
                pltpu.VMEM((tm, w), U32),
                pltpu.SemaphoreType.DMA,
                pltpu.SemaphoreType.DMA,
            ],
        ),
        out_shape=jax.ShapeDtypeStruct((rows, w), U32),
        compiler_params=_params("arbitrary"),
        name="dispatch_rows",
    )(pend, n_used, dest.reshape(t // tt, 1, TOP_K * tt), h_packed)


def _expert_changed(be_ref, i):
    return (i == 0) | (be_ref[i] != be_ref[jnp.maximum(i - 1, 0)])


CAST_ROWS = 256


def _cast_rows(src_ref, dst_ref):
    def body(r, carry):
        rows = pl.ds(pl.multiple_of(r * CAST_ROWS, CAST_ROWS), CAST_ROWS)
        dst_ref[rows, :] = src_ref[rows, :].astype(dst_ref.dtype)
        return carry
    lax.fori_loop(0, src_ref.shape[0] // CAST_ROWS, body, 0)


def _stream_weights(be_ref, nx_ref, i, sweep, n_sweeps, copies, on_ready, cnt_ref):
    @pl.when((sweep == 0) & (i == 0))
    def _():
        cnt_ref[0] = 0
        for cp in copies(be_ref[0], 0, 0):
            cp.start()

    @pl.when(_expert_changed(be_ref, i))
    def _():
        slot = cnt_ref[0] & 1
        for cp in copies(be_ref[i], sweep, slot):
            cp.wait()
        on_ready(slot)
        nxt = nx_ref[i]

        @pl.when(nxt >= 0)
        def _():
            for cp in copies(nxt, sweep, 1 - slot):
                cp.start()

        @pl.when((nxt < 0) & (sweep + 1 < n_sweeps))
        def _():
            for cp in copies(be_ref[0], sweep + 1, 1 - slot):
                cp.start()

        cnt_ref[0] = cnt_ref[0] + 1


def _gate_up_body(be_ref, nx_ref, nu_ref, x_ref, wg_hbm, wu_hbm, o_ref, wbuf_ref, wgb_ref, wub_ref, sem_ref, cnt_ref,
                  *, tf):
    j = pl.program_id(0)
    i = pl.program_id(1)
    half = x_ref.shape[1]

    def copies(e, jj, slot):
        cols = pl.ds(pl.multiple_of(jj * tf, tf), tf)
        return (pltpu.make_async_copy(wg_hbm.at[e, :, cols], wbuf_ref.at[slot, 0], sem_ref.at[slot]),
                pltpu.make_async_copy(wu_hbm.at[e, :, cols], wbuf_ref.at[slot, 1], sem_ref.at[slot]))

    def on_ready(slot):
        _cast_rows(wbuf_ref.at[slot, 0], wgb_ref)
        _cast_rows(wbuf_ref.at[slot, 1], wub_ref)

    _stream_weights(be_ref, nx_ref, i, j, pl.num_programs(0), copies, on_ready, cnt_ref)

    @pl.when(i < nu_ref[0])
    def _():
        x_lo, x_hi = _unpack_pairs(x_ref[...])
        x_lo = x_lo.astype(BF16)
        x_hi = x_hi.astype(BF16)

        def mm(w_ref):
            return (jnp.dot(x_lo, w_ref[:half, :], preferred_element_type=F32)
                    + jnp.dot(x_hi, w_ref[half:, :], preferred_element_type=F32))

        a = mm(wgb_ref)
        b = mm(wub_ref)
        o_ref[...] = (a * _sigmoid(a) * b).astype(o_ref.dtype)

    @pl.when(i >= nu_ref[0])
    def _():
        o_ref[...] = jnp.zeros(o_ref.shape, o_ref.dtype)


def expert_gate_up(blk_e, blk_next, n_used, xs, w_gate, w_up, tm, tf):
    rows = xs.shape[0]
    d, f = w_gate.shape[1], w_gate.shape[2]
    nblk = rows // tm
    last = lambda i, nu: jnp.minimum(i, nu[0] - 1)
    hbm = pl.BlockSpec(memory_space=pl.ANY)
    return pl.pallas_call(
        functools.partial(_gate_up_body, tf=tf),
        grid_spec=pltpu.PrefetchScalarGridSpec(
            num_scalar_prefetch=3,
            grid=(f // tf, nblk),
            in_specs=[pl.BlockSpec((tm, d // 2), lambda j, i, be, nx, nu: (last(i, nu), 0)), hbm, hbm],
            out_specs=pl.BlockSpec((tm, tf), lambda j, i, be, nx, nu: (i, j)),
            scratch_shapes=[
                pltpu.VMEM((2, 2, d, tf), F32),
                pltpu.VMEM((d, tf), BF16),
                pltpu.VMEM((d, tf), BF16),
                pltpu.SemaphoreType.DMA((2,)),
                pltpu.SMEM((1,), I32),
            ],
        ),
        out_shape=jax.ShapeDtypeStruct((rows, f), BF16),
        compiler_params=_params("arbitrary", "arbitrary"),
        name="expert_gate_up",
    )(blk_e, blk_next, n_used, xs, w_gate, w_up)


def _down_body(be_ref, nx_ref, nu_ref, a_ref, wd_hbm, o_ref, wbuf_ref, wdb_ref, sem_ref, cnt_ref):
    i = pl.program_id(0)
    half = o_ref.shape[1]

    def copies(e, sweep, slot):
        return (pltpu.make_async_copy(wd_hbm.at[e], wbuf_ref.at[slot], sem_ref.at[slot]),)

    def on_ready(slot):
        _cast_rows(wbuf_ref.at[slot], wdb_ref)

    _stream_weights(be_ref, nx_ref, i, 0, 1, copies, on_ready, cnt_ref)

    @pl.when(i < nu_ref[0])
    def _():
        y = jnp.dot(a_ref[...], wdb_ref[...], preferred_element_type=F32)
        o_ref[...] = _pack_pairs(y[:, :half], y[:, half:])

    @pl.when(i >= nu_ref[0])
    def _():
        o_ref[...] = jnp.zeros(o_ref.shape, o_ref.dtype)


def expert_down(blk_e, blk_next, n_used, act, w_down, tm):
    rows, f = act.shape
    d = w_down.shape[2]
    nblk = rows // tm
    last = lambda i, nu: jnp.minimum(i, nu[0] - 1)
    return pl.pallas_call(
        _down_body,
        grid_spec=pltpu.PrefetchScalarGridSpec(
            num_scalar_prefetch=3,
            grid=(nblk,),
            in_specs=[
                pl.BlockSpec((tm, f), lambda i, be, nx, nu: (last(i, nu), 0)),
                pl.BlockSpec(memory_space=pl.ANY),
            ],
            out_specs=pl.BlockSpec((tm, d // 2), lambda i, be, nx, nu: (i, 0)),
            scratch_shapes=[
                pltpu.VMEM((2, f, d), F32),
                pltpu.VMEM((f, d), BF16),
                pltpu.SemaphoreType.DMA((2,)),
                pltpu.SMEM((1,), I32),
            ],
        ),
        out_shape=jax.ShapeDtypeStruct((rows, d // 2), U32),
        compiler_params=_params("arbitrary"),
        name="expert_down",
    )(blk_e, blk_next, n_used, act, w_down)


def _combine_body(dest_ref, x_ref, g_ref, w_ref, y_hbm, o_ref, idx_smem, buf_ref, sem, isem, *, tt):
    d = x_ref.shape[1]
    half = d // 2
    cp = pltpu.make_async_copy(dest_ref.at[0, 0], idx_smem, isem)
    cp.start()
    cp.wait()

    def issue(r, c):
        for k in range(TOP_K):
            src = idx_smem[TOP_K * r + k]
            pltpu.make_async_copy(y_hbm.at[pl.ds(src, 1), :], buf_ref.at[k, pl.ds(r, 1), :], sem).start(priority=k)
        return c

    lax.fori_loop(0, tt, issue, 0, unroll=8)
    pltpu.make_async_copy(buf_ref, buf_ref, sem).wait()

    y0_lo, y0_hi = _unpack_pairs(buf_ref[0])
    y1_lo, y1_hi = _unpack_pairs(buf_ref[1])
    g0 = g_ref[:, 0:1]
    g1 = g_ref[:, 1:2]
    x_lo = x_ref[:, :half] + (g0 * y0_lo + g1 * y1_lo)
    x_hi = x_ref[:, half:] + (g0 * y0_hi + g1 * y1_hi)
    ms = (jnp.sum(x_lo * x_lo, axis=-1, keepdims=True) + jnp.sum(x_hi * x_hi, axis=-1, keepdims=True)) / d
    inv = lax.rsqrt(ms + EPS)
    o_ref[:, :half] = x_lo * inv * w_ref[:, :half]
    o_ref[:, half:] = x_hi * inv * w_ref[:, half:]


def combine_and_norm(x, y_packed, dest, gate, w, tt):
    t, d = x.shape
    row = lambda wd: pl.BlockSpec((tt, wd), lambda i: (i, 0))
    return pl.pallas_call(
        functools.partial(_combine_body, tt=tt),
        grid=(t // tt,),
        in_specs=[
            pl.BlockSpec((1, 1, TOP_K * tt), lambda i: (i, 0, 0)),
            row(d), row(LANES), pl.BlockSpec((1, d), lambda i: (0, 0)),
            pl.BlockSpec(memory_space=pl.ANY),
        ],
        out_specs=row(d),
        out_shape=jax.ShapeDtypeStruct((t, d), F32),
        scratch_shapes=[
            pltpu.SMEM((TOP_K * tt,), I32),
            pltpu.VMEM((TOP_K, tt, d // 2), U32),
            pltpu.SemaphoreType.DMA,
            pltpu.SemaphoreType.DMA,
        ],
        compiler_params=_params("arbitrary"),
        name="combine_norm",
    )(dest.reshape(t // tt, 1, TOP_K * tt), x, gate, w.reshape(1, d).astype(F32), y_packed)


def _tiles(batch, seq):
    t = batch * seq
    return dict(
        norm_tm=min(256, t),
        mm_tm=min(1024, t),
        mm_tn=1024,
        out_tm=min(256, t),
        out_tn=2048,
        attn_tq=min(1024, seq),
        attn_tk=min(512, seq),
        moe_tm=min(256, t),
        moe_tf=512,
        disp_tt=min(1024, t),
        comb_tt=min(512, t),
    )


def _lambda_init_at(layer):
    return 0.8 - 0.6 * math.exp(-0.3 * layer)


def kernel(x, rel_bias, norm1_w, w_in, lambda_q1, lambda_k1, lambda_q2, lambda_k2, subln_w, conv_w, conv_b,
           dt_bias_f, dt_bias_b, a_log_f, a_log_b, d_skip, ssm_norm_w, w_out, norm2_w, w_group_router,
           b_group_router, w_expert_router, b_expert_router, w_gate, w_up, w_down, final_norm_w):
    batch, seq, d = x.shape
    t = batch * seq
    depth = norm1_w.shape[0]
    attn_w = d // 2
    ssm_w = d - attn_w
    dv = 2 * ATTN_HEAD_DIM
    a_heads = attn_w // dv
    s_heads = ssm_w // SSM_HEAD_DIM
    cc = ssm_w + 2 * SSM_GROUPS * SSM_STATE
    main_w = 3 * attn_w + ssm_w + cc
    assert ssm_w == attn_w and cc == 2 * ssm_w and 2 * s_heads <= LANES
    tl = _tiles(batch, seq)
    log2e = math.log2(math.e)
    band = _band_table(rel_bias, tl["attn_tq"], tl["attn_tk"], log2e)
    q_scale = jnp.where(jnp.arange(w_in.shape[2]) < attn_w, ATTN_HEAD_DIM ** -0.5 * log2e, 1.0).astype(F32)

    def pad_lanes(v):
        return jnp.pad(v.astype(F32), (0, LANES - v.shape[0])).reshape(1, LANES)

    xf = x.reshape(t, d)
    for layer in range(depth):
        lam_init = _lambda_init_at(layer)
        w_t = (jnp.transpose(w_in[layer]) * q_scale[:, None]).astype(BF16)
        w_dt = jnp.pad(w_t[main_w:], ((0, LANES - 2 * s_heads), (0, 0)))
        h = rmsnorm_rows(xf, norm1_w[layer], BF16, tl["norm_tm"])
        tm, tn = tl["mm_tm"], tl["mm_tn"]
        qk = matmul_nt(h, w_t, BF16, t, 2 * attn_w, tm, tn, "in_proj_qk")
        v_t = matmul_nt(w_t, h, BF16, attn_w, t, tn, tm, "in_proj_v", a_blk0=2 * attn_w // tn)
        z, dt, dt_t = projection_with_dt(h, w_t, w_dt, BF16, t, ssm_w, tm, tn, "in_proj_z_dt",
                                         b_blk0=3 * attn_w // tn)
        xbc = matmul_nt(h, w_t, BF16, t, cc, tm, tn, "in_proj_xbc", b_blk0=(3 * attn_w + ssm_w) // tn)

        vec = lambda v: v[layer].reshape(1, -1).astype(F32)
        attn = diff_attention(qk, v_t, band, vec(lambda_q1), vec(lambda_k1), vec(lambda_q2), vec(lambda_k2),
                              subln_w[layer].reshape(dv, 1).astype(F32), batch=batch, seq=seq, heads=a_heads,
                              lam_init=lam_init, tq=tl["attn_tq"], tk=tl["attn_tk"])

        dt_bias = pad_lanes(jnp.concatenate([dt_bias_f[layer], dt_bias_b[layer]]))
        a_log = pad_lanes(jnp.concatenate([a_log_f[layer], a_log_b[layer]]))
        cw = conv_w[layer].astype(F32)
        cb = conv_b[layer].reshape(1, cc).astype(F32)
        common = dict(batch=batch, seq=seq, heads=s_heads)
        y_bwd, act = ssd_pass(xbc, dt, dt_t, cw, cb, dt_bias, a_log, None, reverse=True, final=False,
                              col_off=s_heads, **common)
        extra = dict(d_skip=jnp.repeat(d_skip[layer].astype(F32), SSM_HEAD_DIM).reshape(1, ssm_w),
                     norm_w=vec(ssm_norm_w), y_other=y_bwd, act=act, z=z)
        ssm = ssd_pass(None, dt, dt_t, cw, cb, dt_bias, a_log, extra, reverse=False, final=True, col_off=0, **common)

        x1 = out_projection(attn, ssm, w_out[layer].astype(BF16), xf, tl["out_tm"], tl["out_tn"])

        w_router = jnp.pad(jnp.concatenate([w_group_router[layer], w_expert_router[layer]], axis=1).astype(F32),
                           ((0, 0), (0, LANES - N_EXPERT_GROUPS - N_EXPERTS)))
        b_router = pad_lanes(jnp.concatenate([b_group_router[layer], b_expert_router[layer]]))
        h2, idx, gate, counts = norm_and_route(x1, norm2_w[layer], w_router, b_router, tl["norm_tm"])
        tm = tl["moe_tm"]
        dest, pend, blk_e, blk_next, n_used, rows = _moe_plan(idx, counts, tm)
        xs = dispatch_rows(pend, n_used, dest, h2, rows, tl["disp_tt"], tm)
        act = expert_gate_up(blk_e, blk_next, n_used, xs, w_gate[layer], w_up[layer], tm, tl["moe_tf"])
        y = expert_down(blk_e, blk_next, n_used, act, w_down[layer], tm)
        if layer + 1 < depth:
            raise NotImplementedError("multi-layer stacking needs an un-normalised combine")
        xf = combine_and_norm(x1, y, dest, gate, final_norm_w, tl["comb_tt"])
    return xf.reshape(batch, seq, d)
```

```python
import functools
import math

import jax
import jax.numpy as jnp
from jax import lax
from jax.experimental import pallas as pl
from jax.experimental.pallas import tpu as pltpu

F32 = jnp.float32
BF16 = jnp.bfloat16
U32 = jnp.uint32
I32 = jnp.int32
EPS = 1e-6

ATTN_HEAD_DIM = 128
NUM_BUCKETS = 32
MAX_DISTANCE = 128
SSM_HEAD_DIM = 64
SSM_GROUPS = 8
SSM_STATE = 128
CONV_WIDTH = 5
CHUNK = 128
N_EXPERT_GROUPS = 4
EXPERTS_PER_GROUP = 8
N_EXPERTS = N_EXPERT_GROUPS * EXPERTS_PER_GROUP
TOP_K = 2

LANES = 128
SUBLANES = 8
BF16_ROWS = 16
VMEM_LIMIT = 56 * 1024 * 1024
HI16 = 0xFFFF0000


def _params(*sem):
    return pltpu.CompilerParams(dimension_semantics=sem, vmem_limit_bytes=VMEM_LIMIT)


def _sigmoid(x):
    return 1.0 / (1.0 + jnp.exp(-x))


def _pack_pairs(lo, hi):
    lo_b = lax.bitcast_convert_type(lo.astype(BF16).astype(F32), U32)
    hi_b = lax.bitcast_convert_type(hi.astype(BF16).astype(F32), U32)
    return (lo_b >> 16) | (hi_b & U32(HI16))


def _unpack_pairs(w):
    return lax.bitcast_convert_type(w << 16, F32), lax.bitcast_convert_type(w & U32(HI16), F32)


def _rmsnorm_body(x_ref, w_ref, o_ref):
    x = x_ref[...]
    ms = jnp.mean(x * x, axis=-1, keepdims=True)
    o_ref[...] = (x * lax.rsqrt(ms + EPS) * w_ref[...]).astype(o_ref.dtype)


def rmsnorm_rows(x, w, out_dtype, tm):
    t, d = x.shape
    return pl.pallas_call(
        _rmsnorm_body,
        grid=(t // tm,),
        in_specs=[pl.BlockSpec((tm, d), lambda i: (i, 0)), pl.BlockSpec((1, d), lambda i: (0, 0))],
        out_specs=pl.BlockSpec((tm, d), lambda i: (i, 0)),
        out_shape=jax.ShapeDtypeStruct((t, d), out_dtype),
        compiler_params=_params("parallel"),
        name="rmsnorm",
    )(x, w.reshape(1, d).astype(F32))


NT_DIMS = (((1,), (1,)), ((), ()))


def _matmul_nt_body(a_ref, b_ref, o_ref):
    o_ref[...] = lax.dot_general(a_ref[...], b_ref[...], NT_DIMS, preferred_element_type=F32).astype(o_ref.dtype)


def matmul_nt(a, b, out_dtype, m, n, tm, tn, name, a_blk0=0, b_blk0=0):
    k = a.shape[1]
    assert b.shape[1] == k and m % tm == 0 and n % tn == 0
    return pl.pallas_call(
        _matmul_nt_body,
        grid=(m // tm, n // tn),
        in_specs=[pl.BlockSpec((tm, k), lambda i, j: (a_blk0 + i, 0)),
                  pl.BlockSpec((tn, k), lambda i, j: (b_blk0 + j, 0))],
        out_specs=pl.BlockSpec((tm, tn), lambda i, j: (i, j)),
        out_shape=jax.ShapeDtypeStruct((m, n), out_dtype),
        compiler_params=_params("parallel", "parallel"),
        name=name,
    )(a, b)


def _proj_with_dt_body(a_ref, b_ref, wdt_ref, o_ref, dt_ref, dtt_ref):
    o_ref[...] = lax.dot_general(a_ref[...], b_ref[...], NT_DIMS, preferred_element_type=F32).astype(o_ref.dtype)

    @pl.when(pl.program_id(1) == 0)
    def _():
        dt = lax.dot_general(a_ref[...], wdt_ref[...], NT_DIMS, preferred_element_type=F32)
        dt_ref[...] = dt
        dtt_ref[...] = dt.T


def projection_with_dt(a, b, w_dt, out_dtype, m, n, tm, tn, name, b_blk0):
    k = a.shape[1]
    nd = w_dt.shape[0]
    return pl.pallas_call(
        _proj_with_dt_body,
        grid=(m // tm, n // tn),
        in_specs=[pl.BlockSpec((tm, k), lambda i, j: (i, 0)),
                  pl.BlockSpec((tn, k), lambda i, j: (b_blk0 + j, 0)),
                  pl.BlockSpec((nd, k), lambda i, j: (0, 0))],
        out_specs=[pl.BlockSpec((tm, tn), lambda i, j: (i, j)),
                   pl.BlockSpec((tm, nd), lambda i, j: (i, 0)),
                   pl.BlockSpec((nd, tm), lambda i, j: (0, i))],
        out_shape=[jax.ShapeDtypeStruct((m, n), out_dtype), jax.ShapeDtypeStruct((m, nd), F32),
                   jax.ShapeDtypeStruct((nd, m), F32)],
        compiler_params=_params("parallel", "arbitrary"),
        name=name,
    )(a, b, w_dt)


def _outproj_body(a_ref, s_ref, wa_ref, ws_ref, r_ref, o_ref):
    acc = jnp.dot(a_ref[...], wa_ref[...], preferred_element_type=F32)
    acc = acc + jnp.dot(s_ref[...], ws_ref[...], preferred_element_type=F32)
    o_ref[...] = r_ref[...] + acc


def out_projection(attn, ssm, w, resid, tm, tn):
    m, ka = attn.shape
    ks = ssm.shape[1]
    assert ka == ks
    n = w.shape[1]
    return pl.pallas_call(
        _outproj_body,
        grid=(n // tn, m // tm),
        in_specs=[
            pl.BlockSpec((tm, ka), lambda j, i: (i, 0)),
            pl.BlockSpec((tm, ks), lambda j, i: (i, 0)),
            pl.BlockSpec((ka, tn), lambda j, i: (0, j)),
            pl.BlockSpec((ks, tn), lambda j, i: (1, j)),
            pl.BlockSpec((tm, tn), lambda j, i: (i, j)),
        ],
        out_specs=pl.BlockSpec((tm, tn), lambda j, i: (i, j)),
        out_shape=jax.ShapeDtypeStruct((m, n), F32),
        compiler_params=_params("parallel", "parallel"),
        name="out_proj",
    )(attn, ssm, w, w, resid)


def _t5_bucket(rel):
    half = NUM_BUCKETS // 2
    max_exact = half // 2
    n = jnp.abs(rel)
    large = max_exact + (
        jnp.log(jnp.maximum(n, 1).astype(F32) / max_exact) / math.log(MAX_DISTANCE / max_exact) * (half - max_exact)
    ).astype(jnp.int32)
    large = jnp.minimum(large, half - 1)
    return jnp.where(rel > 0, half, 0) + jnp.where(n < max_exact, n, large)


def _band_body(v_ref, o_ref, *, tq, w):
    npad = v_ref.shape[-1]
    x = jnp.broadcast_to(v_ref[0], (tq, npad))
    y = pltpu.roll(x, npad - (tq - 1), 1, stride=1, stride_axis=0)
    o_ref[0] = y[:, :w].T


def _band_table(rel_bias, tq, tk, scale):
    assert tk + 1 >= MAX_DISTANCE
    heads = rel_bias.shape[1]
    w = tq + 4 * tk
    n = w + tq - 1
    npad = -(-n // LANES) * LANES
    rel = jnp.arange(npad, dtype=jnp.int32) - (2 * tk + tq - 1)
    v = (rel_bias[_t5_bucket(rel)].astype(F32) * scale).T.reshape(heads, 1, npad)
    return pl.pallas_call(
        functools.partial(_band_body, tq=tq, w=w),
        grid=(heads,),
        in_specs=[pl.BlockSpec((1, 1, npad), lambda h: (h, 0, 0))],
        out_specs=pl.BlockSpec((1, w, tq), lambda h: (h, 0, 0)),
        out_shape=jax.ShapeDtypeStruct((heads, w, tq), F32),
        compiler_params=_params("parallel"),
        name="band_table",
    )(v)


ATTN_STRIP = 16


def _attn_body(q_ref, k_ref, vt_ref, band_ref, eye_ref, lq1_ref, lk1_ref, lq2_ref, lk2_ref, subw_ref, o_ref,
               m_ref, l_ref, acc_ref, sa_ref, sb_ref, mxa_ref, mxb_ref, p_ref, *, tq, tk, nk, lam_init):
    dh = ATTN_HEAD_DIM
    rs = ATTN_STRIP
    qi = pl.program_id(2)

    m_ref[...] = jnp.full(m_ref.shape, -jnp.inf, F32)
    l_ref[...] = jnp.zeros(l_ref.shape, F32)
    acc_ref[...] = jnp.zeros(acc_ref.shape, F32)

    def scores(kc, s_ref, mx_ref):
        k0 = pl.multiple_of(kc * tk, tk)
        start = pl.multiple_of(jnp.clip(kc * tk - qi * tq + 2 * tk, 0, tq + 3 * tk), LANES)
        for mi in range(2):
            kk = k_ref[pl.ds(k0, tk), mi * dh:(mi + 1) * dh]
            q = q_ref[:, mi * dh:(mi + 1) * dh]
            s = lax.dot_general(kk, q, NT_DIMS, preferred_element_type=F32) + band_ref[0, pl.ds(start, tk), :]
            s_ref[mi] = s
            mx = s[0:SUBLANES]
            for i in range(1, tk // SUBLANES):
                mx = jnp.maximum(mx, s[i * SUBLANES:(i + 1) * SUBLANES])
            mx_ref[mi] = mx

    def softmax_pv(kc, s_ref, mx_ref):
        k0 = pl.multiple_of(kc * tk, tk)
        for mi in range(2):
            m_prev = m_ref[mi]
            m_new = jnp.maximum(m_prev, jnp.max(mx_ref[mi], axis=0, keepdims=True))
            alpha = jnp.exp2(m_prev - m_new)

            def strip_exp(i, ls):
                p = jnp.exp2(s_ref[mi, pl.ds(i * rs, rs), :] - m_new)
                p_ref[mi, pl.ds(i * rs, rs), :] = p.astype(BF16)
                return ls + p

            ls = lax.fori_loop(0, tk // rs, strip_exp, jnp.zeros((rs, tq), F32), unroll=True)
            l_ref[mi] = alpha * l_ref[mi] + jnp.sum(ls, axis=0, keepdims=True)
            m_ref[mi] = m_new
            pv = jnp.dot(vt_ref[:, pl.ds(k0, tk)], p_ref[mi], preferred_element_type=F32)
            acc_ref[mi] = acc_ref[mi] * alpha + pv

    scores(0, sa_ref, mxa_ref)
    if nk > 1:
        assert nk % 2 == 0

        def pair(j, carry):
            scores(2 * j + 1, sb_ref, mxb_ref)
            softmax_pv(2 * j, sa_ref, mxa_ref)
            scores(2 * j + 2, sa_ref, mxa_ref)
            softmax_pv(2 * j + 1, sb_ref, mxb_ref)
            return carry

        lax.fori_loop(0, nk // 2 - 1, pair, 0)
        scores(nk - 1, sb_ref, mxb_ref)
        softmax_pv(nk - 2, sa_ref, mxa_ref)
        softmax_pv(nk - 1, sb_ref, mxb_ref)
    else:
        softmax_pv(0, sa_ref, mxa_ref)

    lam = (jnp.exp(jnp.sum(lq1_ref[...] * lk1_ref[...], axis=-1, keepdims=True))
           - jnp.exp(jnp.sum(lq2_ref[...] * lk2_ref[...], axis=-1, keepdims=True)) + lam_init)
    o = acc_ref[0] / l_ref[0] - lam * (acc_ref[1] / l_ref[1])
    ms = jnp.mean(o * o, axis=0, keepdims=True)
    o = (o * lax.rsqrt(ms + EPS) * subw_ref[...] * (1.0 - lam_init)).astype(BF16)
    o_ref[...] = lax.dot_general(eye_ref[...], o, NT_DIMS, preferred_element_type=F32).astype(o_ref.dtype)


def diff_attention(qk, vt, band, lq1, lk1, lq2, lk2, subw, *, batch, seq, heads, lam_init, tq, tk):
    dv = 2 * ATTN_HEAD_DIM
    nq, nk = seq // tq, seq // tk
    assert tk % ATTN_STRIP == 0
    vec = pl.BlockSpec((1, ATTN_HEAD_DIM), lambda b, h, i: (0, 0))
    body = functools.partial(_attn_body, tq=tq, tk=tk, nk=nk, lam_init=lam_init)
    eye = jnp.eye(tq, dtype=BF16)
    return pl.pallas_call(
        body,
        grid=(batch, heads, nq),
        in_specs=[
            pl.BlockSpec((tq, dv), lambda b, h, i: (b * nq + i, h)),
            pl.BlockSpec((seq, dv), lambda b, h, i: (b, heads + h)),
            pl.BlockSpec((dv, seq), lambda b, h, i: (h, b)),
            pl.BlockSpec((1, tq + 4 * tk, tq), lambda b, h, i: (h, 0, 0)),
            pl.BlockSpec((tq, tq), lambda b, h, i: (0, 0)),
            vec, vec, vec, vec,
            pl.BlockSpec((dv, 1), lambda b, h, i: (0, 0)),
        ],
        out_specs=pl.BlockSpec((tq, dv), lambda b, h, i: (b * nq + i, h)),
        out_shape=jax.ShapeDtypeStruct((batch * seq, heads * dv), BF16),
        scratch_shapes=[
            pltpu.VMEM((2, 1, tq), F32),
            pltpu.VMEM((2, 1, tq), F32),
            pltpu.VMEM((2, dv, tq), F32),
            pltpu.VMEM((2, tk, tq), F32),
            pltpu.VMEM((2, tk, tq), F32),
            pltpu.VMEM((2, SUBLANES, tq), F32),
            pltpu.VMEM((2, SUBLANES, tq), F32),
            pltpu.VMEM((2, tk, tq), BF16),
        ],
        compiler_params=_params("parallel", "parallel", "parallel"),
        name="diff_attention",
    )(qk, qk, vt, band, eye, lq1, lk1, lq2, lk2, subw)


def _ssd_body(*refs, reverse, final, heads, col_off):
    if final:
        (z_ref, act_in_ref, dt_ref, dtt_ref, dtb_ref, alog_ref, dtbt_ref, alogt_ref, dskip_ref, nw_ref, yb_ref, o_ref,
         state_ref) = refs
    else:
        (xbc_ref, hp_ref, hn_ref, dt_ref, dtt_ref, cw_ref, cb_ref, dtb_ref, alog_ref, dtbt_ref, alogt_ref, o_ref,
         act_out_ref, u_ref, act_ref, state_ref) = refs
    L = CHUNK
    assert SSM_STATE == CHUNK
    G, N, P = SSM_GROUPS, SSM_STATE, SSM_HEAD_DIM
    R = heads // G
    GW = R * P
    W = heads * P
    CC = W + 2 * G * N
    HALO = SUBLANES
    c = pl.program_id(1)
    nc = pl.num_programs(1)
    pos = (nc - 1 - c) if reverse else c

    @pl.when(c == 0)
    def _():
        state_ref[...] = jnp.zeros(state_ref.shape, F32)

    if final:
        def chan(lo, hi):
            return act_in_ref[:, lo:hi].astype(F32)
    else:
        prev = hp_ref[...].astype(F32)[BF16_ROWS - HALO:, :]
        nxt = hn_ref[...].astype(F32)[:HALO, :]
        u_ref[0:HALO, :] = jnp.where(pos > 0, prev, 0.0)
        u_ref[HALO:HALO + L, :] = xbc_ref[...].astype(F32)
        u_ref[HALO + L:, :] = jnp.where(pos < nc - 1, nxt, 0.0)
        SLAB = 2 * LANES

        def conv_slab(j, carry):
            lo = pl.multiple_of(j * SLAB, SLAB)
            acc = jnp.broadcast_to(cb_ref[:, pl.ds(lo, SLAB)], (L, SLAB))
            for t in range(CONV_WIDTH):
                r0 = HALO - CONV_WIDTH // 2 + t
                acc = acc + cw_ref[t:t + 1, pl.ds(lo, SLAB)] * u_ref[r0:r0 + L, pl.ds(lo, SLAB)]
            act = acc * _sigmoid(acc)
            act_ref[:, pl.ds(lo, SLAB)] = act
            act_out_ref[:, pl.ds(lo, SLAB)] = act.astype(act_out_ref.dtype)
            return carry

        lax.fori_loop(0, CC // SLAB, conv_slab, 0)

        def chan(lo, hi):
            return act_ref[:, lo:hi]

    def softplus(v):
        return jnp.maximum(v, 0.0) + jnp.log1p(jnp.exp(-jnp.abs(v)))

    dtv = softplus(dt_ref[...] + dtb_ref[...])
    a = -jnp.exp(alog_ref[...]) * dtv
    a_t = -jnp.exp(alogt_ref[...]) * softplus(dtt_ref[...] + dtbt_ref[...])
    row = lax.broadcasted_iota(jnp.int32, (L, L), 0)
    col = lax.broadcasted_iota(jnp.int32, (L, L), 1)
    keep = (row <= col) if reverse else (row >= col)
    keep_t = (row >= col) if reverse else (row <= col)
    cum = jnp.dot(keep.astype(F32), a, precision=lax.Precision.HIGHEST, preferred_element_type=F32)
    cum_t = jnp.dot(a_t, keep_t.astype(F32), precision=lax.Precision.HIGHEST, preferred_element_type=F32)
    total = cum[0:1, :] if reverse else cum[L - 1:L, :]
    eye_bf = jnp.where(row == col, 1.0, 0.0).astype(BF16)
    seg = lax.broadcasted_iota(jnp.int32, (L, GW), 1) // P
    seg1 = lax.broadcasted_iota(jnp.int32, (1, GW), 1) // P

    for g in range(G):
        xs = chan(g * GW, (g + 1) * GW)
        bmat = chan(W + g * N, W + (g + 1) * N)
        cmat = chan(W + (G + g) * N, W + (G + g + 1) * N)
        b_bf = bmat.astype(BF16)
        c_bf = cmat.astype(BF16)
        cb = lax.dot_general(c_bf, b_bf, NT_DIMS, preferred_element_type=F32)
        bt_bf = lax.dot_general(eye_bf, b_bf, NT_DIMS, preferred_element_type=F32).astype(BF16)
        dt_e = jnp.zeros((L, GW), F32)
        cum_e = jnp.zeros((L, GW), F32)
        tot_e = jnp.zeros((1, GW), F32)
        for r in range(R):
            hc = col_off + g * R + r
            dt_e = jnp.where(seg == r, dtv[:, hc:hc + 1], dt_e)
            cum_e = jnp.where(seg == r, cum[:, hc:hc + 1], cum_e)
            tot_e = jnp.where(seg1 == r, total[:, hc:hc + 1], tot_e)
        x_dt = xs * dt_e
        y = jnp.zeros((L, GW), F32)
        for r in range(R):
            hc = col_off + g * R + r
            decay = jnp.exp(jnp.where(keep, cum[:, hc:hc + 1] - cum_t[hc:hc + 1, :], -jnp.inf))
            m_h = (cb * decay).astype(BF16)
            x_h = jnp.where(seg == r, x_dt, 0.0).astype(BF16)
            y = y + jnp.dot(m_h, x_h, preferred_element_type=F32)
        s_prev = state_ref[g]
        y = y + jnp.dot(c_bf, s_prev.astype(BF16), preferred_element_type=F32) * jnp.exp(cum_e)
        x_end = (x_dt * jnp.exp(tot_e - cum_e)).astype(BF16)
        state_ref[g] = s_prev * jnp.exp(tot_e) + jnp.dot(bt_bf, x_end, preferred_element_type=F32)
        if final:
            y = y + yb_ref[:, g * GW:(g + 1) * GW].astype(F32) + dskip_ref[:, g * GW:(g + 1) * GW] * xs
            zg = z_ref[:, g * GW:(g + 1) * GW].astype(F32)
            y = y * (zg * _sigmoid(zg))
            ms = jnp.mean(y * y, axis=-1, keepdims=True)
            y = y * lax.rsqrt(ms + EPS) * nw_ref[:, g * GW:(g + 1) * GW]
        o_ref[:, g * GW:(g + 1) * GW] = y.astype(o_ref.dtype)


def ssd_pass(xbc, dt, dt_t, conv_w, conv_b, dt_bias, a_log, extra, *, batch, seq, heads, reverse, final, col_off):
    L = CHUNK
    W = heads * SSM_HEAD_DIM
    CC = W + 2 * SSM_GROUPS * SSM_STATE
    GW = W // SSM_GROUPS
    nc = seq // L
    nhalo = batch * seq // BF16_ROWS
    per = L // BF16_ROWS

    def rb(b, c):
        return b * nc + ((nc - 1 - c) if reverse else c)

    full = lambda shape: pl.BlockSpec(shape, lambda b, c: (0,) * len(shape))
    rows = lambda w: pl.BlockSpec((L, w), lambda b, c: (rb(b, c), 0))
    cols = pl.BlockSpec((LANES, L), lambda b, c: (0, rb(b, c)))
    state = pltpu.VMEM((SSM_GROUPS, SSM_STATE, GW), F32)
    body = functools.partial(_ssd_body, reverse=reverse, final=final, heads=heads, col_off=col_off)
    t = batch * seq
    head_specs = [full((1, LANES)), full((1, LANES)), full((LANES, 1)), full((LANES, 1))]
    head_args = [dt_bias, a_log, dt_bias.reshape(LANES, 1), a_log.reshape(LANES, 1)]
    if final:
        in_specs = [rows(W), rows(CC), rows(LANES), cols] + head_specs + [full((1, W)), full((1, W)), rows(W)]
        args = [extra["z"], extra["act"], dt, dt_t] + head_args + [extra["d_skip"], extra["norm_w"], extra["y_other"]]
        out_specs = rows(W)
        out_shape = jax.ShapeDtypeStruct((t, W), BF16)
        scratch = [state]
    else:
        in_specs = [
            rows(CC),
            pl.BlockSpec((BF16_ROWS, CC), lambda b, c: (jnp.maximum(rb(b, c) * per - 1, 0), 0)),
            pl.BlockSpec((BF16_ROWS, CC), lambda b, c: (jnp.minimum((rb(b, c) + 1) * per, nhalo - 1), 0)),
            rows(LANES), cols,
            full((CONV_WIDTH, CC)), full((1, CC)),
        ] + head_specs
        args = [xbc, xbc, xbc, dt, dt_t, conv_w, conv_b] + head_args
        out_specs = [rows(W), rows(CC)]
        out_shape = [jax.ShapeDtypeStruct((t, W), BF16), jax.ShapeDtypeStruct((t, CC), BF16)]
        scratch = [pltpu.VMEM((L + 2 * SUBLANES, CC), F32), pltpu.VMEM((L, CC), F32), state]
    return pl.pallas_call(
        body,
        grid=(batch, nc),
        in_specs=in_specs,
        out_specs=out_specs,
        out_shape=out_shape,
        scratch_shapes=scratch,
        compiler_params=_params("parallel", "arbitrary"),
        name="ssd_final" if final else "ssd_scan",
    )(*args)


def _router_body(x_ref, nw_ref, wr_ref, br_ref, h_ref, idx_ref, gate_ref, cnt_ref, run_ref):
    tm, d = x_ref.shape
    half = d // 2

    @pl.when(pl.program_id(0) == 0)
    def _():
        run_ref[...] = jnp.zeros(run_ref.shape, F32)

    x = x_ref[...]
    ms = jnp.mean(x * x, axis=-1, keepdims=True)
    h = x * lax.rsqrt(ms + EPS) * nw_ref[...]
    h_ref[...] = _pack_pairs(h[:, :half], h[:, half:])
    h_hi = h.astype(BF16)
    h_lo = (h - h_hi.astype(F32)).astype(BF16)
    hw = jnp.dot(h_hi, wr_ref[...], preferred_element_type=F32)
    logits = (hw[:, :LANES] + (hw[:, LANES:] + jnp.dot(h_lo, wr_ref[:, :LANES], preferred_element_type=F32))
              + br_ref[...])
    lane = lax.broadcasted_iota(jnp.int32, logits.shape, 1).astype(F32)
    ninf = -jnp.inf
    is_g = lane < N_EXPERT_GROUPS
    gl = jnp.where(is_g, logits, ninf)
    gmax = jnp.max(gl, axis=-1, keepdims=True)
    gsum = jnp.sum(jnp.where(is_g, jnp.exp(gl - gmax), 0.0), axis=-1, keepdims=True)
    p_group = 1.0 / gsum
    g_sel = jnp.min(jnp.where(gl == gmax, lane, float(LANES)), axis=-1, keepdims=True)
    lo = N_EXPERT_GROUPS + EXPERTS_PER_GROUP * g_sel
    el = jnp.where((lane >= lo) & (lane < lo + EXPERTS_PER_GROUP), logits, ninf)
    m1 = jnp.max(el, axis=-1, keepdims=True)
    i1 = jnp.min(jnp.where(el == m1, lane, float(LANES)), axis=-1, keepdims=True)
    el2 = jnp.where(lane == i1, ninf, el)
    m2 = jnp.max(el2, axis=-1, keepdims=True)
    i2 = jnp.min(jnp.where(el2 == m2, lane, float(LANES)), axis=-1, keepdims=True)
    r = jnp.exp(m2 - m1)
    g1 = p_group / (1.0 + r)
    g2 = p_group * r / (1.0 + r)
    e1 = i1 - N_EXPERT_GROUPS
    e2 = i2 - N_EXPERT_GROUPS
    oh1 = jnp.where(lane == e1, 1.0, 0.0)
    oh2 = jnp.where(lane == e2, 1.0, 0.0)
    both = oh1 + oh2
    ri = lax.broadcasted_iota(jnp.int32, (tm, tm), 0)
    ci = lax.broadcasted_iota(jnp.int32, (tm, tm), 1)
    earlier = jnp.where(ci < ri, 1.0, 0.0).astype(BF16)
    before = jnp.dot(earlier, both.astype(BF16), preferred_element_type=F32) + run_ref[...]
    rank1 = jnp.sum(oh1 * before, axis=-1, keepdims=True)
    rank2 = jnp.sum(oh2 * before, axis=-1, keepdims=True)
    run_ref[...] = run_ref[...] + jnp.sum(both, axis=0, keepdims=True)
    cnt_ref[...] = run_ref[...]
    packed = jnp.where(lane == 0, e1, jnp.where(lane == 1, e2, jnp.where(lane == 2, rank1, jnp.where(lane == 3, rank2, 0.0))))
    idx_ref[...] = packed.astype(jnp.int32)
    gate_ref[...] = jnp.where(lane == 0, g1, jnp.where(lane == 1, g2, 0.0))


def norm_and_route(x, norm_w, w_router, b_router, tm):
    t, d = x.shape
    row = lambda w: pl.BlockSpec((tm, w), lambda i: (i, 0))
    one = lambda w: pl.BlockSpec((1, w), lambda i: (0, 0))
    w_hi = w_router.astype(BF16)
    w_router = jnp.concatenate([w_hi, (w_router - w_hi.astype(F32)).astype(BF16)], axis=1)
    return pl.pallas_call(
        _router_body,
        grid=(t // tm,),
        in_specs=[row(d), one(d), pl.BlockSpec((d, 2 * LANES), lambda i: (0, 0)), one(LANES)],
        out_specs=[row(d // 2), row(LANES), row(LANES), one(LANES)],
        out_shape=[jax.ShapeDtypeStruct((t, d // 2), U32), jax.ShapeDtypeStruct((t, LANES), jnp.int32),
                   jax.ShapeDtypeStruct((t, LANES), F32), jax.ShapeDtypeStruct((1, LANES), F32)],
        scratch_shapes=[pltpu.VMEM((1, LANES), F32)],
        compiler_params=_params("arbitrary"),
        name="router",
    )(x, norm_w.reshape(1, d).astype(F32), w_router, b_router)


def _moe_plan(idx, counts, tm):
    t = idx.shape[0]
    n = t * TOP_K
    experts = jnp.arange(N_EXPERTS, dtype=I32)
    counts = counts[0, :N_EXPERTS].astype(I32)
    padded = (counts + tm - 1) // tm * tm
    pend = jnp.cumsum(padded)
    pstart = pend - padded
    e = idx[:, :TOP_K]
    rank = idx[:, TOP_K:2 * TOP_K]
    dest = rank + jnp.sum(jnp.where(e[:, :, None] == experts, pstart, 0), axis=-1)
    nblk = n // tm + N_EXPERTS
    n_used = pend[-1] // tm
    blk = jnp.arange(nblk, dtype=I32)
    blk_e = jnp.minimum(jnp.sum((pend[None, :] <= (blk * tm)[:, None]).astype(I32), axis=-1), N_EXPERTS - 1)
    last_e = jnp.sum(jnp.where(blk == n_used - 1, blk_e, 0))
    blk_e = jnp.where(blk < n_used, blk_e, last_e)
    later = (counts[None, :] > 0) & (experts[None, :] > experts[:, None])
    next_of = jnp.min(jnp.where(later, experts[None, :], N_EXPERTS), axis=1)
    next_of = jnp.where(next_of < N_EXPERTS, next_of, -1)
    blk_next = jnp.sum(jnp.where(blk_e[:, None] == experts, next_of[None, :], 0), axis=1)
    return (dest.astype(I32), pend.astype(I32), blk_e.astype(I32), blk_next.astype(I32),
            n_used.reshape(1).astype(I32), nblk * tm)


def _dispatch_body(pend_ref, nu_ref, dest_ref, h_ref, xs_hbm, idx_smem, zero_ref, sem, isem, *, tt, tm):
    i = pl.program_id(0)

    @pl.when(i == 0)
    def _():
        zero_ref[...] = jnp.zeros(zero_ref.shape, zero_ref.dtype)

        def zero_block(row0):
            cp = pltpu.make_async_copy(zero_ref, xs_hbm.at[pl.ds(pl.multiple_of(row0, tm), tm), :], sem)
            cp.start()
            cp.wait()

        def fill(e, c):
            end = pend_ref[e]
            prev = jnp.where(e > 0, pend_ref[jnp.maximum(e - 1, 0)], 0)

            @pl.when(end > prev)
            def _():
                zero_block(end - tm)
            return c

        lax.fori_loop(0, N_EXPERTS, fill, 0)

        def fill_unused(b, c):
            zero_block(b * tm)
            return c

        lax.fori_loop(nu_ref[0], xs_hbm.shape[0] // tm, fill_unused, 0)

    cp = pltpu.make_async_copy(dest_ref.at[0, 0], idx_smem, isem)
    cp.start()
    cp.wait()

    def issue(r, c):
        for k in range(TOP_K):
            d = idx_smem[TOP_K * r + k]
            pltpu.make_async_copy(h_ref.at[pl.ds(r, 1), :], xs_hbm.at[pl.ds(d, 1), :], sem).start(priority=k)
        return c

    lax.fori_loop(0, tt, issue, 0, unroll=8)
    all_rows = xs_hbm.at[pl.ds(0, TOP_K * tt), :]
    pltpu.make_async_copy(all_rows, all_rows, sem).wait()


def dispatch_rows(pend, n_used, dest, h_packed, rows, tt, tm):
    t, w = h_packed.shape
    return pl.pallas_call(
        functools.partial(_dispatch_body, tt=tt, tm=tm),
        grid_spec=pltpu.PrefetchScalarGridSpec(
            num_scalar_prefetch=2,
            grid=(t // tt,),
            in_specs=[
                pl.BlockSpec((1, 1, TOP_K * tt), lambda i, pe, nu: (i, 0, 0)),
                pl.BlockSpec((tt, w), lambda i, pe, nu: (i, 0)),
            ],
            out_specs=pl.BlockSpec(memory_space=pl.ANY),
            scratch_shapes=[
                pltpu.SMEM((TOP_K * tt,), I32),
                pltpu.VMEM((tm, w), U32),
                pltpu.SemaphoreType.DMA,
                pltpu.SemaphoreType.DMA,
            ],
        ),
        out_shape=jax.ShapeDtypeStruct((rows, w), U32),
        compiler_params=_params("arbitrary"),
        name="dispatch_rows",
    )(pend, n_used, dest.reshape(t // tt, 1, TOP_K * tt), h_packed)


def _expert_changed(be_ref, i):
    return (i == 0) | (be_ref[i] != be_ref[jnp.maximum(i - 1, 0)])


CAST_ROWS = 256


def _cast_rows(src_ref, dst_ref):
    def body(r, carry):
        rows = pl.ds(pl.multiple_of(r * CAST_ROWS, CAST_ROWS), CAST_ROWS)
        dst_ref[rows, :] = src_ref[rows, :].astype(dst_ref.dtype)
        return carry
    lax.fori_loop(0, src_ref.shape[0] // CAST_ROWS, body, 0)


def _stream_weights(be_ref, nx_ref, i, sweep, n_sweeps, copies, on_ready, cnt_ref):
    @pl.when((sweep == 0) & (i == 0))
    def _():
        cnt_ref[0] = 0
        for cp in copies(be_ref[0], 0, 0):
            cp.start()

    @pl.when(_expert_changed(be_ref, i))
    def _():
        slot = cnt_ref[0] & 1
        for cp in copies(be_ref[i], sweep, slot):
            cp.wait()
        on_ready(slot)
        nxt = nx_ref[i]

        @pl.when(nxt >= 0)
        def _():
            for cp in copies(nxt, sweep, 1 - slot):
                cp.start()

        @pl.when((nxt < 0) & (sweep + 1 < n_sweeps))
        def _():
            for cp in copies(be_ref[0], sweep + 1, 1 - slot):
                cp.start()

        cnt_ref[0] = cnt_ref[0] + 1


def _gate_up_body(be_ref, nx_ref, nu_ref, x_ref, wg_hbm, wu_hbm, o_ref, wbuf_ref, wgb_ref, wub_ref, sem_ref, cnt_ref,
                  *, tf):
    j = pl.program_id(0)
    i = pl.program_id(1)
    half = x_ref.shape[1]

    def copies(e, jj, slot):
        cols = pl.ds(pl.multiple_of(jj * tf, tf), tf)
        return (pltpu.make_async_copy(wg_hbm.at[e, :, cols], wbuf_ref.at[slot, 0], sem_ref.at[slot]),
                pltpu.make_async_copy(wu_hbm.at[e, :, cols], wbuf_ref.at[slot, 1], sem_ref.at[slot]))

    def on_ready(slot):
        _cast_rows(wbuf_ref.at[slot, 0], wgb_ref)
        _cast_rows(wbuf_ref.at[slot, 1], wub_ref)

    _stream_weights(be_ref, nx_ref, i, j, pl.num_programs(0), copies, on_ready, cnt_ref)

    @pl.when(i < nu_ref[0])
    def _():
        x_lo, x_hi = _unpack_pairs(x_ref[...])
        x_lo = x_lo.astype(BF16)
        x_hi = x_hi.astype(BF16)

        def mm(w_ref):
            return (jnp.dot(x_lo, w_ref[:half, :], preferred_element_type=F32)
                    + jnp.dot(x_hi, w_ref[half:, :], preferred_element_type=F32))

        a = mm(wgb_ref)
        b = mm(wub_ref)
        o_ref[...] = (a * _sigmoid(a) * b).astype(o_ref.dtype)

    @pl.when(i >= nu_ref[0])
    def _():
        o_ref[...] = jnp.zeros(o_ref.shape, o_ref.dtype)


def expert_gate_up(blk_e, blk_next, n_used, xs, w_gate, w_up, tm, tf):
    rows = xs.shape[0]
    d, f = w_gate.shape[1], w_gate.shape[2]
    nblk = rows // tm
    last = lambda i, nu: jnp.minimum(i, nu[0] - 1)
    hbm = pl.BlockSpec(memory_space=pl.ANY)
    return pl.pallas_call(
        functools.partial(_gate_up_body, tf=tf),
        grid_spec=pltpu.PrefetchScalarGridSpec(
            num_scalar_prefetch=3,
            grid=(f // tf, nblk),
            in_specs=[pl.BlockSpec((tm, d // 2), lambda j, i, be, nx, nu: (last(i, nu), 0)), hbm, hbm],
            out_specs=pl.BlockSpec((tm, tf), lambda j, i, be, nx, nu: (i, j)),
            scratch_shapes=[
                pltpu.VMEM((2, 2, d, tf), F32),
                pltpu.VMEM((d, tf), BF16),
                pltpu.VMEM((d, tf), BF16),
                pltpu.SemaphoreType.DMA((2,)),
                pltpu.SMEM((1,), I32),
            ],
        ),
        out_shape=jax.ShapeDtypeStruct((rows, f), BF16),
        compiler_params=_params("arbitrary", "arbitrary"),
        name="expert_gate_up",
    )(blk_e, blk_next, n_used, xs, w_gate, w_up)


def _down_body(be_ref, nx_ref, nu_ref, a_ref, wd_hbm, o_ref, wbuf_ref, wdb_ref, sem_ref, cnt_ref):
    i = pl.program_id(0)
    half = o_ref.shape[1]

    def copies(e, sweep, slot):
        return (pltpu.make_async_copy(wd_hbm.at[e], wbuf_ref.at[slot], sem_ref.at[slot]),)

    def on_ready(slot):
        _cast_rows(wbuf_ref.at[slot], wdb_ref)

    _stream_weights(be_ref, nx_ref, i, 0, 1, copies, on_ready, cnt_ref)

    @pl.when(i < nu_ref[0])
    def _():
        y = jnp.dot(a_ref[...], wdb_ref[...], preferred_element_type=F32)
        o_ref[...] = _pack_pairs(y[:, :half], y[:, half:])

    @pl.when(i >= nu_ref[0])
    def _():
        o_ref[...] = jnp.zeros(o_ref.shape, o_ref.dtype)


def expert_down(blk_e, blk_next, n_used, act, w_down, tm):
    rows, f = act.shape
    d = w_down.shape[2]
    nblk = rows // tm
    last = lambda i, nu: jnp.minimum(i, nu[0] - 1)
    return pl.pallas_call(
        _down_body,
        grid_spec=pltpu.PrefetchScalarGridSpec(
            num_scalar_prefetch=3,
            grid=(nblk,),
            in_specs=[
                pl.BlockSpec((tm, f), lambda i, be, nx, nu: (last(i, nu), 0)),
                pl.BlockSpec(memory_space=pl.ANY),
            ],
            out_specs=pl.BlockSpec((tm, d // 2), lambda i, be, nx, nu: (i, 0)),
            scratch_shapes=[
                pltpu.VMEM((2, f, d), F32),
                pltpu.VMEM((f, d), BF16),
                pltpu.SemaphoreType.DMA((2,)),
                pltpu.SMEM((1,), I32),
            ],
        ),
        out_shape=jax.ShapeDtypeStruct((rows, d // 2), U32),
        compiler_params=_params("arbitrary"),
        name="expert_down",
    )(blk_e, blk_next, n_used, act, w_down)


def _combine_body(dest_ref, x_ref, g_ref, w_ref, y_hbm, o_ref, idx_smem, buf_ref, sem, isem, *, tt):
    d = x_ref.shape[1]
    half = d // 2
    cp = pltpu.make_async_copy(dest_ref.at[0, 0], idx_smem, isem)
    cp.start()
    cp.wait()

    def issue(r, c):
        for k in range(TOP_K):
            src = idx_smem[TOP_K * r + k]
            pltpu.make_async_copy(y_hbm.at[pl.ds(src, 1), :], buf_ref.at[k, pl.ds(r, 1), :], sem).start(priority=k)
        return c

    lax.fori_loop(0, tt, issue, 0, unroll=8)
    pltpu.make_async_copy(buf_ref, buf_ref, sem).wait()

    y0_lo, y0_hi = _unpack_pairs(buf_ref[0])
    y1_lo, y1_hi = _unpack_pairs(buf_ref[1])
    g0 = g_ref[:, 0:1]
    g1 = g_ref[:, 1:2]
    x_lo = x_ref[:, :half] + (g0 * y0_lo + g1 * y1_lo)
    x_hi = x_ref[:, half:] + (g0 * y0_hi + g1 * y1_hi)
    ms = (jnp.sum(x_lo * x_lo, axis=-1, keepdims=True) + jnp.sum(x_hi * x_hi, axis=-1, keepdims=True)) / d
    inv = lax.rsqrt(ms + EPS)
    o_ref[:, :half] = x_lo * inv * w_ref[:, :half]
    o_ref[:, half:] = x_hi * inv * w_ref[:, half:]


def combine_and_norm(x, y_packed, dest, gate, w, tt):
    t, d = x.shape
    row = lambda wd: pl.BlockSpec((tt, wd), lambda i: (i, 0))
    return pl.pallas_call(
        functools.partial(_combine_body, tt=tt),
        grid=(t // tt,),
        in_specs=[
            pl.BlockSpec((1, 1, TOP_K * tt), lambda i: (i, 0, 0)),
            row(d), row(LANES), pl.BlockSpec((1, d), lambda i: (0, 0)),
            pl.BlockSpec(memory_space=pl.ANY),
        ],
        out_specs=row(d),
        out_shape=jax.ShapeDtypeStruct((t, d), F32),
        scratch_shapes=[
            pltpu.SMEM((TOP_K * tt,), I32),
            pltpu.VMEM((TOP_K, tt, d // 2), U32),
            pltpu.SemaphoreType.DMA,
            pltpu.SemaphoreType.DMA,
        ],
        compiler_params=_params("arbitrary"),
        name="combine_norm",
    )(dest.reshape(t // tt, 1, TOP_K * tt), x, gate, w.reshape(1, d).astype(F32), y_packed)


def _tiles(batch, seq):
    t = batch * seq
    return dict(
        norm_tm=min(256, t),
        mm_tm=min(1024, t),
        mm_tn=1024,
        out_tm=min(512, t),
        out_tn=1024,
        attn_tq=min(1024, seq),
        attn_tk=min(512, seq),
        moe_tm=min(256, t),
        moe_tf=512,
        disp_tt=min(1024, t),
        comb_tt=min(512, t),
    )


def _lambda_init_at(layer):
    return 0.8 - 0.6 * math.exp(-0.3 * layer)


def kernel(x, rel_bias, norm1_w, w_in, lambda_q1, lambda_k1, lambda_q2, lambda_k2, subln_w, conv_w, conv_b,
           dt_bias_f, dt_bias_b, a_log_f, a_log_b, d_skip, ssm_norm_w, w_out, norm2_w, w_group_router,
           b_group_router, w_expert_router, b_expert_router, w_gate, w_up, w_down, final_norm_w):
    batch, seq, d = x.shape
    t = batch * seq
    depth = norm1_w.shape[0]
    attn_w = d // 2
    ssm_w = d - attn_w
    dv = 2 * ATTN_HEAD_DIM
    a_heads = attn_w // dv
    s_heads = ssm_w // SSM_HEAD_DIM
    cc = ssm_w + 2 * SSM_GROUPS * SSM_STATE
    main_w = 3 * attn_w + ssm_w + cc
    assert ssm_w == attn_w and cc == 2 * ssm_w and 2 * s_heads <= LANES
    tl = _tiles(batch, seq)
    log2e = math.log2(math.e)
    band = _band_table(rel_bias, tl["attn_tq"], tl["attn_tk"], log2e)
    q_scale = jnp.where(jnp.arange(w_in.shape[2]) < attn_w, ATTN_HEAD_DIM ** -0.5 * log2e, 1.0).astype(F32)

    def pad_lanes(v):
        return jnp.pad(v.astype(F32), (0, LANES - v.shape[0])).reshape(1, LANES)

    xf = x.reshape(t, d)
    for layer in range(depth):
        lam_init = _lambda_init_at(layer)
        w_t = (jnp.transpose(w_in[layer]) * q_scale[:, None]).astype(BF16)
        w_dt = jnp.pad(w_t[main_w:], ((0, LANES - 2 * s_heads), (0, 0)))
        h = rmsnorm_rows(xf, norm1_w[layer], BF16, tl["norm_tm"])
        tm, tn = tl["mm_tm"], tl["mm_tn"]
        qk = matmul_nt(h, w_t, BF16, t, 2 * attn_w, tm, tn, "in_proj_qk")
        v_t = matmul_nt(w_t, h, BF16, attn_w, t, tn, tm, "in_proj_v", a_blk0=2 * attn_w // tn)
        z, dt, dt_t = projection_with_dt(h, w_t, w_dt, BF16, t, ssm_w, tm, tn, "in_proj_z_dt",
                                         b_blk0=3 * attn_w // tn)
        xbc = matmul_nt(h, w_t, BF16, t, cc, tm, tn, "in_proj_xbc", b_blk0=(3 * attn_w + ssm_w) // tn)

        vec = lambda v: v[layer].reshape(1, -1).astype(F32)
        attn = diff_attention(qk, v_t, band, vec(lambda_q1), vec(lambda_k1), vec(lambda_q2), vec(lambda_k2),
                              subln_w[layer].reshape(dv, 1).astype(F32), batch=batch, seq=seq, heads=a_heads,
                              lam_init=lam_init, tq=tl["attn_tq"], tk=tl["attn_tk"])

        dt_bias = pad_lanes(jnp.concatenate([dt_bias_f[layer], dt_bias_b[layer]]))
        a_log = pad_lanes(jnp.concatenate([a_log_f[layer], a_log_b[layer]]))
        cw = conv_w[layer].astype(F32)
        cb = conv_b[layer].reshape(1, cc).astype(F32)
        common = dict(batch=batch, seq=seq, heads=s_heads)
        y_bwd, act = ssd_pass(xbc, dt, dt_t, cw, cb, dt_bias, a_log, None, reverse=True, final=False,
                              col_off=s_heads, **common)
        extra = dict(d_skip=jnp.repeat(d_skip[layer].astype(F32), SSM_HEAD_DIM).reshape(1, ssm_w),
                     norm_w=vec(ssm_norm_w), y_other=y_bwd, act=act, z=z)
        ssm = ssd_pass(None, dt, dt_t, cw, cb, dt_bias, a_log, extra, reverse=False, final=True, col_off=0, **common)

        x1 = out_projection(attn, ssm, w_out[layer].astype(BF16), xf, tl["out_tm"], tl["out_tn"])

        w_router = jnp.pad(jnp.concatenate([w_group_router[layer], w_expert_router[layer]], axis=1).astype(F32),
                           ((0, 0), (0, LANES - N_EXPERT_GROUPS - N_EXPERTS)))
        b_router = pad_lanes(jnp.concatenate([b_group_router[layer], b_expert_router[layer]]))
        h2, idx, gate, counts = norm_and_route(x1, norm2_w[layer], w_router, b_router, tl["norm_tm"])
        tm = tl["moe_tm"]
        dest, pend, blk_e, blk_next, n_used, rows = _moe_plan(idx, counts, tm)
        xs = dispatch_rows(pend, n_used, dest, h2, rows, tl["disp_tt"], tm)
        act = expert_gate_up(blk_e, blk_next, n_used, xs, w_gate[layer], w_up[layer], tm, tl["moe_tf"])
        y = expert_down(blk_e, blk_next, n_used, act, w_down[layer], tm)
        if layer + 1 < depth:
            raise NotImplementedError("multi-layer stacking needs an un-normalised combine")
        xf = combine_and_norm(x1, y, dest, gate, final_norm_w, tl["comb_tt"])
    return xf.reshape(batch, seq, d)
```

```python
import functools
import math

import jax
import jax.numpy as jnp
from jax import lax
from jax.experimental import pallas as pl
from jax.experimental.pallas import tpu as pltpu

F32 = jnp.float32
BF16 = jnp.bfloat16
U32 = jnp.uint32
I32 = jnp.int32
EPS = 1e-6

ATTN_HEAD_DIM = 128
NUM_BUCKETS = 32
MAX_DISTANCE = 128
SSM_HEAD_DIM = 64
SSM_GROUPS = 8
SSM_STATE = 128
CONV_WIDTH = 5
CHUNK = 128
N_EXPERT_GROUPS = 4
EXPERTS_PER_GROUP = 8
N_EXPERTS = N_EXPERT_GROUPS * EXPERTS_PER_GROUP
TOP_K = 2

LANES = 128
SUBLANES = 8
BF16_ROWS = 16
VMEM_LIMIT = 56 * 1024 * 1024
HI16 = 0xFFFF0000


def _params(*sem):
    return pltpu.CompilerParams(dimension_semantics=sem, vmem_limit_bytes=VMEM_LIMIT)


def _sigmoid(x):
    return 1.0 / (1.0 + jnp.exp(-x))


def _pack_pairs(lo, hi):
    lo_b = lax.bitcast_convert_type(lo.astype(BF16).astype(F32), U32)
    hi_b = lax.bitcast_convert_type(hi.astype(BF16).astype(F32), U32)
    return (lo_b >> 16) | (hi_b & U32(HI16))


def _unpack_pairs(w):
    return lax.bitcast_convert_type(w << 16, F32), lax.bitcast_convert_type(w & U32(HI16), F32)


def _rmsnorm_body(x_ref, w_ref, o_ref):
    x = x_ref[...]
    ms = jnp.mean(x * x, axis=-1, keepdims=True)
    o_ref[...] = (x * lax.rsqrt(ms + EPS) * w_ref[...]).astype(o_ref.dtype)


def rmsnorm_rows(x, w, out_dtype, tm):
    t, d = x.shape
    return pl.pallas_call(
        _rmsnorm_body,
        grid=(t // tm,),
        in_specs=[pl.BlockSpec((tm, d), lambda i: (i, 0)), pl.BlockSpec((1, d), lambda i: (0, 0))],
        out_specs=pl.BlockSpec((tm, d), lambda i: (i, 0)),
        out_shape=jax.ShapeDtypeStruct((t, d), out_dtype),
        compiler_params=_params("parallel"),
        name="rmsnorm",
    )(x, w.reshape(1, d).astype(F32))


NT_DIMS = (((1,), (1,)), ((), ()))


def _matmul_nt_body(a_ref, b_ref, o_ref):
    o_ref[...] = lax.dot_general(a_ref[...], b_ref[...], NT_DIMS, preferred_element_type=F32).astype(o_ref.dtype)


def matmul_nt(a, b, out_dtype, m, n, tm, tn, name, a_blk0=0, b_blk0=0):
    k = a.shape[1]
    assert b.shape[1] == k and m % tm == 0 and n % tn == 0
    return pl.pallas_call(
        _matmul_nt_body,
        grid=(m // tm, n // tn),
        in_specs=[pl.BlockSpec((tm, k), lambda i, j: (a_blk0 + i, 0)),
                  pl.BlockSpec((tn, k), lambda i, j: (b_blk0 + j, 0))],
        out_specs=pl.BlockSpec((tm, tn), lambda i, j: (i, j)),
        out_shape=jax.ShapeDtypeStruct((m, n), out_dtype),
        compiler_params=_params("parallel", "parallel"),
        name=name,
    )(a, b)


def _proj_with_dt_body(a_ref, b_ref, wdt_ref, o_ref, dt_ref, dtt_ref):
    o_ref[...] = lax.dot_general(a_ref[...], b_ref[...], NT_DIMS, preferred_element_type=F32).astype(o_ref.dtype)

    @pl.when(pl.program_id(1) == 0)
    def _():
        dt = lax.dot_general(a_ref[...], wdt_ref[...], NT_DIMS, preferred_element_type=F32)
        dt_ref[...] = dt
        dtt_ref[...] = dt.T


def projection_with_dt(a, b, w_dt, out_dtype, m, n, tm, tn, name, b_blk0):
    k = a.shape[1]
    nd = w_dt.shape[0]
    return pl.pallas_call(
        _proj_with_dt_body,
        grid=(m // tm, n // tn),
        in_specs=[pl.BlockSpec((tm, k), lambda i, j: (i, 0)),
                  pl.BlockSpec((tn, k), lambda i, j: (b_blk0 + j, 0)),
                  pl.BlockSpec((nd, k), lambda i, j: (0, 0))],
        out_specs=[pl.BlockSpec((tm, tn), lambda i, j: (i, j)),
                   pl.BlockSpec((tm, nd), lambda i, j: (i, 0)),
                   pl.BlockSpec((nd, tm), lambda i, j: (0, i))],
        out_shape=[jax.ShapeDtypeStruct((m, n), out_dtype), jax.ShapeDtypeStruct((m, nd), F32),
                   jax.ShapeDtypeStruct((nd, m), F32)],
        compiler_params=_params("parallel", "arbitrary"),
        name=name,
    )(a, b, w_dt)


def _outproj_body(a_ref, s_ref, wa_ref, ws_ref, r_ref, o_ref):
    acc = jnp.dot(a_ref[...], wa_ref[...], preferred_element_type=F32)
    acc = acc + jnp.dot(s_ref[...], ws_ref[...], preferred_element_type=F32)
    o_ref[...] = r_ref[...] + acc


def out_projection(attn, ssm, w, resid, tm, tn):
    m, ka = attn.shape
    ks = ssm.shape[1]
    assert ka == ks
    n = w.shape[1]
    return pl.pallas_call(
        _outproj_body,
        grid=(n // tn, m // tm),
        in_specs=[
            pl.BlockSpec((tm, ka), lambda j, i: (i, 0)),
            pl.BlockSpec((tm, ks), lambda j, i: (i, 0)),
            pl.BlockSpec((ka, tn), lambda j, i: (0, j)),
            pl.BlockSpec((ks, tn), lambda j, i: (1, j)),
            pl.BlockSpec((tm, tn), lambda j, i: (i, j)),
        ],
        out_specs=pl.BlockSpec((tm, tn), lambda j, i: (i, j)),
        out_shape=jax.ShapeDtypeStruct((m, n), F32),
        compiler_params=_params("parallel", "parallel"),
        name="out_proj",
    )(attn, ssm, w, w, resid)


def _t5_bucket(rel):
    half = NUM_BUCKETS // 2
    max_exact = half // 2
    n = jnp.abs(rel)
    large = max_exact + (
        jnp.log(jnp.maximum(n, 1).astype(F32) / max_exact) / math.log(MAX_DISTANCE / max_exact) * (half - max_exact)
    ).astype(jnp.int32)
    large = jnp.minimum(large, half - 1)
    return jnp.where(rel > 0, half, 0) + jnp.where(n < max_exact, n, large)


def _band_body(v_ref, o_ref, *, tq, w):
    npad = v_ref.shape[-1]
    x = jnp.broadcast_to(v_ref[0], (tq, npad))
    y = pltpu.roll(x, npad - (tq - 1), 1, stride=1, stride_axis=0)
    o_ref[0] = y[:, :w].T


def _band_table(rel_bias, tq, tk, scale):
    assert tk + 1 >= MAX_DISTANCE
    heads = rel_bias.shape[1]
    w = tq + 4 * tk
    n = w + tq - 1
    npad = -(-n // LANES) * LANES
    rel = jnp.arange(npad, dtype=jnp.int32) - (2 * tk + tq - 1)
    v = (rel_bias[_t5_bucket(rel)].astype(F32) * scale).T.reshape(heads, 1, npad)
    return pl.pallas_call(
        functools.partial(_band_body, tq=tq, w=w),
        grid=(heads,),
        in_specs=[pl.BlockSpec((1, 1, npad), lambda h: (h, 0, 0))],
        out_specs=pl.BlockSpec((1, w, tq), lambda h: (h, 0, 0)),
        out_shape=jax.ShapeDtypeStruct((heads, w, tq), F32),
        compiler_params=_params("parallel"),
        name="band_table",
    )(v)


ATTN_STRIP = 16


def _attn_body(q_ref, k_ref, vt_ref, band_ref, eye_ref, lq1_ref, lk1_ref, lq2_ref, lk2_ref, subw_ref, o_ref,
               m_ref, l_ref, acc_ref, sa_ref, sb_ref, mxa_ref, mxb_ref, p_ref, *, tq, tk, nk, lam_init):
    dh = ATTN_HEAD_DIM
    rs = ATTN_STRIP
    qi = pl.program_id(2)

    m_ref[...] = jnp.full(m_ref.shape, -jnp.inf, F32)
    l_ref[...] = jnp.zeros(l_ref.shape, F32)
    acc_ref[...] = jnp.zeros(acc_ref.shape, F32)

    def scores(kc, s_ref, mx_ref):
        k0 = pl.multiple_of(kc * tk, tk)
        start = pl.multiple_of(jnp.clip(kc * tk - qi * tq + 2 * tk, 0, tq + 3 * tk), LANES)
        for mi in range(2):
            kk = k_ref[pl.ds(k0, tk), mi * dh:(mi + 1) * dh]
            q = q_ref[:, mi * dh:(mi + 1) * dh]
            s = lax.dot_general(kk, q, NT_DIMS, preferred_element_type=F32) + band_ref[0, pl.ds(start, tk), :]
            s_ref[mi] = s
            mx = s[0:SUBLANES]
            for i in range(1, tk // SUBLANES):
                mx = jnp.maximum(mx, s[i * SUBLANES:(i + 1) * SUBLANES])
            mx_ref[mi] = mx

    def softmax_pv(kc, s_ref, mx_ref):
        k0 = pl.multiple_of(kc * tk, tk)
        for mi in range(2):
            m_prev = m_ref[mi]
            m_new = jnp.maximum(m_prev, jnp.max(mx_ref[mi], axis=0, keepdims=True))
            alpha = jnp.exp2(m_prev - m_new)

            def strip_exp(i, ls):
                p = jnp.exp2(s_ref[mi, pl.ds(i * rs, rs), :] - m_new)
                p_ref[mi, pl.ds(i * rs, rs), :] = p.astype(BF16)
                return ls + p

            ls = lax.fori_loop(0, tk // rs, strip_exp, jnp.zeros((rs, tq), F32), unroll=True)
            l_ref[mi] = alpha * l_ref[mi] + jnp.sum(ls, axis=0, keepdims=True)
            m_ref[mi] = m_new
            pv = jnp.dot(vt_ref[:, pl.ds(k0, tk)], p_ref[mi], preferred_element_type=F32)
            acc_ref[mi] = acc_ref[mi] * alpha + pv

    scores(0, sa_ref, mxa_ref)
    if nk > 1:
        assert nk % 2 == 0

        def pair(j, carry):
            scores(2 * j + 1, sb_ref, mxb_ref)
            softmax_pv(2 * j, sa_ref, mxa_ref)
            scores(2 * j + 2, sa_ref, mxa_ref)
            softmax_pv(2 * j + 1, sb_ref, mxb_ref)
            return carry

        lax.fori_loop(0, nk // 2 - 1, pair, 0)
        scores(nk - 1, sb_ref, mxb_ref)
        softmax_pv(nk - 2, sa_ref, mxa_ref)
        softmax_pv(nk - 1, sb_ref, mxb_ref)
    else:
        softmax_pv(0, sa_ref, mxa_ref)

    lam = (jnp.exp(jnp.sum(lq1_ref[...] * lk1_ref[...], axis=-1, keepdims=True))
           - jnp.exp(jnp.sum(lq2_ref[...] * lk2_ref[...], axis=-1, keepdims=True)) + lam_init)
    o = acc_ref[0] / l_ref[0] - lam * (acc_ref[1] / l_ref[1])
    ms = jnp.mean(o * o, axis=0, keepdims=True)
    o = (o * lax.rsqrt(ms + EPS) * subw_ref[...] * (1.0 - lam_init)).astype(BF16)
    o_ref[...] = lax.dot_general(eye_ref[...], o, NT_DIMS, preferred_element_type=F32).astype(o_ref.dtype)


def diff_attention(qk, vt, band, lq1, lk1, lq2, lk2, subw, *, batch, seq, heads, lam_init, tq, tk):
    dv = 2 * ATTN_HEAD_DIM
    nq, nk = seq // tq, seq // tk
    assert tk % ATTN_STRIP == 0
    vec = pl.BlockSpec((1, ATTN_HEAD_DIM), lambda b, h, i: (0, 0))
    body = functools.partial(_attn_body, tq=tq, tk=tk, nk=nk, lam_init=lam_init)
    eye = jnp.eye(tq, dtype=BF16)
    return pl.pallas_call(
        body,
        grid=(batch, heads, nq),
        in_specs=[
            pl.BlockSpec((tq, dv), lambda b, h, i: (b * nq + i, h)),
            pl.BlockSpec((seq, dv), lambda b, h, i: (b, heads + h)),
            pl.BlockSpec((dv, seq), lambda b, h, i: (h, b)),
            pl.BlockSpec((1, tq + 4 * tk, tq), lambda b, h, i: (h, 0, 0)),
            pl.BlockSpec((tq, tq), lambda b, h, i: (0, 0)),
            vec, vec, vec, vec,
            pl.BlockSpec((dv, 1), lambda b, h, i: (0, 0)),
        ],
        out_specs=pl.BlockSpec((tq, dv), lambda b, h, i: (b * nq + i, h)),
        out_shape=jax.ShapeDtypeStruct((batch * seq, heads * dv), BF16),
        scratch_shapes=[
            pltpu.VMEM((2, 1, tq), F32),
            pltpu.VMEM((2, 1, tq), F32),
            pltpu.VMEM((2, dv, tq), F32),
            pltpu.VMEM((2, tk, tq), F32),
            pltpu.VMEM((2, tk, tq), F32),
            pltpu.VMEM((2, SUBLANES, tq), F32),
            pltpu.VMEM((2, SUBLANES, tq), F32),
            pltpu.VMEM((2, tk, tq), BF16),
        ],
        compiler_params=_params("parallel", "parallel", "parallel"),
        name="diff_attention",
    )(qk, qk, vt, band, eye, lq1, lk1, lq2, lk2, subw)


def _ssd_body(*refs, reverse, final, heads, col_off):
    if final:
        (z_ref, act_in_ref, dt_ref, dtt_ref, dtb_ref, alog_ref, dtbt_ref, alogt_ref, dskip_ref, nw_ref, yb_ref, o_ref,
         state_ref) = refs
    else:
        (xbc_ref, hp_ref, hn_ref, dt_ref, dtt_ref, cw_ref, cb_ref, dtb_ref, alog_ref, dtbt_ref, alogt_ref, o_ref,
         act_out_ref, u_ref, act_ref, state_ref) = refs
    L = CHUNK
    assert SSM_STATE == CHUNK
    G, N, P = SSM_GROUPS, SSM_STATE, SSM_HEAD_DIM
    R = heads // G
    GW = R * P
    W = heads * P
    CC = W + 2 * G * N
    HALO = SUBLANES
    c = pl.program_id(1)
    nc = pl.num_programs(1)
    pos = (nc - 1 - c) if reverse else c

    @pl.when(c == 0)
    def _():
        state_ref[...] = jnp.zeros(state_ref.shape, F32)

    if final:
        def chan(lo, hi):
            return act_in_ref[:, lo:hi].astype(F32)
    else:
        prev = hp_ref[...].astype(F32)[BF16_ROWS - HALO:, :]
        nxt = hn_ref[...].astype(F32)[:HALO, :]
        u_ref[0:HALO, :] = jnp.where(pos > 0, prev, 0.0)
        u_ref[HALO:HALO + L, :] = xbc_ref[...].astype(F32)
        u_ref[HALO + L:, :] = jnp.where(pos < nc - 1, nxt, 0.0)
        SLAB = 2 * LANES

        def conv_slab(j, carry):
            lo = pl.multiple_of(j * SLAB, SLAB)
            acc = jnp.broadcast_to(cb_ref[:, pl.ds(lo, SLAB)], (L, SLAB))
            for t in range(CONV_WIDTH):
                r0 = HALO - CONV_WIDTH // 2 + t
                acc = acc + cw_ref[t:t + 1, pl.ds(lo, SLAB)] * u_ref[r0:r0 + L, pl.ds(lo, SLAB)]
            act = acc * _sigmoid(acc)
            act_ref[:, pl.ds(lo, SLAB)] = act
            act_out_ref[:, pl.ds(lo, SLAB)] = act.astype(act_out_ref.dtype)
            return carry

        lax.fori_loop(0, CC // SLAB, conv_slab, 0)

        def chan(lo, hi):
            return act_ref[:, lo:hi]

    def softplus(v):
        return jnp.maximum(v, 0.0) + jnp.log1p(jnp.exp(-jnp.abs(v)))

    dtv = softplus(dt_ref[...] + dtb_ref[...])
    a = -jnp.exp(alog_ref[...]) * dtv
    a_t = -jnp.exp(alogt_ref[...]) * softplus(dtt_ref[...] + dtbt_ref[...])
    row = lax.broadcasted_iota(jnp.int32, (L, L), 0)
    col = lax.broadcasted_iota(jnp.int32, (L, L), 1)
    keep = (row <= col) if reverse else (row >= col)
    keep_t = (row >= col) if reverse else (row <= col)
    cum = jnp.dot(keep.astype(F32), a, precision=lax.Precision.HIGHEST, preferred_element_type=F32)
    cum_t = jnp.dot(a_t, keep_t.astype(F32), precision=lax.Precision.HIGHEST, preferred_element_type=F32)
    total = cum[0:1, :] if reverse else cum[L - 1:L, :]
    eye_bf = jnp.where(row == col, 1.0, 0.0).astype(BF16)
    seg = lax.broadcasted_iota(jnp.int32, (L, GW), 1) // P
    seg1 = lax.broadcasted_iota(jnp.int32, (1, GW), 1) // P

    for g in range(G):
        xs = chan(g * GW, (g + 1) * GW)
        bmat = chan(W + g * N, W + (g + 1) * N)
        cmat = chan(W + (G + g) * N, W + (G + g + 1) * N)
        b_bf = bmat.astype(BF16)
        c_bf = cmat.astype(BF16)
        cb = lax.dot_general(c_bf, b_bf, NT_DIMS, preferred_element_type=F32)
        bt_bf = lax.dot_general(eye_bf, b_bf, NT_DIMS, preferred_element_type=F32).astype(BF16)
        dt_e = jnp.zeros((L, GW), F32)
        cum_e = jnp.zeros((L, GW), F32)
        tot_e = jnp.zeros((1, GW), F32)
        for r in range(R):
            hc = col_off + g * R + r
            dt_e = jnp.where(seg == r, dtv[:, hc:hc + 1], dt_e)
            cum_e = jnp.where(seg == r, cum[:, hc:hc + 1], cum_e)
            tot_e = jnp.where(seg1 == r, total[:, hc:hc + 1], tot_e)
        x_dt = xs * dt_e
        y = jnp.zeros((L, GW), F32)
        for r in range(R):
            hc = col_off + g * R + r
            decay = jnp.exp(jnp.where(keep, cum[:, hc:hc + 1] - cum_t[hc:hc + 1, :], -jnp.inf))
            m_h = (cb * decay).astype(BF16)
            x_h = jnp.where(seg == r, x_dt, 0.0).astype(BF16)
            y = y + jnp.dot(m_h, x_h, preferred_element_type=F32)
        s_prev = state_ref[g]
        y = y + jnp.dot(c_bf, s_prev.astype(BF16), preferred_element_type=F32) * jnp.exp(cum_e)
        x_end = (x_dt * jnp.exp(tot_e - cum_e)).astype(BF16)
        state_ref[g] = s_prev * jnp.exp(tot_e) + jnp.dot(bt_bf, x_end, preferred_element_type=F32)
        if final:
            y = y + yb_ref[:, g * GW:(g + 1) * GW].astype(F32) + dskip_ref[:, g * GW:(g + 1) * GW] * xs
            zg = z_ref[:, g * GW:(g + 1) * GW].astype(F32)
            y = y * (zg * _sigmoid(zg))
            ms = jnp.mean(y * y, axis=-1, keepdims=True)
            y = y * lax.rsqrt(ms + EPS) * nw_ref[:, g * GW:(g + 1) * GW]
        o_ref[:, g * GW:(g + 1) * GW] = y.astype(o_ref.dtype)


def ssd_pass(xbc, dt, dt_t, conv_w, conv_b, dt_bias, a_log, extra, *, batch, seq, heads, reverse, final, col_off):
    L = CHUNK
    W = heads * SSM_HEAD_DIM
    CC = W + 2 * SSM_GROUPS * SSM_STATE
    GW = W // SSM_GROUPS
    nc = seq // L
    nhalo = batch * seq // BF16_ROWS
    per = L // BF16_ROWS

    def rb(b, c):
        return b * nc + ((nc - 1 - c) if reverse else c)

    full = lambda shape: pl.BlockSpec(shape, lambda b, c: (0,) * len(shape))
    rows = lambda w: pl.BlockSpec((L, w), lambda b, c: (rb(b, c), 0))
    cols = pl.BlockSpec((LANES, L), lambda b, c: (0, rb(b, c)))
    state = pltpu.VMEM((SSM_GROUPS, SSM_STATE, GW), F32)
    body = functools.partial(_ssd_body, reverse=reverse, final=final, heads=heads, col_off=col_off)
    t = batch * seq
    head_specs = [full((1, LANES)), full((1, LANES)), full((LANES, 1)), full((LANES, 1))]
    head_args = [dt_bias, a_log, dt_bias.reshape(LANES, 1), a_log.reshape(LANES, 1)]
    if final:
        in_specs = [rows(W), rows(CC), rows(LANES), cols] + head_specs + [full((1, W)), full((1, W)), rows(W)]
        args = [extra["z"], extra["act"], dt, dt_t] + head_args + [extra["d_skip"], extra["norm_w"], extra["y_other"]]
        out_specs = rows(W)
        out_shape = jax.ShapeDtypeStruct((t, W), BF16)
        scratch = [state]
    else:
        in_specs = [
            rows(CC),
            pl.BlockSpec((BF16_ROWS, CC), lambda b, c: (jnp.maximum(rb(b, c) * per - 1, 0), 0)),
            pl.BlockSpec((BF16_ROWS, CC), lambda b, c: (jnp.minimum((rb(b, c) + 1) * per, nhalo - 1), 0)),
            rows(LANES), cols,
            full((CONV_WIDTH, CC)), full((1, CC)),
        ] + head_specs
        args = [xbc, xbc, xbc, dt, dt_t, conv_w, conv_b] + head_args
        out_specs = [rows(W), rows(CC)]
        out_shape = [jax.ShapeDtypeStruct((t, W), BF16), jax.ShapeDtypeStruct((t, CC), BF16)]
        scratch = [pltpu.VMEM((L + 2 * SUBLANES, CC), F32), pltpu.VMEM((L, CC), F32), state]
    return pl.pallas_call(
        body,
        grid=(batch, nc),
        in_specs=in_specs,
        out_specs=out_specs,
        out_shape=out_shape,
        scratch_shapes=scratch,
        compiler_params=_params("parallel", "arbitrary"),
        name="ssd_final" if final else "ssd_scan",
    )(*args)


def _router_body(x_ref, nw_ref, wr_ref, br_ref, h_ref, idx_ref, gate_ref, cnt_ref, run_ref):
    tm, d = x_ref.shape
    half = d // 2

    @pl.when(pl.program_id(0) == 0)
    def _():
        run_ref[...] = jnp.zeros(run_ref.shape, F32)

    x = x_ref[...]
    ms = jnp.mean(x * x, axis=-1, keepdims=True)
    h = x * lax.rsqrt(ms + EPS) * nw_ref[...]
    h_ref[...] = _pack_pairs(h[:, :half], h[:, half:])
    h_hi = h.astype(BF16)
    h_lo = (h - h_hi.astype(F32)).astype(BF16)
    hw = jnp.dot(h_hi, wr_ref[...], preferred_element_type=F32)
    logits = (hw[:, :LANES] + (hw[:, LANES:] + jnp.dot(h_lo, wr_ref[:, :LANES], preferred_element_type=F32))
              + br_ref[...])
    lane = lax.broadcasted_iota(jnp.int32, logits.shape, 1).astype(F32)
    ninf = -jnp.inf
    is_g = lane < N_EXPERT_GROUPS
    gl = jnp.where(is_g, logits, ninf)
    gmax = jnp.max(gl, axis=-1, keepdims=True)
    gsum = jnp.sum(jnp.where(is_g, jnp.exp(gl - gmax), 0.0), axis=-1, keepdims=True)
    p_group = 1.0 / gsum
    g_sel = jnp.min(jnp.where(gl == gmax, lane, float(LANES)), axis=-1, keepdims=True)
    lo = N_EXPERT_GROUPS + EXPERTS_PER_GROUP * g_sel
    el = jnp.where((lane >= lo) & (lane < lo + EXPERTS_PER_GROUP), logits, ninf)
    m1 = jnp.max(el, axis=-1, keepdims=True)
    i1 = jnp.min(jnp.where(el == m1, lane, float(LANES)), axis=-1, keepdims=True)
    el2 = jnp.where(lane == i1, ninf, el)
    m2 = jnp.max(el2, axis=-1, keepdims=True)
    i2 = jnp.min(jnp.where(el2 == m2, lane, float(LANES)), axis=-1, keepdims=True)
    r = jnp.exp(m2 - m1)
    g1 = p_group / (1.0 + r)
    g2 = p_group * r / (1.0 + r)
    e1 = i1 - N_EXPERT_GROUPS
    e2 = i2 - N_EXPERT_GROUPS
    oh1 = jnp.where(lane == e1, 1.0, 0.0)
    oh2 = jnp.where(lane == e2, 1.0, 0.0)
    both = oh1 + oh2
    ri = lax.broadcasted_iota(jnp.int32, (tm, tm), 0)
    ci = lax.broadcasted_iota(jnp.int32, (tm, tm), 1)
    earlier = jnp.where(ci < ri, 1.0, 0.0).astype(BF16)
    before = jnp.dot(earlier, both.astype(BF16), preferred_element_type=F32) + run_ref[...]
    rank1 = jnp.sum(oh1 * before, axis=-1, keepdims=True)
    rank2 = jnp.sum(oh2 * before, axis=-1, keepdims=True)
    run_ref[...] = run_ref[...] + jnp.sum(both, axis=0, keepdims=True)
    cnt_ref[...] = run_ref[...]
    packed = jnp.where(lane == 0, e1, jnp.where(lane == 1, e2, jnp.where(lane == 2, rank1, jnp.where(lane == 3, rank2, 0.0))))
    idx_ref[...] = packed.astype(jnp.int32)
    gate_ref[...] = jnp.where(lane == 0, g1, jnp.where(lane == 1, g2, 0.0))


def norm_and_route(x, norm_w, w_router, b_router, tm):
    t, d = x.shape
    row = lambda w: pl.BlockSpec((tm, w), lambda i: (i, 0))
    one = lambda w: pl.BlockSpec((1, w), lambda i: (0, 0))
    w_hi = w_router.astype(BF16)
    w_router = jnp.concatenate([w_hi, (w_router - w_hi.astype(F32)).astype(BF16)], axis=1)
    return pl.pallas_call(
        _router_body,
        grid=(t // tm,),
        in_specs=[row(d), one(d), pl.BlockSpec((d, 2 * LANES), lambda i: (0, 0)), one(LANES)],
        out_specs=[row(d // 2), row(LANES), row(LANES), one(LANES)],
        out_shape=[jax.ShapeDtypeStruct((t, d // 2), U32), jax.ShapeDtypeStruct((t, LANES), jnp.int32),
                   jax.ShapeDtypeStruct((t, LANES), F32), jax.ShapeDtypeStruct((1, LANES), F32)],
        scratch_shapes=[pltpu.VMEM((1, LANES), F32)],
        compiler_params=_params("arbitrary"),
        name="router",
    )(x, norm_w.reshape(1, d).astype(F32), w_router, b_router)


def _moe_plan(idx, counts, tm):
    t = idx.shape[0]
    n = t * TOP_K
    experts = jnp.arange(N_EXPERTS, dtype=I32)
    counts = counts[0, :N_EXPERTS].astype(I32)
    padded = (counts + tm - 1) // tm * tm
    pend = jnp.cumsum(padded)
    pstart = pend - padded
    e = idx[:, :TOP_K]
    rank = idx[:, TOP_K:2 * TOP_K]
    dest = rank + jnp.sum(jnp.where(e[:, :, None] == experts, pstart, 0), axis=-1)
    nblk = n // tm + N_EXPERTS
    n_used = pend[-1] // tm
    blk = jnp.arange(nblk, dtype=I32)
    blk_e = jnp.minimum(jnp.sum((pend[None, :] <= (blk * tm)[:, None]).astype(I32), axis=-1), N_EXPERTS - 1)
    last_e = jnp.sum(jnp.where(blk == n_used - 1, blk_e, 0))
    blk_e = jnp.where(blk < n_used, blk_e, last_e)
    later = (counts[None, :] > 0) & (experts[None, :] > experts[:, None])
    next_of = jnp.min(jnp.where(later, experts[None, :], N_EXPERTS), axis=1)
    next_of = jnp.where(next_of < N_EXPERTS, next_of, -1)
    blk_next = jnp.sum(jnp.where(blk_e[:, None] == experts, next_of[None, :], 0), axis=1)
    return (dest.astype(I32), pend.astype(I32), blk_e.astype(I32), blk_next.astype(I32),
            n_used.reshape(1).astype(I32), nblk * tm)


def _dispatch_body(pend_ref, nu_ref, dest_ref, h_ref, xs_hbm, idx_smem, zero_ref, sem, isem, *, tt, tm):
    i = pl.program_id(0)

    @pl.when(i == 0)
    def _():
        zero_ref[...] = jnp.zeros(zero_ref.shape, zero_ref.dtype)

        def zero_block(row0):
            cp = pltpu.make_async_copy(zero_ref, xs_hbm.at[pl.ds(pl.multiple_of(row0, tm), tm), :], sem)
            cp.start()
            cp.wait()

        def fill(e, c):
            end = pend_ref[e]
            prev = jnp.where(e > 0, pend_ref[jnp.maximum(e - 1, 0)], 0)

            @pl.when(end > prev)
            def _():
                zero_block(end - tm)
            return c

        lax.fori_loop(0, N_EXPERTS, fill, 0)

        def fill_unused(b, c):
            zero_block(b * tm)
            return c

        lax.fori_loop(nu_ref[0], xs_hbm.shape[0] // tm, fill_unused, 0)

    cp = pltpu.make_async_copy(dest_ref.at[0, 0], idx_smem, isem)
    cp.start()
    cp.wait()

    def issue(r, c):
        for k in range(TOP_K):
            d = idx_smem[TOP_K * r + k]
            pltpu.make_async_copy(h_ref.at[pl.ds(r, 1), :], xs_hbm.at[pl.ds(d, 1), :], sem).start(priority=k)
        return c

    lax.fori_loop(0, tt, issue, 0, unroll=8)
    all_rows = xs_hbm.at[pl.ds(0, TOP_K * tt), :]
    pltpu.make_async_copy(all_rows, all_rows, sem).wait()


def dispatch_rows(pend, n_used, dest, h_packed, rows, tt, tm):
    t, w = h_packed.shape
    return pl.pallas_call(
        functools.partial(_dispatch_body, tt=tt, tm=tm),
        grid_spec=pltpu.PrefetchScalarGridSpec(
            num_scalar_prefetch=2,
            grid=(t // tt,),
            in_specs=[
                pl.BlockSpec((1, 1, TOP_K * tt), lambda i, pe, nu: (i, 0, 0)),
                pl.BlockSpec((tt, w), lambda i, pe, nu: (i, 0)),
            ],
            out_specs=pl.BlockSpec(memory_space=pl.ANY),
            scratch_shapes=[
                pltpu.SMEM((TOP_K * tt,), I32),
                pltpu.VMEM((tm, w), U32),
                pltpu.SemaphoreType.DMA,
                pltpu.SemaphoreType.DMA,
            ],
        ),
        out_shape=jax.ShapeDtypeStruct((rows, w), U32),
        compiler_params=_params("arbitrary"),
        name="dispatch_rows",
    )(pend, n_used, dest.reshape(t // tt, 1, TOP_K * tt), h_packed)


def _expert_changed(be_ref, i):
    return (i == 0) | (be_ref[i] != be_ref[jnp.maximum(i - 1, 0)])


CAST_ROWS = 256


def _cast_rows(src_ref, dst_ref):
    def body(r, carry):
        rows = pl.ds(pl.multiple_of(r * CAST_ROWS, CAST_ROWS), CAST_ROWS)
        dst_ref[rows, :] = src_ref[rows, :].astype(dst_ref.dtype)
        return carry
    lax.fori_loop(0, src_ref.shape[0] // CAST_ROWS, body, 0)


def _stream_weights(be_ref, nx_ref, i, sweep, n_sweeps, copies, on_ready, cnt_ref):
    @pl.when((sweep == 0) & (i == 0))
    def _():
        cnt_ref[0] = 0
        for cp in copies(be_ref[0], 0, 0):
            cp.start()

    @pl.when(_expert_changed(be_ref, i))
    def _():
        slot = cnt_ref[0] & 1
        for cp in copies(be_ref[i], sweep, slot):
            cp.wait()
        on_ready(slot)
        nxt = nx_ref[i]

        @pl.when(nxt >= 0)
        def _():
            for cp in copies(nxt, sweep, 1 - slot):
                cp.start()

        @pl.when((nxt < 0) & (sweep + 1 < n_sweeps))
        def _():
            for cp in copies(be_ref[0], sweep + 1, 1 - slot):
                cp.start()

        cnt_ref[0] = cnt_ref[0] + 1


def _gate_up_body(be_ref, nx_ref, nu_ref, x_ref, wg_hbm, wu_hbm, o_ref, wbuf_ref, wgb_ref, wub_ref, sem_ref, cnt_ref,
                  *, tf):
    j = pl.program_id(0)
    i = pl.program_id(1)
    half = x_ref.shape[1]

    def copies(e, jj, slot):
        cols = pl.ds(pl.multiple_of(jj * tf, tf), tf)
        return (pltpu.make_async_copy(wg_hbm.at[e, :, cols], wbuf_ref.at[slot, 0], sem_ref.at[slot]),
                pltpu.make_async_copy(wu_hbm.at[e, :, cols], wbuf_ref.at[slot, 1], sem_ref.at[slot]))

    def on_ready(slot):
        _cast_rows(wbuf_ref.at[slot, 0], wgb_ref)
        _cast_rows(wbuf_ref.at[slot, 1], wub_ref)

    _stream_weights(be_ref, nx_ref, i, j, pl.num_programs(0), copies, on_ready, cnt_ref)

    @pl.when(i < nu_ref[0])
    def _():
        x_lo, x_hi = _unpack_pairs(x_ref[...])
        x_lo = x_lo.astype(BF16)
        x_hi = x_hi.astype(BF16)

        def mm(w_ref):
            return (jnp.dot(x_lo, w_ref[:half, :], preferred_element_type=F32)
                    + jnp.dot(x_hi, w_ref[half:, :], preferred_element_type=F32))

        a = mm(wgb_ref)
        b = mm(wub_ref)
        o_ref[...] = (a * _sigmoid(a) * b).astype(o_ref.dtype)

    @pl.when(i >= nu_ref[0])
    def _():
        o_ref[...] = jnp.zeros(o_ref.shape, o_ref.dtype)


def expert_gate_up(blk_e, blk_next, n_used, xs, w_gate, w_up, tm, tf):
    rows = xs.shape[0]
    d, f = w_gate.shape[1], w_gate.shape[2]
    nblk = rows // tm
    last = lambda i, nu: jnp.minimum(i, nu[0] - 1)
    hbm = pl.BlockSpec(memory_space=pl.ANY)
    return pl.pallas_call(
        functools.partial(_gate_up_body, tf=tf),
        grid_spec=pltpu.PrefetchScalarGridSpec(
            num_scalar_prefetch=3,
            grid=(f // tf, nblk),
            in_specs=[pl.BlockSpec((tm, d // 2), lambda j, i, be, nx, nu: (last(i, nu), 0)), hbm, hbm],
            out_specs=pl.BlockSpec((tm, tf), lambda j, i, be, nx, nu: (i, j)),
            scratch_shapes=[
                pltpu.VMEM((2, 2, d, tf), F32),
                pltpu.VMEM((d, tf), BF16),
                pltpu.VMEM((d, tf), BF16),
                pltpu.SemaphoreType.DMA((2,)),
                pltpu.SMEM((1,), I32),
            ],
        ),
        out_shape=jax.ShapeDtypeStruct((rows, f), BF16),
        compiler_params=_params("arbitrary", "arbitrary"),
        name="expert_gate_up",
    )(blk_e, blk_next, n_used, xs, w_gate, w_up)


def _down_body(be_ref, nx_ref, nu_ref, a_ref, wd_hbm, o_ref, wbuf_ref, wdb_ref, sem_ref, cnt_ref):
    i = pl.program_id(0)
    half = o_ref.shape[1]

    def copies(e, sweep, slot):
        return (pltpu.make_async_copy(wd_hbm.at[e], wbuf_ref.at[slot], sem_ref.at[slot]),)

    def on_ready(slot):
        _cast_rows(wbuf_ref.at[slot], wdb_ref)

    _stream_weights(be_ref, nx_ref, i, 0, 1, copies, on_ready, cnt_ref)

    @pl.when(i < nu_ref[0])
    def _():
        y = jnp.dot(a_ref[...], wdb_ref[...], preferred_element_type=F32)
        o_ref[...] = _pack_pairs(y[:, :half], y[:, half:])

    @pl.when(i >= nu_ref[0])
    def _():
        o_ref[...] = jnp.zeros(o_ref.shape, o_ref.dtype)


def expert_down(blk_e, blk_next, n_used, act, w_down, tm):
    rows, f = act.shape
    d = w_down.shape[2]
    nblk = rows // tm
    last = lambda i, nu: jnp.minimum(i, nu[0] - 1)
    return pl.pallas_call(
        _down_body,
        grid_spec=pltpu.PrefetchScalarGridSpec(
            num_scalar_prefetch=3,
            grid=(nblk,),
            in_specs=[
                pl.BlockSpec((tm, f), lambda i, be, nx, nu: (last(i, nu), 0)),
                pl.BlockSpec(memory_space=pl.ANY),
            ],
            out_specs=pl.BlockSpec((tm, d // 2), lambda i, be, nx, nu: (i, 0)),
            scratch_shapes=[
                pltpu.VMEM((2, f, d), F32),
                pltpu.VMEM((f, d), BF16),
                pltpu.SemaphoreType.DMA((2,)),
                pltpu.SMEM((1,), I32),
            ],
        ),
        out_shape=jax.ShapeDtypeStruct((rows, d // 2), U32),
        compiler_params=_params("arbitrary"),
        name="expert_down",
    )(blk_e, blk_next, n_used, act, w_down)


def _combine_body(dest_ref, x_ref, g_ref, w_ref, y_hbm, o_ref, idx_smem, buf_ref, sem, isem, *, tt):
    d = x_ref.shape[1]
    half = d // 2
    cp = pltpu.make_async_copy(dest_ref.at[0, 0], idx_smem, isem)
    cp.start()
    cp.wait()

    def issue(r, c):
        for k in range(TOP_K):
            src = idx_smem[TOP_K * r + k]
            pltpu.make_async_copy(y_hbm.at[pl.ds(src, 1), :], buf_ref.at[k, pl.ds(r, 1), :], sem).start(priority=k)
        return c

    lax.fori_loop(0, tt, issue, 0, unroll=8)
    pltpu.make_async_copy(buf_ref, buf_ref, sem).wait()

    y0_lo, y0_hi = _unpack_pairs(buf_ref[0])
    y1_lo, y1_hi = _unpack_pairs(buf_ref[1])
    g0 = g_ref[:, 0:1]
    g1 = g_ref[:, 1:2]
    x_lo = x_ref[:, :half] + (g0 * y0_lo + g1 * y1_lo)
    x_hi = x_ref[:, half:] + (g0 * y0_hi + g1 * y1_hi)
    ms = (jnp.sum(x_lo * x_lo, axis=-1, keepdims=True) + jnp.sum(x_hi * x_hi, axis=-1, keepdims=True)) / d
    inv = lax.rsqrt(ms + EPS)
    o_ref[:, :half] = x_lo * inv * w_ref[:, :half]
    o_ref[:, half:] = x_hi * inv * w_ref[:, half:]


def combine_and_norm(x, y_packed, dest, gate, w, tt):
    t, d = x.shape
    row = lambda wd: pl.BlockSpec((tt, wd), lambda i: (i, 0))
    return pl.pallas_call(
        functools.partial(_combine_body, tt=tt),
        grid=(t // tt,),
        in_specs=[
            pl.BlockSpec((1, 1, TOP_K * tt), lambda i: (i, 0, 0)),
            row(d), row(LANES), pl.BlockSpec((1, d), lambda i: (0, 0)),
            pl.BlockSpec(memory_space=pl.ANY),
        ],
        out_specs=row(d),
        out_shape=jax.ShapeDtypeStruct((t, d), F32),
        scratch_shapes=[
            pltpu.SMEM((TOP_K * tt,), I32),
            pltpu.VMEM((TOP_K, tt, d // 2), U32),
            pltpu.SemaphoreType.DMA,
            pltpu.SemaphoreType.DMA,
        ],
        compiler_params=_params("arbitrary"),
        name="combine_norm",
    )(dest.reshape(t // tt, 1, TOP_K * tt), x, gate, w.reshape(1, d).astype(F32), y_packed)


def _tiles(batch, seq):
    t = batch * seq
    return dict(
        norm_tm=min(512, t),
        mm_tm=min(1024, t),
        mm_tn=1024,
        out_tm=min(512, t),
        out_tn=1024,
        attn_tq=min(1024, seq),
        attn_tk=min(512, seq),
        moe_tm=min(256, t),
        moe_tf=512,
        disp_tt=min(1024, t),
        comb_tt=min(512, t),
    )


def _lambda_init_at(layer):
    return 0.8 - 0.6 * math.exp(-0.3 * layer)


def kernel(x, rel_bias, norm1_w, w_in, lambda_q1, lambda_k1, lambda_q2, lambda_k2, subln_w, conv_w, conv_b,
           dt_bias_f, dt_bias_b, a_log_f, a_log_b, d_skip, ssm_norm_w, w_out, norm2_w, w_group_router,
           b_group_router, w_expert_router, b_expert_router, w_gate, w_up, w_down, final_norm_w):
    batch, seq, d = x.shape
    t = batch * seq
    depth = norm1_w.shape[0]
    attn_w = d // 2
    ssm_w = d - attn_w
    dv = 2 * ATTN_HEAD_DIM
    a_heads = attn_w // dv
    s_heads = ssm_w // SSM_HEAD_DIM
    cc = ssm_w + 2 * SSM_GROUPS * SSM_STATE
    main_w = 3 * attn_w + ssm_w + cc
    assert ssm_w == attn_w and cc == 2 * ssm_w and 2 * s_heads <= LANES
    tl = _tiles(batch, seq)
    log2e = math.log2(math.e)
    band = _band_table(rel_bias, tl["attn_tq"], tl["attn_tk"], log2e)
    q_scale = jnp.where(jnp.arange(w_in.shape[2]) < attn_w, ATTN_HEAD_DIM ** -0.5 * log2e, 1.0).astype(F32)

    def pad_lanes(v):
        return jnp.pad(v.astype(F32), (0, LANES - v.shape[0])).reshape(1, LANES)

    xf = x.reshape(t, d)
    for layer in range(depth):
        lam_init = _lambda_init_at(layer)
        w_t = (jnp.transpose(w_in[layer]) * q_scale[:, None]).astype(BF16)
        w_dt = jnp.pad(w_t[main_w:], ((0, LANES - 2 * s_heads), (0, 0)))
        h = rmsnorm_rows(xf, norm1_w[layer], BF16, tl["norm_tm"])
        tm, tn = tl["mm_tm"], tl["mm_tn"]
        qk = matmul_nt(h, w_t, BF16, t, 2 * attn_w, tm, tn, "in_proj_qk")
        v_t = matmul_nt(w_t, h, BF16, attn_w, t, tn, tm, "in_proj_v", a_blk0=2 * attn_w // tn)
        z, dt, dt_t = projection_with_dt(h, w_t, w_dt, BF16, t, ssm_w, tm, tn, "in_proj_z_dt",
                                         b_blk0=3 * attn_w // tn)
        xbc = matmul_nt(h, w_t, BF16, t, cc, tm, tn, "in_proj_xbc", b_blk0=(3 * attn_w + ssm_w) // tn)

        vec = lambda v: v[layer].reshape(1, -1).astype(F32)
        attn = diff_attention(qk, v_t, band, vec(lambda_q1), vec(lambda_k1), vec(lambda_q2), vec(lambda_k2),
                              subln_w[layer].reshape(dv, 1).astype(F32), batch=batch, seq=seq, heads=a_heads,
                              lam_init=lam_init, tq=tl["attn_tq"], tk=tl["attn_tk"])

        dt_bias = pad_lanes(jnp.concatenate([dt_bias_f[layer], dt_bias_b[layer]]))
        a_log = pad_lanes(jnp.concatenate([a_log_f[layer], a_log_b[layer]]))
        cw = conv_w[layer].astype(F32)
        cb = conv_b[layer].reshape(1, cc).astype(F32)
        common = dict(batch=batch, seq=seq, heads=s_heads)
        y_bwd, act = ssd_pass(xbc, dt, dt_t, cw, cb, dt_bias, a_log, None, reverse=True, final=False,
                              col_off=s_heads, **common)
        extra = dict(d_skip=jnp.repeat(d_skip[layer].astype(F32), SSM_HEAD_DIM).reshape(1, ssm_w),
                     norm_w=vec(ssm_norm_w), y_other=y_bwd, act=act, z=z)
        ssm = ssd_pass(None, dt, dt_t, cw, cb, dt_bias, a_log, extra, reverse=False, final=True, col_off=0, **common)

        x1 = out_projection(attn, ssm, w_out[layer].astype(BF16), xf, tl["out_tm"], tl["out_tn"])

        w_router = jnp.pad(jnp.concatenate([w_group_router[layer], w_expert_router[layer]], axis=1).astype(F32),
                           ((0, 0), (0, LANES - N_EXPERT_GROUPS - N_EXPERTS)))
        b_router = pad_lanes(jnp.concatenate([b_group_router[layer], b_expert_router[layer]]))
        h2, idx, gate, counts = norm_and_route(x1, norm2_w[layer], w_router, b_router, tl["norm_tm"])
        tm = tl["moe_tm"]
        dest, pend, blk_e, blk_next, n_used, rows = _moe_plan(idx, counts, tm)
        xs = dispatch_rows(pend, n_used, dest, h2, rows, tl["disp_tt"], tm)
        act = expert_gate_up(blk_e, blk_next, n_used, xs, w_gate[layer], w_up[layer], tm, tl["moe_tf"])
        y = expert_down(blk_e, blk_next, n_used, act, w_down[layer], tm)
        if layer + 1 < depth:
            raise NotImplementedError("multi-layer stacking needs an un-normalised combine")
        xf = combine_and_norm(x1, y, dest, gate, final_norm_w, tl["comb_tt"])
    return xf.reshape(batch, seq, d)
```

```python
import functools
import math

import jax
import jax.numpy as jnp
from jax import lax
from jax.experimental import pallas as pl
from jax.experimental.pallas import tpu as pltpu

F32 = jnp.float32
BF16 = jnp.bfloat16
U32 = jnp.uint32
I32 = jnp.int32
EPS = 1e-6

ATTN_HEAD_DIM = 128
NUM_BUCKETS = 32
MAX_DISTANCE = 128
SSM_HEAD_DIM = 64
SSM_GROUPS = 8
SSM_STATE = 128
CONV_WIDTH = 5
CHUNK = 128
N_EXPERT_GROUPS = 4
EXPERTS_PER_GROUP = 8
N_EXPERTS = N_EXPERT_GROUPS * EXPERTS_PER_GROUP
TOP_K = 2

LANES = 128
SUBLANES = 8
BF16_ROWS = 16
VMEM_LIMIT = 56 * 1024 * 1024
HI16 = 0xFFFF0000


def _params(*sem):
    return pltpu.CompilerParams(dimension_semantics=sem, vmem_limit_bytes=VMEM_LIMIT)


def _sigmoid(x):
    return 1.0 / (1.0 + jnp.exp(-x))


def _pack_pairs(lo, hi):
    lo_b = lax.bitcast_convert_type(lo.astype(BF16).astype(F32), U32)
    hi_b = lax.bitcast_convert_type(hi.astype(BF16).astype(F32), U32)
    return (lo_b >> 16) | (hi_b & U32(HI16))


def _unpack_pairs(w):
    return lax.bitcast_convert_type(w << 16, F32), lax.bitcast_convert_type(w & U32(HI16), F32)


def _rmsnorm_body(x_ref, w_ref, o_ref):
    x = x_ref[...]
    ms = jnp.mean(x * x, axis=-1, keepdims=True)
    o_ref[...] = (x * lax.rsqrt(ms + EPS) * w_ref[...]).astype(o_ref.dtype)


def rmsnorm_rows(x, w, out_dtype, tm):
    t, d = x.shape
    return pl.pallas_call(
        _rmsnorm_body,
        grid=(t // tm,),
        in_specs=[pl.BlockSpec((tm, d), lambda i: (i, 0)), pl.BlockSpec((1, d), lambda i: (0, 0))],
        out_specs=pl.BlockSpec((tm, d), lambda i: (i, 0)),
        out_shape=jax.ShapeDtypeStruct((t, d), out_dtype),
        compiler_params=_params("parallel"),
        name="rmsnorm",
    )(x, w.reshape(1, d).astype(F32))


NT_DIMS = (((1,), (1,)), ((), ()))


def _matmul_nt_body(a_ref, b_ref, o_ref):
    o_ref[...] = lax.dot_general(a_ref[...], b_ref[...], NT_DIMS, preferred_element_type=F32).astype(o_ref.dtype)


def matmul_nt(a, b, out_dtype, m, n, tm, tn, name, a_blk0=0, b_blk0=0):
    k = a.shape[1]
    assert b.shape[1] == k and m % tm == 0 and n % tn == 0
    return pl.pallas_call(
        _matmul_nt_body,
        grid=(m // tm, n // tn),
        in_specs=[pl.BlockSpec((tm, k), lambda i, j: (a_blk0 + i, 0)),
                  pl.BlockSpec((tn, k), lambda i, j: (b_blk0 + j, 0))],
        out_specs=pl.BlockSpec((tm, tn), lambda i, j: (i, j)),
        out_shape=jax.ShapeDtypeStruct((m, n), out_dtype),
        compiler_params=_params("parallel", "parallel"),
        name=name,
    )(a, b)


def _proj_with_dt_body(a_ref, b_ref, wdt_ref, o_ref, dt_ref, dtt_ref):
    o_ref[...] = lax.dot_general(a_ref[...], b_ref[...], NT_DIMS, preferred_element_type=F32).astype(o_ref.dtype)

    @pl.when(pl.program_id(1) == 0)
    def _():
        dt = lax.dot_general(a_ref[...], wdt_ref[...], NT_DIMS, preferred_element_type=F32)
        dt_ref[...] = dt
        dtt_ref[...] = dt.T


def projection_with_dt(a, b, w_dt, out_dtype, m, n, tm, tn, name, b_blk0):
    k = a.shape[1]
    nd = w_dt.shape[0]
    return pl.pallas_call(
        _proj_with_dt_body,
        grid=(m // tm, n // tn),
        in_specs=[pl.BlockSpec((tm, k), lambda i, j: (i, 0)),
                  pl.BlockSpec((tn, k), lambda i, j: (b_blk0 + j, 0)),
                  pl.BlockSpec((nd, k), lambda i, j: (0, 0))],
        out_specs=[pl.BlockSpec((tm, tn), lambda i, j: (i, j)),
                   pl.BlockSpec((tm, nd), lambda i, j: (i, 0)),
                   pl.BlockSpec((nd, tm), lambda i, j: (0, i))],
        out_shape=[jax.ShapeDtypeStruct((m, n), out_dtype), jax.ShapeDtypeStruct((m, nd), F32),
                   jax.ShapeDtypeStruct((nd, m), F32)],
        compiler_params=_params("parallel", "arbitrary"),
        name=name,
    )(a, b, w_dt)


def _outproj_body(a_ref, s_ref, wa_ref, ws_ref, r_ref, o_ref):
    acc = jnp.dot(a_ref[...], wa_ref[...], preferred_element_type=F32)
    acc = acc + jnp.dot(s_ref[...], ws_ref[...], preferred_element_type=F32)
    o_ref[...] = r_ref[...] + acc


def out_projection(attn, ssm, w, resid, tm, tn):
    m, ka = attn.shape
    ks = ssm.shape[1]
    assert ka == ks
    n = w.shape[1]
    return pl.pallas_call(
        _outproj_body,
        grid=(n // tn, m // tm),
        in_specs=[
            pl.BlockSpec((tm, ka), lambda j, i: (i, 0)),
            pl.BlockSpec((tm, ks), lambda j, i: (i, 0)),
            pl.BlockSpec((ka, tn), lambda j, i: (0, j)),
            pl.BlockSpec((ks, tn), lambda j, i: (1, j)),
            pl.BlockSpec((tm, tn), lambda j, i: (i, j)),
        ],
        out_specs=pl.BlockSpec((tm, tn), lambda j, i: (i, j)),
        out_shape=jax.ShapeDtypeStruct((m, n), F32),
        compiler_params=_params("parallel", "parallel"),
        name="out_proj",
    )(attn, ssm, w, w, resid)


def _t5_bucket(rel):
    half = NUM_BUCKETS // 2
    max_exact = half // 2
    n = jnp.abs(rel)
    large = max_exact + (
        jnp.log(jnp.maximum(n, 1).astype(F32) / max_exact) / math.log(MAX_DISTANCE / max_exact) * (half - max_exact)
    ).astype(jnp.int32)
    large = jnp.minimum(large, half - 1)
    return jnp.where(rel > 0, half, 0) + jnp.where(n < max_exact, n, large)


def _band_body(v_ref, o_ref, *, tq, w):
    npad = v_ref.shape[-1]
    x = jnp.broadcast_to(v_ref[0], (tq, npad))
    y = pltpu.roll(x, npad - (tq - 1), 1, stride=1, stride_axis=0)
    o_ref[0] = y[:, :w].T


def _band_table(rel_bias, tq, tk, scale):
    assert tk + 1 >= MAX_DISTANCE
    heads = rel_bias.shape[1]
    w = tq + 4 * tk
    n = w + tq - 1
    npad = -(-n // LANES) * LANES
    rel = jnp.arange(npad, dtype=jnp.int32) - (2 * tk + tq - 1)
    v = (rel_bias[_t5_bucket(rel)].astype(F32) * scale).T.reshape(heads, 1, npad)
    return pl.pallas_call(
        functools.partial(_band_body, tq=tq, w=w),
        grid=(heads,),
        in_specs=[pl.BlockSpec((1, 1, npad), lambda h: (h, 0, 0))],
        out_specs=pl.BlockSpec((1, w, tq), lambda h: (h, 0, 0)),
        out_shape=jax.ShapeDtypeStruct((heads, w, tq), F32),
        compiler_params=_params("parallel"),
        name="band_table",
    )(v)


ATTN_STRIP = 16


def _attn_body(q_ref, k_ref, vt_ref, band_ref, eye_ref, lq1_ref, lk1_ref, lq2_ref, lk2_ref, subw_ref, o_ref,
               m_ref, l_ref, acc_ref, sa_ref, sb_ref, mxa_ref, mxb_ref, p_ref, *, tq, tk, nk, lam_init):
    dh = ATTN_HEAD_DIM
    rs = ATTN_STRIP
    qi = pl.program_id(2)

    m_ref[...] = jnp.full(m_ref.shape, -jnp.inf, F32)
    l_ref[...] = jnp.zeros(l_ref.shape, F32)
    acc_ref[...] = jnp.zeros(acc_ref.shape, F32)

    def scores(kc, s_ref, mx_ref):
        k0 = pl.multiple_of(kc * tk, tk)
        start = pl.multiple_of(jnp.clip(kc * tk - qi * tq + 2 * tk, 0, tq + 3 * tk), LANES)
        for mi in range(2):
            kk = k_ref[pl.ds(k0, tk), mi * dh:(mi + 1) * dh]
            q = q_ref[:, mi * dh:(mi + 1) * dh]
            s = lax.dot_general(kk, q, NT_DIMS, preferred_element_type=F32) + band_ref[0, pl.ds(start, tk), :]
            s_ref[mi] = s
            mx = s[0:SUBLANES]
            for i in range(1, tk // SUBLANES):
                mx = jnp.maximum(mx, s[i * SUBLANES:(i + 1) * SUBLANES])
            mx_ref[mi] = mx

    def softmax_pv(kc, s_ref, mx_ref):
        k0 = pl.multiple_of(kc * tk, tk)
        for mi in range(2):
            m_prev = m_ref[mi]
            m_new = jnp.maximum(m_prev, jnp.max(mx_ref[mi], axis=0, keepdims=True))
            alpha = jnp.exp2(m_prev - m_new)

            def strip_exp(i, ls):
                p = jnp.exp2(s_ref[mi, pl.ds(i * rs, rs), :] - m_new)
                p_ref[mi, pl.ds(i * rs, rs), :] = p.astype(BF16)
                return ls + p

            ls = lax.fori_loop(0, tk // rs, strip_exp, jnp.zeros((rs, tq), F32), unroll=True)
            l_ref[mi] = alpha * l_ref[mi] + jnp.sum(ls, axis=0, keepdims=True)
            m_ref[mi] = m_new
            pv = jnp.dot(vt_ref[:, pl.ds(k0, tk)], p_ref[mi], preferred_element_type=F32)
            acc_ref[mi] = acc_ref[mi] * alpha + pv

    scores(0, sa_ref, mxa_ref)
    if nk > 1:
        assert nk % 2 == 0

        def pair(j, carry):
            scores(2 * j + 1, sb_ref, mxb_ref)
            softmax_pv(2 * j, sa_ref, mxa_ref)
            scores(2 * j + 2, sa_ref, mxa_ref)
            softmax_pv(2 * j + 1, sb_ref, mxb_ref)
            return carry

        lax.fori_loop(0, nk // 2 - 1, pair, 0)
        scores(nk - 1, sb_ref, mxb_ref)
        softmax_pv(nk - 2, sa_ref, mxa_ref)
        softmax_pv(nk - 1, sb_ref, mxb_ref)
    else:
        softmax_pv(0, sa_ref, mxa_ref)

    lam = (jnp.exp(jnp.sum(lq1_ref[...] * lk1_ref[...], axis=-1, keepdims=True))
           - jnp.exp(jnp.sum(lq2_ref[...] * lk2_ref[...], axis=-1, keepdims=True)) + lam_init)
    o = acc_ref[0] / l_ref[0] - lam * (acc_ref[1] / l_ref[1])
    ms = jnp.mean(o * o, axis=0, keepdims=True)
    o = (o * lax.rsqrt(ms + EPS) * subw_ref[...] * (1.0 - lam_init)).astype(BF16)
    o_ref[...] = lax.dot_general(eye_ref[...], o, NT_DIMS, preferred_element_type=F32).astype(o_ref.dtype)


def diff_attention(qk, vt, band, lq1, lk1, lq2, lk2, subw, *, batch, seq, heads, lam_init, tq, tk):
    dv = 2 * ATTN_HEAD_DIM
    nq, nk = seq // tq, seq // tk
    assert tk % ATTN_STRIP == 0
    vec = pl.BlockSpec((1, ATTN_HEAD_DIM), lambda b, h, i: (0, 0))
    body = functools.partial(_attn_body, tq=tq, tk=tk, nk=nk, lam_init=lam_init)
    eye = jnp.eye(tq, dtype=BF16)
    return pl.pallas_call(
        body,
        grid=(batch, heads, nq),
        in_specs=[
            pl.BlockSpec((tq, dv), lambda b, h, i: (b * nq + i, h)),
            pl.BlockSpec((seq, dv), lambda b, h, i: (b, heads + h)),
            pl.BlockSpec((dv, seq), lambda b, h, i: (h, b)),
            pl.BlockSpec((1, tq + 4 * tk, tq), lambda b, h, i: (h, 0, 0)),
            pl.BlockSpec((tq, tq), lambda b, h, i: (0, 0)),
            vec, vec, vec, vec,
            pl.BlockSpec((dv, 1), lambda b, h, i: (0, 0)),
        ],
        out_specs=pl.BlockSpec((tq, dv), lambda b, h, i: (b * nq + i, h)),
        out_shape=jax.ShapeDtypeStruct((batch * seq, heads * dv), BF16),
        scratch_shapes=[
            pltpu.VMEM((2, 1, tq), F32),
            pltpu.VMEM((2, 1, tq), F32),
            pltpu.VMEM((2, dv, tq), F32),
            pltpu.VMEM((2, tk, tq), F32),
            pltpu.VMEM((2, tk, tq), F32),
            pltpu.VMEM((2, SUBLANES, tq), F32),
            pltpu.VMEM((2, SUBLANES, tq), F32),
            pltpu.VMEM((2, tk, tq), BF16),
        ],
        compiler_params=_params("parallel", "parallel", "parallel"),
        name="diff_attention",
    )(qk, qk, vt, band, eye, lq1, lk1, lq2, lk2, subw)


def _ssd_body(*refs, reverse, final, heads, col_off):
    if final:
        (z_ref, act_in_ref, dt_ref, dtt_ref, dtb_ref, alog_ref, dtbt_ref, alogt_ref, dskip_ref, nw_ref, yb_ref, o_ref,
         state_ref) = refs
    else:
        (xbc_ref, hp_ref, hn_ref, dt_ref, dtt_ref, cw_ref, cb_ref, dtb_ref, alog_ref, dtbt_ref, alogt_ref, o_ref,
         act_out_ref, u_ref, act_ref, state_ref) = refs
    L = CHUNK
    assert SSM_STATE == CHUNK
    G, N, P = SSM_GROUPS, SSM_STATE, SSM_HEAD_DIM
    R = heads // G
    GW = R * P
    W = heads * P
    CC = W + 2 * G * N
    HALO = SUBLANES
    c = pl.program_id(1)
    nc = pl.num_programs(1)
    pos = (nc - 1 - c) if reverse else c

    @pl.when(c == 0)
    def _():
        state_ref[...] = jnp.zeros(state_ref.shape, F32)

    if final:
        def chan(lo, hi):
            return act_in_ref[:, lo:hi].astype(F32)
    else:
        prev = hp_ref[...].astype(F32)[BF16_ROWS - HALO:, :]
        nxt = hn_ref[...].astype(F32)[:HALO, :]
        u_ref[0:HALO, :] = jnp.where(pos > 0, prev, 0.0)
        u_ref[HALO:HALO + L, :] = xbc_ref[...].astype(F32)
        u_ref[HALO + L:, :] = jnp.where(pos < nc - 1, nxt, 0.0)
        SLAB = 2 * LANES

        def conv_slab(j, carry):
            lo = pl.multiple_of(j * SLAB, SLAB)
            acc = jnp.broadcast_to(cb_ref[:, pl.ds(lo, SLAB)], (L, SLAB))
            for t in range(CONV_WIDTH):
                r0 = HALO - CONV_WIDTH // 2 + t
                acc = acc + cw_ref[t:t + 1, pl.ds(lo, SLAB)] * u_ref[r0:r0 + L, pl.ds(lo, SLAB)]
            act = acc * _sigmoid(acc)
            act_ref[:, pl.ds(lo, SLAB)] = act
            act_out_ref[:, pl.ds(lo, SLAB)] = act.astype(act_out_ref.dtype)
            return carry

        lax.fori_loop(0, CC // SLAB, conv_slab, 0)

        def chan(lo, hi):
            return act_ref[:, lo:hi]

    def softplus(v):
        return jnp.maximum(v, 0.0) + jnp.log1p(jnp.exp(-jnp.abs(v)))

    dtv = softplus(dt_ref[...] + dtb_ref[...])
    a = -jnp.exp(alog_ref[...]) * dtv
    a_t = -jnp.exp(alogt_ref[...]) * softplus(dtt_ref[...] + dtbt_ref[...])
    row = lax.broadcasted_iota(jnp.int32, (L, L), 0)
    col = lax.broadcasted_iota(jnp.int32, (L, L), 1)
    keep = (row <= col) if reverse else (row >= col)
    keep_t = (row >= col) if reverse else (row <= col)
    cum = jnp.dot(keep.astype(F32), a, precision=lax.Precision.HIGHEST, preferred_element_type=F32)
    cum_t = jnp.dot(a_t, keep_t.astype(F32), precision=lax.Precision.HIGHEST, preferred_element_type=F32)
    total = cum[0:1, :] if reverse else cum[L - 1:L, :]
    eye_bf = jnp.where(row == col, 1.0, 0.0).astype(BF16)
    seg = lax.broadcasted_iota(jnp.int32, (L, GW), 1) // P
    seg1 = lax.broadcasted_iota(jnp.int32, (1, GW), 1) // P

    for g in range(G):
        xs = chan(g * GW, (g + 1) * GW)
        bmat = chan(W + g * N, W + (g + 1) * N)
        cmat = chan(W + (G + g) * N, W + (G + g + 1) * N)
        b_bf = bmat.astype(BF16)
        c_bf = cmat.astype(BF16)
        cb = lax.dot_general(c_bf, b_bf, NT_DIMS, preferred_element_type=F32)
        bt_bf = lax.dot_general(eye_bf, b_bf, NT_DIMS, preferred_element_type=F32).astype(BF16)
        dt_e = jnp.zeros((L, GW), F32)
        cum_e = jnp.zeros((L, GW), F32)
        tot_e = jnp.zeros((1, GW), F32)
        for r in range(R):
            hc = col_off + g * R + r
            dt_e = jnp.where(seg == r, dtv[:, hc:hc + 1], dt_e)
            cum_e = jnp.where(seg == r, cum[:, hc:hc + 1], cum_e)
            tot_e = jnp.where(seg1 == r, total[:, hc:hc + 1], tot_e)
        x_dt = xs * dt_e
        y = jnp.zeros((L, GW), F32)
        for r in range(R):
            hc = col_off + g * R + r
            decay = jnp.exp(jnp.where(keep, cum[:, hc:hc + 1] - cum_t[hc:hc + 1, :], -jnp.inf))
            m_h = (cb * decay).astype(BF16)
            x_h = jnp.where(seg == r, x_dt, 0.0).astype(BF16)
            y = y + jnp.dot(m_h, x_h, preferred_element_type=F32)
        s_prev = state_ref[g]
        y = y + jnp.dot(c_bf, s_prev.astype(BF16), preferred_element_type=F32) * jnp.exp(cum_e)
        x_end = (x_dt * jnp.exp(tot_e - cum_e)).astype(BF16)
        state_ref[g] = s_prev * jnp.exp(tot_e) + jnp.dot(bt_bf, x_end, preferred_element_type=F32)
        if final:
            y = y + yb_ref[:, g * GW:(g + 1) * GW].astype(F32) + dskip_ref[:, g * GW:(g + 1) * GW] * xs
            zg = z_ref[:, g * GW:(g + 1) * GW].astype(F32)
            y = y * (zg * _sigmoid(zg))
            ms = jnp.mean(y * y, axis=-1, keepdims=True)
            y = y * lax.rsqrt(ms + EPS) * nw_ref[:, g * GW:(g + 1) * GW]
        o_ref[:, g * GW:(g + 1) * GW] = y.astype(o_ref.dtype)


def ssd_pass(xbc, dt, dt_t, conv_w, conv_b, dt_bias, a_log, extra, *, batch, seq, heads, reverse, final, col_off):
    L = CHUNK
    W = heads * SSM_HEAD_DIM
    CC = W + 2 * SSM_GROUPS * SSM_STATE
    GW = W // SSM_GROUPS
    nc = seq // L
    nhalo = batch * seq // BF16_ROWS
    per = L // BF16_ROWS

    def rb(b, c):
        return b * nc + ((nc - 1 - c) if reverse else c)

    full = lambda shape: pl.BlockSpec(shape, lambda b, c: (0,) * len(shape))
    rows = lambda w: pl.BlockSpec((L, w), lambda b, c: (rb(b, c), 0))
    cols = pl.BlockSpec((LANES, L), lambda b, c: (0, rb(b, c)))
    state = pltpu.VMEM((SSM_GROUPS, SSM_STATE, GW), F32)
    body = functools.partial(_ssd_body, reverse=reverse, final=final, heads=heads, col_off=col_off)
    t = batch * seq
    head_specs = [full((1, LANES)), full((1, LANES)), full((LANES, 1)), full((LANES, 1))]
    head_args = [dt_bias, a_log, dt_bias.reshape(LANES, 1), a_log.reshape(LANES, 1)]
    if final:
        in_specs = [rows(W), rows(CC), rows(LANES), cols] + head_specs + [full((1, W)), full((1, W)), rows(W)]
        args = [extra["z"], extra["act"], dt, dt_t] + head_args + [extra["d_skip"], extra["norm_w"], extra["y_other"]]
        out_specs = rows(W)
        out_shape = jax.ShapeDtypeStruct((t, W), BF16)
        scratch = [state]
    else:
        in_specs = [
            rows(CC),
            pl.BlockSpec((BF16_ROWS, CC), lambda b, c: (jnp.maximum(rb(b, c) * per - 1, 0), 0)),
            pl.BlockSpec((BF16_ROWS, CC), lambda b, c: (jnp.minimum((rb(b, c) + 1) * per, nhalo - 1), 0)),
            rows(LANES), cols,
            full((CONV_WIDTH, CC)), full((1, CC)),
        ] + head_specs
        args = [xbc, xbc, xbc, dt, dt_t, conv_w, conv_b] + head_args
        out_specs = [rows(W), rows(CC)]
        out_shape = [jax.ShapeDtypeStruct((t, W), BF16), jax.ShapeDtypeStruct((t, CC), BF16)]
        scratch = [pltpu.VMEM((L + 2 * SUBLANES, CC), F32), pltpu.VMEM((L, CC), F32), state]
    return pl.pallas_call(
        body,
        grid=(batch, nc),
        in_specs=in_specs,
        out_specs=out_specs,
        out_shape=out_shape,
        scratch_shapes=scratch,
        compiler_params=_params("parallel", "arbitrary"),
        name="ssd_final" if final else "ssd_scan",
    )(*args)


def _router_body(x_ref, nw_ref, wr_ref, br_ref, h_ref, idx_ref, gate_ref, cnt_ref, run_ref):
    tm, d = x_ref.shape
    half = d // 2

    @pl.when(pl.program_id(0) == 0)
    def _():
        run_ref[...] = jnp.zeros(run_ref.shape, F32)

    x = x_ref[...]
    ms = jnp.mean(x * x, axis=-1, keepdims=True)
    h = x * lax.rsqrt(ms + EPS) * nw_ref[...]
    h_ref[...] = _pack_pairs(h[:, :half], h[:, half:])
    h_hi = h.astype(BF16)
    h_lo = (h - h_hi.astype(F32)).astype(BF16)
    hw = jnp.dot(h_hi, wr_ref[...], preferred_element_type=F32)
    logits = (hw[:, :LANES] + (hw[:, LANES:] + jnp.dot(h_lo, wr_ref[:, :LANES], preferred_element_type=F32))
              + br_ref[...])
    lane = lax.broadcasted_iota(jnp.int32, logits.shape, 1).astype(F32)
    ninf = -jnp.inf
    is_g = lane < N_EXPERT_GROUPS
    gl = jnp.where(is_g, logits, ninf)
    gmax = jnp.max(gl, axis=-1, keepdims=True)
    gsum = jnp.sum(jnp.where(is_g, jnp.exp(gl - gmax), 0.0), axis=-1, keepdims=True)
    p_group = 1.0 / gsum
    g_sel = jnp.min(jnp.where(gl == gmax, lane, float(LANES)), axis=-1, keepdims=True)
    lo = N_EXPERT_GROUPS + EXPERTS_PER_GROUP * g_sel
    el = jnp.where((lane >= lo) & (lane < lo + EXPERTS_PER_GROUP), logits, ninf)
    m1 = jnp.max(el, axis=-1, keepdims=True)
    i1 = jnp.min(jnp.where(el == m1, lane, float(LANES)), axis=-1, keepdims=True)
    el2 = jnp.where(lane == i1, ninf, el)
    m2 = jnp.max(el2, axis=-1, keepdims=True)
    i2 = jnp.min(jnp.where(el2 == m2, lane, float(LANES)), axis=-1, keepdims=True)
    r = jnp.exp(m2 - m1)
    g1 = p_group / (1.0 + r)
    g2 = p_group * r / (1.0 + r)
    e1 = i1 - N_EXPERT_GROUPS
    e2 = i2 - N_EXPERT_GROUPS
    oh1 = jnp.where(lane == e1, 1.0, 0.0)
    oh2 = jnp.where(lane == e2, 1.0, 0.0)
    both = oh1 + oh2
    ri = lax.broadcasted_iota(jnp.int32, (tm, tm), 0)
    ci = lax.broadcasted_iota(jnp.int32, (tm, tm), 1)
    earlier = jnp.where(ci < ri, 1.0, 0.0).astype(BF16)
    before = jnp.dot(earlier, both.astype(BF16), preferred_element_type=F32) + run_ref[...]
    rank1 = jnp.sum(oh1 * before, axis=-1, keepdims=True)
    rank2 = jnp.sum(oh2 * before, axis=-1, keepdims=True)
    run_ref[...] = run_ref[...] + jnp.sum(both, axis=0, keepdims=True)
    cnt_ref[...] = run_ref[...]
    packed = jnp.where(lane == 0, e1, jnp.where(lane == 1, e2, jnp.where(lane == 2, rank1, jnp.where(lane == 3, rank2, 0.0))))
    idx_ref[...] = packed.astype(jnp.int32)
    gate_ref[...] = jnp.where(lane == 0, g1, jnp.where(lane == 1, g2, 0.0))


def norm_and_route(x, norm_w, w_router, b_router, tm):
    t, d = x.shape
    row = lambda w: pl.BlockSpec((tm, w), lambda i: (i, 0))
    one = lambda w: pl.BlockSpec((1, w), lambda i: (0, 0))
    w_hi = w_router.astype(BF16)
    w_router = jnp.concatenate([w_hi, (w_router - w_hi.astype(F32)).astype(BF16)], axis=1)
    return pl.pallas_call(
        _router_body,
        grid=(t // tm,),
        in_specs=[row(d), one(d), pl.BlockSpec((d, 2 * LANES), lambda i: (0, 0)), one(LANES)],
        out_specs=[row(d // 2), row(LANES), row(LANES), one(LANES)],
        out_shape=[jax.ShapeDtypeStruct((t, d // 2), U32), jax.ShapeDtypeStruct((t, LANES), jnp.int32),
                   jax.ShapeDtypeStruct((t, LANES), F32), jax.ShapeDtypeStruct((1, LANES), F32)],
        scratch_shapes=[pltpu.VMEM((1, LANES), F32)],
        compiler_params=_params("arbitrary"),
        name="router",
    )(x, norm_w.reshape(1, d).astype(F32), w_router, b_router)


def _moe_plan(idx, counts, tm):
    t = idx.shape[0]
    n = t * TOP_K
    experts = jnp.arange(N_EXPERTS, dtype=I32)
    counts = counts[0, :N_EXPERTS].astype(I32)
    padded = (counts + tm - 1) // tm * tm
    pend = jnp.cumsum(padded)
    pstart = pend - padded
    e = idx[:, :TOP_K]
    rank = idx[:, TOP_K:2 * TOP_K]
    dest = rank + jnp.sum(jnp.where(e[:, :, None] == experts, pstart, 0), axis=-1)
    nblk = n // tm + N_EXPERTS
    n_used = pend[-1] // tm
    blk = jnp.arange(nblk, dtype=I32)
    blk_e = jnp.minimum(jnp.sum((pend[None, :] <= (blk * tm)[:, None]).astype(I32), axis=-1), N_EXPERTS - 1)
    last_e = jnp.sum(jnp.where(blk == n_used - 1, blk_e, 0))
    blk_e = jnp.where(blk < n_used, blk_e, last_e)
    later = (counts[None, :] > 0) & (experts[None, :] > experts[:, None])
    next_of = jnp.min(jnp.where(later, experts[None, :], N_EXPERTS), axis=1)
    next_of = jnp.where(next_of < N_EXPERTS, next_of, -1)
    blk_next = jnp.sum(jnp.where(blk_e[:, None] == experts, next_of[None, :], 0), axis=1)
    return (dest.astype(I32), pend.astype(I32), blk_e.astype(I32), blk_next.astype(I32),
            n_used.reshape(1).astype(I32), nblk * tm)


def _dispatch_body(pend_ref, nu_ref, dest_ref, h_ref, xs_hbm, idx_smem, zero_ref, sem, isem, *, tt, tm):
    i = pl.program_id(0)

    @pl.when(i == 0)
    def _():
        zero_ref[...] = jnp.zeros(zero_ref.shape, zero_ref.dtype)

        def zero_block(row0):
            cp = pltpu.make_async_copy(zero_ref, xs_hbm.at[pl.ds(pl.multiple_of(row0, tm), tm), :], sem)
            cp.start()
            cp.wait()

        def fill(e, c):
            end = pend_ref[e]
            prev = jnp.where(e > 0, pend_ref[jnp.maximum(e - 1, 0)], 0)

            @pl.when(end > prev)
            def _():
                zero_block(end - tm)
            return c

        lax.fori_loop(0, N_EXPERTS, fill, 0)

        def fill_unused(b, c):
            zero_block(b * tm)
            return c

        lax.fori_loop(nu_ref[0], xs_hbm.shape[0] // tm, fill_unused, 0)

    cp = pltpu.make_async_copy(dest_ref.at[0, 0], idx_smem, isem)
    cp.start()
    cp.wait()

    def issue(r, c):
        for k in range(TOP_K):
            d = idx_smem[TOP_K * r + k]
            pltpu.make_async_copy(h_ref.at[pl.ds(r, 1), :], xs_hbm.at[pl.ds(d, 1), :], sem).start(priority=k)
        return c

    lax.fori_loop(0, tt, issue, 0, unroll=8)
    all_rows = xs_hbm.at[pl.ds(0, TOP_K * tt), :]
    pltpu.make_async_copy(all_rows, all_rows, sem).wait()


def dispatch_rows(pend, n_used, dest, h_packed, rows, tt, tm):
    t, w = h_packed.shape
    return pl.pallas_call(
        functools.partial(_dispatch_body, tt=tt, tm=tm),
        grid_spec=pltpu.PrefetchScalarGridSpec(
            num_scalar_prefetch=2,
            grid=(t // tt,),
            in_specs=[
                pl.BlockSpec((1, 1, TOP_K * tt), lambda i, pe, nu: (i, 0, 0)),
                pl.BlockSpec((tt, w), lambda i, pe, nu: (i, 0)),
            ],
            out_specs=pl.BlockSpec(memory_space=pl.ANY),
            scratch_shapes=[
                pltpu.SMEM((TOP_K * tt,), I32),
                pltpu.VMEM((tm, w), U32),
                pltpu.SemaphoreType.DMA,
                pltpu.SemaphoreType.DMA,
            ],
        ),
        out_shape=jax.ShapeDtypeStruct((rows, w), U32),
        compiler_params=_params("arbitrary"),
        name="dispatch_rows",
    )(pend, n_used, dest.reshape(t // tt, 1, TOP_K * tt), h_packed)


def _expert_changed(be_ref, i):
    return (i == 0) | (be_ref[i] != be_ref[jnp.maximum(i - 1, 0)])


CAST_ROWS = 256


def _cast_rows(src_ref, dst_ref):
    def body(r, carry):
        rows = pl.ds(pl.multiple_of(r * CAST_ROWS, CAST_ROWS), CAST_ROWS)
        dst_ref[rows, :] = src_ref[rows, :].astype(dst_ref.dtype)
        return carry
    lax.fori_loop(0, src_ref.shape[0] // CAST_ROWS, body, 0)


def _stream_weights(be_ref, nx_ref, i, sweep, n_sweeps, copies, on_ready, cnt_ref):
    @pl.when((sweep == 0) & (i == 0))
    def _():
        cnt_ref[0] = 0
        for cp in copies(be_ref[0], 0, 0):
            cp.start()

    @pl.when(_expert_changed(be_ref, i))
    def _():
        slot = cnt_ref[0] & 1
        for cp in copies(be_ref[i], sweep, slot):
            cp.wait()
        on_ready(slot)
        nxt = nx_ref[i]

        @pl.when(nxt >= 0)
        def _():
            for cp in copies(nxt, sweep, 1 - slot):
                cp.start()

        @pl.when((nxt < 0) & (sweep + 1 < n_sweeps))
        def _():
            for cp in copies(be_ref[0], sweep + 1, 1 - slot):
                cp.start()

        cnt_ref[0] = cnt_ref[0] + 1


def _gate_up_body(be_ref, nx_ref, nu_ref, x_ref, wg_hbm, wu_hbm, o_ref, wbuf_ref, wgb_ref, wub_ref, sem_ref, cnt_ref,
                  *, tf):
    j = pl.program_id(0)
    i = pl.program_id(1)
    half = x_ref.shape[1]

    def copies(e, jj, slot):
        cols = pl.ds(pl.multiple_of(jj * tf, tf), tf)
        return (pltpu.make_async_copy(wg_hbm.at[e, :, cols], wbuf_ref.at[slot, 0], sem_ref.at[slot]),
                pltpu.make_async_copy(wu_hbm.at[e, :, cols], wbuf_ref.at[slot, 1], sem_ref.at[slot]))

    def on_ready(slot):
        _cast_rows(wbuf_ref.at[slot, 0], wgb_ref)
        _cast_rows(wbuf_ref.at[slot, 1], wub_ref)

    _stream_weights(be_ref, nx_ref, i, j, pl.num_programs(0), copies, on_ready, cnt_ref)

    @pl.when(i < nu_ref[0])
    def _():
        x_lo, x_hi = _unpack_pairs(x_ref[...])
        x_lo = x_lo.astype(BF16)
        x_hi = x_hi.astype(BF16)

        def mm(w_ref):
            return (jnp.dot(x_lo, w_ref[:half, :], preferred_element_type=F32)
                    + jnp.dot(x_hi, w_ref[half:, :], preferred_element_type=F32))

        a = mm(wgb_ref)
        b = mm(wub_ref)
        o_ref[...] = (a * _sigmoid(a) * b).astype(o_ref.dtype)

    @pl.when(i >= nu_ref[0])
    def _():
        o_ref[...] = jnp.zeros(o_ref.shape, o_ref.dtype)


def expert_gate_up(blk_e, blk_next, n_used, xs, w_gate, w_up, tm, tf):
    rows = xs.shape[0]
    d, f = w_gate.shape[1], w_gate.shape[2]
    nblk = rows // tm
    last = lambda i, nu: jnp.minimum(i, nu[0] - 1)
    hbm = pl.BlockSpec(memory_space=pl.ANY)
    return pl.pallas_call(
        functools.partial(_gate_up_body, tf=tf),
        grid_spec=pltpu.PrefetchScalarGridSpec(
            num_scalar_prefetch=3,
            grid=(f // tf, nblk),
            in_specs=[pl.BlockSpec((tm, d // 2), lambda j, i, be, nx, nu: (last(i, nu), 0)), hbm, hbm],
            out_specs=pl.BlockSpec((tm, tf), lambda j, i, be, nx, nu: (i, j)),
            scratch_shapes=[
                pltpu.VMEM((2, 2, d, tf), F32),
                pltpu.VMEM((d, tf), BF16),
                pltpu.VMEM((d, tf), BF16),
                pltpu.SemaphoreType.DMA((2,)),
                pltpu.SMEM((1,), I32),
            ],
        ),
        out_shape=jax.ShapeDtypeStruct((rows, f), BF16),
        compiler_params=_params("arbitrary", "arbitrary"),
        name="expert_gate_up",
    )(blk_e, blk_next, n_used, xs, w_gate, w_up)


def _down_body(be_ref, nx_ref, nu_ref, a_ref, wd_hbm, o_ref, wbuf_ref, wdb_ref, sem_ref, cnt_ref):
    i = pl.program_id(0)
    half = o_ref.shape[1]

    def copies(e, sweep, slot):
        return (pltpu.make_async_copy(wd_hbm.at[e], wbuf_ref.at[slot], sem_ref.at[slot]),)

    def on_ready(slot):
        _cast_rows(wbuf_ref.at[slot], wdb_ref)

    _stream_weights(be_ref, nx_ref, i, 0, 1, copies, on_ready, cnt_ref)

    @pl.when(i < nu_ref[0])
    def _():
        y = jnp.dot(a_ref[...], wdb_ref[...], preferred_element_type=F32)
        o_ref[...] = _pack_pairs(y[:, :half], y[:, half:])

    @pl.when(i >= nu_ref[0])
    def _():
        o_ref[...] = jnp.zeros(o_ref.shape, o_ref.dtype)


def expert_down(blk_e, blk_next, n_used, act, w_down, tm):
    rows, f = act.shape
    d = w_down.shape[2]
    nblk = rows // tm
    last = lambda i, nu: jnp.minimum(i, nu[0] - 1)
    return pl.pallas_call(
        _down_body,
        grid_spec=pltpu.PrefetchScalarGridSpec(
            num_scalar_prefetch=3,
            grid=(nblk,),
            in_specs=[
                pl.BlockSpec((tm, f), lambda i, be, nx, nu: (last(i, nu), 0)),
                pl.BlockSpec(memory_space=pl.ANY),
            ],
            out_specs=pl.BlockSpec((tm, d // 2), lambda i, be, nx, nu: (i, 0)),
            scratch_shapes=[
                pltpu.VMEM((2, f, d), F32),
                pltpu.VMEM((f, d), BF16),
                pltpu.SemaphoreType.DMA((2,)),
                pltpu.SMEM((1,), I32),
            ],
        ),
        out_shape=jax.ShapeDtypeStruct((rows, d // 2), U32),
        compiler_params=_params("arbitrary"),
        name="expert_down",
    )(blk_e, blk_next, n_used, act, w_down)


def _combine_body(dest_ref, x_ref, g_ref, w_ref, y_hbm, o_ref, idx_smem, buf_ref, sem, isem, *, tt):
    d = x_ref.shape[1]
    half = d // 2
    cp = pltpu.make_async_copy(dest_ref.at[0, 0], idx_smem, isem)
    cp.start()
    cp.wait()

    def issue(r, c):
        for k in range(TOP_K):
            src = idx_smem[TOP_K * r + k]
            pltpu.make_async_copy(y_hbm.at[pl.ds(src, 1), :], buf_ref.at[k, pl.ds(r, 1), :], sem).start(priority=k)
        return c

    lax.fori_loop(0, tt, issue, 0, unroll=8)
    pltpu.make_async_copy(buf_ref, buf_ref, sem).wait()

    y0_lo, y0_hi = _unpack_pairs(buf_ref[0])
    y1_lo, y1_hi = _unpack_pairs(buf_ref[1])
    g0 = g_ref[:, 0:1]
    g1 = g_ref[:, 1:2]
    x_lo = x_ref[:, :half] + (g0 * y0_lo + g1 * y1_lo)
    x_hi = x_ref[:, half:] + (g0 * y0_hi + g1 * y1_hi)
    ms = (jnp.sum(x_lo * x_lo, axis=-1, keepdims=True) + jnp.sum(x_hi * x_hi, axis=-1, keepdims=True)) / d
    inv = lax.rsqrt(ms + EPS)
    o_ref[:, :half] = x_lo * inv * w_ref[:, :half]
    o_ref[:, half:] = x_hi * inv * w_ref[:, half:]


def combine_and_norm(x, y_packed, dest, gate, w, tt):
    t, d = x.shape
    row = lambda wd: pl.BlockSpec((tt, wd), lambda i: (i, 0))
    return pl.pallas_call(
        functools.partial(_combine_body, tt=tt),
        grid=(t // tt,),
        in_specs=[
            pl.BlockSpec((1, 1, TOP_K * tt), lambda i: (i, 0, 0)),
            row(d), row(LANES), pl.BlockSpec((1, d), lambda i: (0, 0)),
            pl.BlockSpec(memory_space=pl.ANY),
        ],
        out_specs=row(d),
        out_shape=jax.ShapeDtypeStruct((t, d), F32),
        scratch_shapes=[
            pltpu.SMEM((TOP_K * tt,), I32),
            pltpu.VMEM((TOP_K, tt, d // 2), U32),
            pltpu.SemaphoreType.DMA,
            pltpu.SemaphoreType.DMA,
        ],
        compiler_params=_params("arbitrary"),
        name="combine_norm",
    )(dest.reshape(t // tt, 1, TOP_K * tt), x, gate, w.reshape(1, d).astype(F32), y_packed)


def _tiles(batch, seq):
    t = batch * seq
    return dict(
        norm_tm=min(512, t),
        mm_tm=min(1024, t),
        mm_tn=1024,
        out_tm=min(512, t),
        out_tn=1024,
        attn_tq=min(1024, seq),
        attn_tk=min(512, seq),
        moe_tm=min(512, t),
        moe_tf=512,
        disp_tt=min(1024, t),
        comb_tt=min(512, t),
    )


def _lambda_init_at(layer):
    return 0.8 - 0.6 * math.exp(-0.3 * layer)


def kernel(x, rel_bias, norm1_w, w_in, lambda_q1, lambda_k1, lambda_q2, lambda_k2, subln_w, conv_w, conv_b,
           dt_bias_f, dt_bias_b, a_log_f, a_log_b, d_skip, ssm_norm_w, w_out, norm2_w, w_group_router,
           b_group_router, w_expert_router, b_expert_router, w_gate, w_up, w_down, final_norm_w):
    batch, seq, d = x.shape
    t = batch * seq
    depth = norm1_w.shape[0]
    attn_w = d // 2
    ssm_w = d - attn_w
    dv = 2 * ATTN_HEAD_DIM
    a_heads = attn_w // dv
    s_heads = ssm_w // SSM_HEAD_DIM
    cc = ssm_w + 2 * SSM_GROUPS * SSM_STATE
    main_w = 3 * attn_w + ssm_w + cc
    assert ssm_w == attn_w and cc == 2 * ssm_w and 2 * s_heads <= LANES
    tl = _tiles(batch, seq)
    log2e = math.log2(math.e)
    band = _band_table(rel_bias, tl["attn_tq"], tl["attn_tk"], log2e)
    q_scale = jnp.where(jnp.arange(w_in.shape[2]) < attn_w, ATTN_HEAD_DIM ** -0.5 * log2e, 1.0).astype(F32)

    def pad_lanes(v):
        return jnp.pad(v.astype(F32), (0, LANES - v.shape[0])).reshape(1, LANES)

    xf = x.reshape(t, d)
    for layer in range(depth):
        lam_init = _lambda_init_at(layer)
        w_t = (jnp.transpose(w_in[layer]) * q_scale[:, None]).astype(BF16)
        w_dt = jnp.pad(w_t[main_w:], ((0, LANES - 2 * s_heads), (0, 0)))
        h = rmsnorm_rows(xf, norm1_w[layer], BF16, tl["norm_tm"])
        tm, tn = tl["mm_tm"], tl["mm_tn"]
        qk = matmul_nt(h, w_t, BF16, t, 2 * attn_w, tm, tn, "in_proj_qk")
        v_t = matmul_nt(w_t, h, BF16, attn_w, t, tn, tm, "in_proj_v", a_blk0=2 * attn_w // tn)
        z, dt, dt_t = projection_with_dt(h, w_t, w_dt, BF16, t, ssm_w, tm, tn, "in_proj_z_dt",
                                         b_blk0=3 * attn_w // tn)
        xbc = matmul_nt(h, w_t, BF16, t, cc, tm, tn, "in_proj_xbc", b_blk0=(3 * attn_w + ssm_w) // tn)

        vec = lambda v: v[layer].reshape(1, -1).astype(F32)
        attn = diff_attention(qk, v_t, band, vec(lambda_q1), vec(lambda_k1), vec(lambda_q2), vec(lambda_k2),
                              subln_w[layer].reshape(dv, 1).astype(F32), batch=batch, seq=seq, heads=a_heads,
                              lam_init=lam_init, tq=tl["attn_tq"], tk=tl["attn_tk"])

        dt_bias = pad_lanes(jnp.concatenate([dt_bias_f[layer], dt_bias_b[layer]]))
        a_log = pad_lanes(jnp.concatenate([a_log_f[layer], a_log_b[layer]]))
        cw = conv_w[layer].astype(F32)
        cb = conv_b[layer].reshape(1, cc).astype(F32)
        common = dict(batch=batch, seq=seq, heads=s_heads)
        y_bwd, act = ssd_pass(xbc, dt, dt_t, cw, cb, dt_bias, a_log, None, reverse=True, final=False,
                              col_off=s_heads, **common)
        extra = dict(d_skip=jnp.repeat(d_skip[layer].astype(F32), SSM_HEAD_DIM).reshape(1, ssm_w),
                     norm_w=vec(ssm_norm_w), y_other=y_bwd, act=act, z=z)
        ssm = ssd_pass(None, dt, dt_t, cw, cb, dt_bias, a_log, extra, reverse=False, final=True, col_off=0, **common)

        x1 = out_projection(attn, ssm, w_out[layer].astype(BF16), xf, tl["out_tm"], tl["out_tn"])

        w_router = jnp.pad(jnp.concatenate([w_group_router[layer], w_expert_router[layer]], axis=1).astype(F32),
                           ((0, 0), (0, LANES - N_EXPERT_GROUPS - N_EXPERTS)))
        b_router = pad_lanes(jnp.concatenate([b_group_router[layer], b_expert_router[layer]]))
        h2, idx, gate, counts = norm_and_route(x1, norm2_w[layer], w_router, b_router, tl["norm_tm"])
        tm = tl["moe_tm"]
        dest, pend, blk_e, blk_next, n_used, rows = _moe_plan(idx, counts, tm)
        xs = dispatch_rows(pend, n_used, dest, h2, rows, tl["disp_tt"], tm)
        act = expert_gate_up(blk_e, blk_next, n_used, xs, w_gate[layer], w_up[layer], tm, tl["moe_tf"])
        y = expert_down(blk_e, blk_next, n_used, act, w_down[layer], tm)
        if layer + 1 < depth:
            raise NotImplementedError("multi-layer stacking needs an un-normalised combine")
        xf = combine_and_norm(x1, y, dest, gate, final_norm_w, tl["comb_tt"])
    return xf.reshape(batch, seq, d)
```
